```python
import jax, jax.numpy as jnp
from jax import lax
import numpy as np

D_MODEL = 1024
BATCH = 8
SEQ = 2048
DEPTH = 4

N_MIXERS = 3
D_RNN = D_MODEL
LRU_BLOCKS = 8
LRU_BW = D_RNN // LRU_BLOCKS
CONV_WIDTH = 4
LRU_C = 8.0
POOL_WINDOWS = (2, 4, 8, 16)
POOL_GROUPS = len(POOL_WINDOWS)
POOL_GW = D_MODEL // POOL_GROUPS
FOX_HEADS = 16
FOX_HEAD_DIM = D_MODEL // FOX_HEADS
Q_BLOCK = 128
D_FF = 7 * D_MODEL // 2
N_EXPERTS = 8
TOP_K = 2
MOE_ROW_BLOCK = 256
LN_EPS = 1e-5
NEG_INF = -1e30
ALPHA = (2 * DEPTH) ** 0.25
BETA = (8 * DEPTH) ** -0.25
N_A = (DEPTH + 2) // 3
N_B = (DEPTH + 1) // 3
N_C = DEPTH // 3
N_DENSE = (DEPTH + 1) // 2
N_MOE = DEPTH // 2

kernel_name = "hybrid_rglru_pool_fox_moe_deepnorm"


def layer_norm(x, g, b):
    xf = x.astype(jnp.float32)
    mu = jnp.mean(xf, axis=-1, keepdims=True)
    var = jnp.mean(jnp.square(xf - mu), axis=-1, keepdims=True)
    y = (xf - mu) * lax.rsqrt(var + LN_EPS)
    return (y * g.astype(jnp.float32) + b.astype(jnp.float32)).astype(x.dtype)


def causal_dwconv(x, w, b):
    S = x.shape[1]
    xp = jnp.pad(x, ((0, 0), (CONV_WIDTH - 1, 0), (0, 0)))
    y = b
    for k in range(CONV_WIDTH):
        y = y + xp[:, k:k + S] * w[k]
    return y


def _lru_combine(c1, c2):
    a1, b1 = c1
    a2, b2 = c2
    return a1 * a2, a2 * b1 + b2


def rglru_block(x, w_in, conv_w, conv_b, w_gates, b_gates, lam, w_out):
    B, S, _ = x.shape
    u = x @ w_in
    gate_branch, rec_branch = jnp.split(u, 2, axis=-1)
    y_gate = jax.nn.gelu(gate_branch, approximate=True)
    xr = causal_dwconv(rec_branch, conv_w, conv_b)
    xb = xr.reshape(B, S, LRU_BLOCKS, LRU_BW)
    z = jnp.einsum('bsnc,ncg->bsng', xb, w_gates).astype(jnp.float32) + b_gates.astype(jnp.float32)
    r = jax.nn.sigmoid(z[..., :LRU_BW])
    i = jax.nn.sigmoid(z[..., LRU_BW:])
    lam_b = lam.astype(jnp.float32).reshape(LRU_BLOCKS, LRU_BW)
    log_a = -LRU_C * r * jax.nn.softplus(-lam_b)
    a = jnp.exp(log_a)
    mult = jnp.sqrt(-jnp.expm1(2.0 * log_a))
    bterm = mult * (i * xb.astype(jnp.float32))
    a = a.reshape(B, S, D_RNN)
    bterm = bterm.reshape(B, S, D_RNN)
    _, h = lax.associative_scan(_lru_combine, (a, bterm), axis=1)
    return (y_gate * h.astype(x.dtype)) @ w_out


def pool_mixer(x, w, scale):
    B, S, D = x.shape
    xf = x.astype(jnp.float32)
    cs = jnp.cumsum(xf, axis=1)
    t = jnp.arange(1, S + 1, dtype=jnp.float32)[:, None]
    outs = []
    for g, wl in enumerate(POOL_WINDOWS):
        csg = cs[..., g * POOL_GW:(g + 1) * POOL_GW]
        prev = jnp.pad(csg, ((0, 0), (wl, 0), (0, 0)))[:, :S]
        mean = (csg - prev) / jnp.minimum(t, float(wl))
        outs.append(mean - xf[..., g * POOL_GW:(g + 1) * POOL_GW])
    p = jnp.stack(outs, axis=2).astype(x.dtype)
    y = jnp.einsum('bsgc,gcd->bsgd', p, w).reshape(B, S, D)
    return y * scale


def fox_attention(x, w_qkvf, b_f, w_o):
    B, S, D = x.shape
    proj = x @ w_qkvf
    q = proj[..., :D].reshape(B, S, FOX_HEADS, FOX_HEAD_DIM).transpose(0, 2, 1, 3)
    k = proj[..., D:2 * D].reshape(B, S, FOX_HEADS, FOX_HEAD_DIM).transpose(0, 2, 1, 3)
    v = proj[..., 2 * D:3 * D].reshape(B, S, FOX_HEADS, FOX_HEAD_DIM).transpose(0, 2, 1, 3)
    log_f = jax.nn.log_sigmoid(proj[..., 3 * D:].astype(jnp.float32) + b_f.astype(jnp.float32))
    F = jnp.cumsum(log_f, axis=1).transpose(0, 2, 1)
    nq = S // Q_BLOCK
    qb = q.reshape(B, FOX_HEADS, nq, Q_BLOCK, FOX_HEAD_DIM).transpose(2, 0, 1, 3, 4)
    Fq = F.reshape(B, FOX_HEADS, nq, Q_BLOCK).transpose(2, 0, 1, 3)
    qpos = jnp.arange(S, dtype=jnp.int32).reshape(nq, Q_BLOCK)
    kpos = jnp.arange(S, dtype=jnp.int32)
    scale = FOX_HEAD_DIM ** -0.5

    def block(args):
        qi, fi, pi = args
        s = jnp.einsum('bhqd,bhkd->bhqk', qi, k, preferred_element_type=jnp.float32) * scale
        s = s + fi[..., None] - F[:, :, None, :]
        s = jnp.where(kpos[None, :] <= pi[:, None], s, NEG_INF)
        p = jax.nn.softmax(s, axis=-1).astype(v.dtype)
        return jnp.einsum('bhqk,bhkd->bhqd', p, v)

    o = lax.map(block, (qb, Fq, qpos))
    o = o.transpose(1, 0, 3, 2, 4).reshape(B, S, D)
    return o @ w_o


def swiglu(x, w_gu, w_down):
    g, u = jnp.split(x @ w_gu, 2, axis=-1)
    return (jax.nn.silu(g) * u) @ w_down


def moe_swiglu(x, w_router, w_gu, w_down):
    B, S, D = x.shape
    T = B * S
    TK = T * TOP_K
    xt = x.reshape(T, D)
    logits = jnp.einsum('td,de->te', xt, w_router, preferred_element_type=jnp.float32)
    top_val, top_idx = lax.top_k(logits, TOP_K)
    gates = jax.nn.softmax(top_val, axis=-1)
    e_flat = top_idx.reshape(-1).astype(jnp.int32)
    g_flat = gates.reshape(-1)
    tok_flat = jnp.arange(TK, dtype=jnp.int32) // TOP_K
    order = jnp.argsort(e_flat)
    e_sorted = e_flat[order]
    tok_sorted = tok_flat[order]
    g_sorted = g_flat[order]
    counts = jnp.bincount(e_flat, length=N_EXPERTS).astype(jnp.int32)
    padded = (counts + MOE_ROW_BLOCK - 1) // MOE_ROW_BLOCK * MOE_ROW_BLOCK
    start = jnp.cumsum(counts) - counts
    pend = jnp.cumsum(padded)
    pstart = pend - padded
    dest = pstart[e_sorted] + jnp.arange(TK, dtype=jnp.int32) - start[e_sorted]
    n_blocks = (TK + N_EXPERTS * (MOE_ROW_BLOCK - 1) + MOE_ROW_BLOCK - 1) // MOE_ROW_BLOCK
    rows = jnp.zeros((n_blocks * MOE_ROW_BLOCK, D), x.dtype).at[dest].set(xt[tok_sorted])
    block_start = jnp.arange(n_blocks, dtype=jnp.int32) * MOE_ROW_BLOCK
    block_expert = jnp.minimum(jnp.searchsorted(pend, block_start, side='right'), N_EXPERTS - 1)

    def expert_block(args):
        xb, e = args
        return swiglu(xb, w_gu[e], w_down[e])

    y_rows = lax.map(expert_block, (rows.reshape(n_blocks, MOE_ROW_BLOCK, D), block_expert))
    y_sorted = y_rows.reshape(-1, D)[dest] * g_sorted[:, None].astype(x.dtype)
    y = jax.ops.segment_sum(y_sorted, tok_sorted, num_segments=T)
    return y.reshape(B, S, D)


def setup_inputs(seed: int = 0) -> dict:
    key = jax.random.key(seed)
    ks = jax.random.split(key, 24)
    D = D_MODEL
    f32 = jnp.float32

    def nrm(k, shape, s):
        return jax.random.normal(k, shape, f32) * s

    x = nrm(ks[0], (BATCH, SEQ, D), 1.0)
    lru_w_in = nrm(ks[1], (N_A, D, 2 * D_RNN), D ** -0.5)
    lru_conv_w = nrm(ks[2], (N_A, CONV_WIDTH, D_RNN), CONV_WIDTH ** -0.5)
    lru_conv_b = nrm(ks[3], (N_A, D_RNN), 0.02)
    lru_w_gates = nrm(ks[4], (N_A, LRU_BLOCKS, LRU_BW, 2 * LRU_BW), LRU_BW ** -0.5)
    lru_b_gates = nrm(ks[5], (N_A, LRU_BLOCKS, 2 * LRU_BW), 0.1)
    a_c = jax.random.uniform(ks[6], (N_A, D_RNN), f32, 0.9, 0.999)
    a0 = a_c ** (1.0 / LRU_C)
    lru_lambda = jnp.log(a0) - jnp.log1p(-a0)
    lru_w_out = nrm(ks[7], (N_A, D_RNN, D), D_RNN ** -0.5 * BETA)
    pool_w = nrm(ks[8], (N_B, POOL_GROUPS, POOL_GW, POOL_GW), POOL_GW ** -0.5 * BETA)
    pool_scale = 1.0 + nrm(ks[9], (N_B, D), 0.1)
    w_qk = nrm(ks[10], (N_C, D, 2 * D), D ** -0.5)
    w_v = nrm(ks[11], (N_C, D, D), D ** -0.5 * BETA)
    w_f = nrm(ks[12], (N_C, D, FOX_HEADS), D ** -0.5)
    fox_w_qkvf = jnp.concatenate([w_qk, w_v, w_f], axis=-1)
    fox_b_f = jax.random.uniform(ks[13], (N_C, FOX_HEADS), f32, 1.0, 4.0)
    fox_w_o = nrm(ks[14], (N_C, D, D), D ** -0.5 * BETA)
    ffn_w_gu = nrm(ks[15], (N_DENSE, D, 2 * D_FF), D ** -0.5)
    ffn_w_down = nrm(ks[16], (N_DENSE, D_FF, D), D_FF ** -0.5 * BETA)
    moe_router = nrm(ks[17], (N_MOE, D, N_EXPERTS), D ** -0.5)
    moe_w_gu = nrm(ks[18], (N_MOE, N_EXPERTS, D, 2 * D_FF), D ** -0.5)
    moe_w_down = nrm(ks[19], (N_MOE, N_EXPERTS, D_FF, D), D_FF ** -0.5 * BETA)
    ln_mix_g = 1.0 + nrm(ks[20], (DEPTH, D), 0.05)
    ln_mix_b = nrm(ks[21], (DEPTH, D), 0.02)
    ln_ffn_g = 1.0 + nrm(ks[22], (DEPTH, D), 0.05)
    ln_ffn_b = nrm(ks[23], (DEPTH, D), 0.02)
    return {"x": x, "lru_w_in": lru_w_in, "lru_conv_w": lru_conv_w, "lru_conv_b": lru_conv_b,
            "lru_w_gates": lru_w_gates, "lru_b_gates": lru_b_gates, "lru_lambda": lru_lambda,
            "lru_w_out": lru_w_out, "pool_w": pool_w, "pool_scale": pool_scale,
            "fox_w_qkvf": fox_w_qkvf, "fox_b_f": fox_b_f, "fox_w_o": fox_w_o,
            "ffn_w_gu": ffn_w_gu, "ffn_w_down": ffn_w_down, "moe_router": moe_router,
            "moe_w_gu": moe_w_gu, "moe_w_down": moe_w_down, "ln_mix_g": ln_mix_g,
            "ln_mix_b": ln_mix_b, "ln_ffn_g": ln_ffn_g, "ln_ffn_b": ln_ffn_b}


def reference(x, lru_w_in, lru_conv_w, lru_conv_b, lru_w_gates, lru_b_gates, lru_lambda,
              lru_w_out, pool_w, pool_scale, fox_w_qkvf, fox_b_f, fox_w_o, ffn_w_gu, ffn_w_down,
              moe_router, moe_w_gu, moe_w_down, ln_mix_g, ln_mix_b, ln_ffn_g, ln_ffn_b):
    for i in range(DEPTH):
        m = i % N_MIXERS
        j = i // N_MIXERS
        if m == 0:
            h = rglru_block(x, lru_w_in[j], lru_conv_w[j], lru_conv_b[j], lru_w_gates[j],
                            lru_b_gates[j], lru_lambda[j], lru_w_out[j])
        elif m == 1:
            h = pool_mixer(x, pool_w[j], pool_scale[j])
        else:
            h = fox_attention(x, fox_w_qkvf[j], fox_b_f[j], fox_w_o[j])
        x = layer_norm(ALPHA * x + h, ln_mix_g[i], ln_mix_b[i])
        if i % 2 == 0:
            f = swiglu(x, ffn_w_gu[i // 2], ffn_w_down[i // 2])
        else:
            f = moe_swiglu(x, moe_router[i // 2], moe_w_gu[i // 2], moe_w_down[i // 2])
        x = layer_norm(ALPHA * x + f, ln_ffn_g[i], ln_ffn_b[i])
    return x
```

```python
import functools

import jax
import jax.numpy as jnp
from jax import lax
from jax.experimental import pallas as pl
from jax.experimental.pallas import tpu as pltpu

F32 = jnp.float32
BF16 = jnp.bfloat16

DEPTH = 4
LRU_BLOCKS = 8
CONV_WIDTH = 4
LRU_C = 8.0
POOL_WINDOWS = (2, 4, 8, 16)
FOX_HEADS = 16
N_EXPERTS = 8
TOP_K = 2
LN_EPS = 1e-5
NEG_INF = -1e30
ALPHA = (2 * DEPTH) ** 0.25

V7X_LANES = 128
V7X_SUBLANES = 8
V7X_VMEM_LIMIT_BYTES = 56 * 1024 * 1024


def _params(semantics, vmem=V7X_VMEM_LIMIT_BYTES):
    return pltpu.CompilerParams(dimension_semantics=semantics, vmem_limit_bytes=vmem)


def _layer_norm(y, g, b):
    mu = jnp.mean(y, axis=-1, keepdims=True)
    yc = y - mu
    var = jnp.mean(yc * yc, axis=-1, keepdims=True)
    return yc * lax.rsqrt(var + LN_EPS) * g + b


def _dot(a, b):
    return jnp.dot(a, b, preferred_element_type=F32)


def _matmul_kernel(x_ref, w_ref, o_ref):
    o_ref[...] = _dot(x_ref[...].astype(BF16), w_ref[...]).astype(o_ref.dtype)


def _matmul(x, w, out_dtype, bm, bn):
    m, k = x.shape
    n = w.shape[1]
    return pl.pallas_call(
        _matmul_kernel,
        grid=(m // bm, n // bn),
        in_specs=[pl.BlockSpec((bm, k), lambda i, j: (i, 0)),
                  pl.BlockSpec((k, bn), lambda i, j: (0, j))],
        out_specs=pl.BlockSpec((bm, bn), lambda i, j: (i, j)),
        out_shape=jax.ShapeDtypeStruct((m, n), out_dtype),
        compiler_params=_params(("parallel", "arbitrary")),
        name="matmul",
    )(x, w)


def _mm_ln_kernel(a_ref, w_ref, x_ref, g_ref, b_ref, o_ref):
    y = _dot(a_ref[...].astype(BF16), w_ref[...])
    o_ref[...] = _layer_norm(ALPHA * x_ref[...] + y, g_ref[...], b_ref[...])


def _mm_ln(a, w, x, g, b, bm):
    m, k = a.shape
    d = w.shape[1]
    return pl.pallas_call(
        _mm_ln_kernel,
        grid=(m // bm,),
        in_specs=[pl.BlockSpec((bm, k), lambda i: (i, 0)),
                  pl.BlockSpec((k, d), lambda i: (0, 0)),
                  pl.BlockSpec((bm, d), lambda i: (i, 0)),
                  pl.BlockSpec((1, d), lambda i: (0, 0)),
                  pl.BlockSpec((1, d), lambda i: (0, 0))],
        out_specs=pl.BlockSpec((bm, d), lambda i: (i, 0)),
        out_shape=jax.ShapeDtypeStruct((m, d), F32),
        compiler_params=_params(("parallel",)),
        name="mm_ln",
    )(a, w, x, g.reshape(1, d), b.reshape(1, d))


def _silu_mul(g, u):
    return g * jax.nn.sigmoid(g) * u


def _ffn_ln_kernel(x_ref, wg_ref, wu_ref, wd_ref, g_ref, b_ref, o_ref, xb_ref, acc_ref):
    j = pl.program_id(1)

    @pl.when(j == 0)
    def _():
        xb_ref[...] = x_ref[...].astype(BF16)
        acc_ref[...] = jnp.zeros_like(acc_ref)

    xb = xb_ref[...]
    h = _silu_mul(_dot(xb, wg_ref[...]), _dot(xb, wu_ref[...])).astype(BF16)
    acc_ref[...] += _dot(h, wd_ref[...])

    @pl.when(j == pl.num_programs(1) - 1)
    def _():
        o_ref[...] = _layer_norm(ALPHA * x_ref[...] + acc_ref[...], g_ref[...], b_ref[...])


def _ffn_ln(x, w_gu, w_down, g, b, bm, fc):
    m, d = x.shape
    f = w_down.shape[0]
    nf = f // fc
    return pl.pallas_call(
        _ffn_ln_kernel,
        grid=(m // bm, nf),
        in_specs=[pl.BlockSpec((bm, d), lambda i, j: (i, 0)),
                  pl.BlockSpec((d, fc), lambda i, j: (0, j)),
                  pl.BlockSpec((d, fc), lambda i, j: (0, nf + j)),
                  pl.BlockSpec((fc, d), lambda i, j: (j, 0)),
                  pl.BlockSpec((1, d), lambda i, j: (0, 0)),
                  pl.BlockSpec((1, d), lambda i, j: (0, 0))],
        out_specs=pl.BlockSpec((bm, d), lambda i, j: (i, 0)),
        out_shape=jax.ShapeDtypeStruct((m, d), F32),
        scratch_shapes=[pltpu.VMEM((bm, d), BF16), pltpu.VMEM((bm, d), F32)],
        compiler_params=_params(("parallel", "arbitrary")),
        name="ffn_ln",
    )(x, w_gu, w_gu, w_down, g.reshape(1, d), b.reshape(1, d))


def _softplus(x):
    return jnp.maximum(x, 0.0) + jnp.log1p(jnp.exp(-jnp.abs(x)))


def _gelu_tanh(x):
    return 0.5 * x * (1.0 + jnp.tanh(0.7978845608028654 * (x + 0.044715 * (x * x * x))))


def _rglru_core_kernel(u_ref, cw_ref, cb_ref, wg_ref, bg_ref, lam_ref, o_ref,
                       tm_ref, a_ref, b_ref, hs_ref, h_ref, *, ts):
    nb = LRU_BLOCKS
    halo = (CONV_WIDTH - 1) * V7X_SUBLANES
    rows = ts * V7X_SUBLANES
    step = pl.program_id(0)

    @pl.when(step == 0)
    def _():
        h_ref[...] = jnp.zeros_like(h_ref)
        tm_ref[:, 0:halo, :] = jnp.zeros((2 * nb, halo, V7X_LANES), F32)

    @pl.when(step > 0)
    def _():
        tm_ref[:, 0:halo, :] = tm_ref[:, rows:rows + halo, :]

    for bi in range(V7X_SUBLANES):
        for c in range(2 * nb):
            tm_ref[c, pl.ds(halo + bi, ts, stride=V7X_SUBLANES), :] = u_ref[bi, :, c * V7X_LANES:(c + 1) * V7X_LANES]

    for n in range(nb):
        sl = slice(n * V7X_LANES, (n + 1) * V7X_LANES)
        xr = cb_ref[:, sl]
        for k in range(CONV_WIDTH):
            xr = xr + tm_ref[nb + n, k * V7X_SUBLANES:k * V7X_SUBLANES + rows, :] * cw_ref[k:k + 1, sl]
        z = _dot(xr.astype(BF16), wg_ref[n]) + bg_ref[n]
        r = jax.nn.sigmoid(z[:, :V7X_LANES])
        i = jax.nn.sigmoid(z[:, V7X_LANES:])
        log_a = (-LRU_C * r) * _softplus(-lam_ref[:, sl])
        a = jnp.exp(log_a)
        a_ref[n] = a
        b_ref[n] = jnp.sqrt(-jnp.tanh(log_a) * (a * a + 1.0)) * (i * xr)

    def scan_step(t, hs):
        r0 = pl.multiple_of(t * V7X_SUBLANES, V7X_SUBLANES)
        new = []
        for n in range(nb):
            h = a_ref[n, pl.ds(r0, V7X_SUBLANES), :] * hs[n] + b_ref[n, pl.ds(r0, V7X_SUBLANES), :]
            hs_ref[n, pl.ds(r0, V7X_SUBLANES), :] = h
            new.append(h)
        return tuple(new)

    hs = lax.fori_loop(0, ts, scan_step, tuple(h_ref[n] for n in range(nb)), unroll=8)
    for n in range(nb):
        h_ref[n] = hs[n]

    for n in range(nb):
        a_ref[n] = _gelu_tanh(tm_ref[n, halo:halo + rows, :]) * hs_ref[n]
        for bi in range(V7X_SUBLANES):
            o_ref[bi, :, n * V7X_LANES:(n + 1) * V7X_LANES] = (
                a_ref[n, pl.ds(bi, ts, stride=V7X_SUBLANES), :].astype(o_ref.dtype))


def _rglru_core(u, conv_w, conv_b, w_gates, b_gates, lam, ts):
    nbat, s, d2 = u.shape
    d = d2 // 2
    assert nbat == V7X_SUBLANES and d == LRU_BLOCKS * V7X_LANES and s % ts == 0
    rows = ts * V7X_SUBLANES
    halo = (CONV_WIDTH - 1) * V7X_SUBLANES
    return pl.pallas_call(
        functools.partial(_rglru_core_kernel, ts=ts),
        grid=(s // ts,),
        in_specs=[pl.BlockSpec((nbat, ts, d2), lambda t: (0, t, 0)),
                  pl.BlockSpec((CONV_WIDTH, d), lambda t: (0, 0)),
                  pl.BlockSpec((1, d), lambda t: (0, 0)),
                  pl.BlockSpec((LRU_BLOCKS, V7X_LANES, 2 * V7X_LANES), lambda t: (0, 0, 0)),
                  pl.BlockSpec((LRU_BLOCKS, 1, 2 * V7X_LANES), lambda t: (0, 0, 0)),
                  pl.BlockSpec((1, d), lambda t: (0, 0))],
        out_specs=pl.BlockSpec((nbat, ts, d), lambda t: (0, t, 0)),
        out_shape=jax.ShapeDtypeStruct((nbat, s, d), BF16),
        scratch_shapes=[pltpu.VMEM((2 * LRU_BLOCKS, halo + rows, V7X_LANES), F32),
                        pltpu.VMEM((LRU_BLOCKS, rows, V7X_LANES), F32),
                        pltpu.VMEM((LRU_BLOCKS, rows, V7X_LANES), F32),
                        pltpu.VMEM((LRU_BLOCKS, rows, V7X_LANES), F32),
                        pltpu.VMEM((LRU_BLOCKS, V7X_SUBLANES, V7X_LANES), F32)],
        compiler_params=_params(("arbitrary",)),
        name="rglru_core",
    )(u, conv_w, conv_b.reshape(1, d), w_gates.astype(BF16), b_gates.reshape(LRU_BLOCKS, 1, 2 * V7X_LANES),
      lam.reshape(1, d))


def _row_block(m, want):
    return want if m % want == 0 else m


def _rglru_layer(xt, nbat, s, w_in, conv_w, conv_b, w_gates, b_gates, lam, w_out, g, b):
    m, d = xt.shape
    bm = _row_block(m, 1024)
    u = _matmul(xt, w_in.astype(BF16), F32, bm, d)
    gh = _rglru_core(u.reshape(nbat, s, 2 * d), conv_w, conv_b, w_gates, b_gates, lam, ts=_row_block(s, 128))
    return _mm_ln(gh.reshape(m, d), w_out.astype(BF16), xt, g, b, bm)


POOL_HALO = 32


def _pool_ln_kernel(x_ref, prev_ref, w_ref, sc_ref, g_ref, b_ref, o_ref, e_ref, st_ref, y_ref, *, rc, tiles_per_seq):
    i = pl.program_id(0)
    first = (i % tiles_per_seq) == 0
    d = x_ref.shape[1]
    gw = d // len(POOL_WINDOWS)
    e_ref[0:POOL_HALO, :] = jnp.where(first, 0.0, prev_ref[...])
    e_ref[POOL_HALO:, :] = x_ref[...]
    t = ((i % tiles_per_seq) * rc + lax.broadcasted_iota(jnp.int32, (rc, 1), 0) + 1).astype(F32)
    for gi, wl in enumerate(POOL_WINDOWS):
        sl = slice(gi * gw, (gi + 1) * gw)
        lo = V7X_SUBLANES
        shift = 1
        cur = e_ref[lo:, sl] + e_ref[lo - shift:POOL_HALO + rc - shift, sl]
        while 2 * shift < wl:
            shift *= 2
            st_ref[lo:, :] = cur
            nlo = lo + V7X_SUBLANES
            cur = st_ref[nlo:, :] + st_ref[nlo - shift:POOL_HALO + rc - shift, :]
            lo = nlo
        win = cur[POOL_HALO - lo:, :]
        xg = x_ref[:, sl]
        p = win / jnp.minimum(t, float(wl)) - xg
        y_ref[:, sl] = _dot(p.astype(BF16), w_ref[gi])
    y = y_ref[...] * sc_ref[...]
    o_ref[...] = _layer_norm(ALPHA * x_ref[...] + y, g_ref[...], b_ref[...])


def _pool_layer(xt, nbat, s, pool_w, pool_scale, g, b):
    m, d = xt.shape
    rc = _row_block(s, 512)
    tiles_per_seq = s // rc
    ng = len(POOL_WINDOWS)
    gw = d // ng
    hb = rc // POOL_HALO
    return pl.pallas_call(
        functools.partial(_pool_ln_kernel, rc=rc, tiles_per_seq=tiles_per_seq),
        grid=(m // rc,),
        in_specs=[pl.BlockSpec((rc, d), lambda i: (i, 0)),
                  pl.BlockSpec((POOL_HALO, d), lambda i: (jnp.maximum(i * hb - 1, 0), 0)),
                  pl.BlockSpec((ng, gw, gw), lambda i: (0, 0, 0)),
                  pl.BlockSpec((1, d), lambda i: (0, 0)),
                  pl.BlockSpec((1, d), lambda i: (0, 0)),
                  pl.BlockSpec((1, d), lambda i: (0, 0))],
        out_specs=pl.BlockSpec((rc, d), lambda i: (i, 0)),
        out_shape=jax.ShapeDtypeStruct((m, d), F32),
        scratch_shapes=[pltpu.VMEM((POOL_HALO + rc, d), F32),
                        pltpu.VMEM((POOL_HALO + rc, gw), F32),
                        pltpu.VMEM((rc, d), F32)],
        compiler_params=_params(("parallel",)),
        name="pool_ln",
    )(xt, xt, pool_w.astype(BF16), pool_scale.reshape(1, d), g.reshape(1, d), b.reshape(1, d))


def _log_sigmoid(z):
    return -_softplus(-z)


def _logf_kernel(x_ref, w_ref, b_ref, o_ref):
    o_ref[...] = _log_sigmoid(_dot(x_ref[...].astype(BF16), w_ref[...]) + b_ref[...])


def _logf(xt, w_f, b_f, bm):
    m, d = xt.shape
    h = w_f.shape[1]
    w_pad = jnp.pad(w_f, ((0, 0), (0, V7X_LANES - h))).astype(BF16)
    b_pad = jnp.pad(b_f, (0, V7X_LANES - h)).reshape(1, V7X_LANES)
    return pl.pallas_call(
        _logf_kernel,
        grid=(m // bm,),
        in_specs=[pl.BlockSpec((bm, d), lambda i: (i, 0)),
                  pl.BlockSpec((d, V7X_LANES), lambda i: (0, 0)),
                  pl.BlockSpec((1, V7X_LANES), lambda i: (0, 0))],
        out_specs=pl.BlockSpec((bm, V7X_LANES), lambda i: (i, 0)),
        out_shape=jax.ShapeDtypeStruct((m, V7X_LANES), F32),
        compiler_params=_params(("parallel",)),
        name="fox_logf",
    )(xt, w_pad, b_pad)


CUMSUM_ROWS = 256


def _split3(v):
    hi = v.astype(BF16)
    r1 = v - hi.astype(F32)
    mid = r1.astype(BF16)
    lo = (r1 - mid.astype(F32)).astype(BF16)
    return hi, mid, lo


def _cumsum_kernel(lf_ref, f_ref, ft_ref, carry_ref):
    j = pl.program_id(1)

    @pl.when(j == 0)
    def _():
        carry_ref[...] = jnp.zeros_like(carry_ref)

    n = lf_ref.shape[0]
    row = lax.broadcasted_iota(jnp.int32, (n, n), 0)
    col = lax.broadcasted_iota(jnp.int32, (n, n), 1)
    tri = jnp.where(row >= col, 1.0, 0.0).astype(BF16)
    hi, mid, lo = _split3(lf_ref[...])
    cs = (_dot(tri, hi) + _dot(tri, mid) + _dot(tri, lo)) + carry_ref[0:1, :]
    f_ref[...] = cs
    ft_ref[...] = cs.T
    carry_ref[...] = jnp.broadcast_to(cs[n - 1:n, :], carry_ref.shape)


def _forget_cumsum(logf, nbat, s):
    n = _row_block(s, CUMSUM_ROWS)
    nj = s // n
    return pl.pallas_call(
        _cumsum_kernel,
        grid=(nbat, nj),
        in_specs=[pl.BlockSpec((n, V7X_LANES), lambda bi, j: (bi * nj + j, 0))],
        out_specs=[pl.BlockSpec((n, V7X_LANES), lambda bi, j: (bi * nj + j, 0)),
                   pl.BlockSpec((None, V7X_LANES, n), lambda bi, j: (bi, 0, j))],
        out_shape=[jax.ShapeDtypeStruct((nbat * s, V7X_LANES), F32),
                   jax.ShapeDtypeStruct((nbat, V7X_LANES, s), F32)],
        scratch_shapes=[pltpu.VMEM((V7X_SUBLANES, V7X_LANES), F32)],
        compiler_params=_params(("parallel", "arbitrary")),
        name="fox_cumsum",
    )(logf)


def _fox_attn_kernel(q_ref, k_ref, v_ref, fq_ref, fk_ref, o_ref, m_ref, l_ref, acc_ref, *, tq, dh):
    p = pl.program_id(1)
    qi = pl.program_id(2)
    scale = dh ** -0.5
    lane = lax.broadcasted_iota(jnp.int32, (tq, V7X_LANES), 1)
    q = q_ref[...]
    fq_all = fq_ref[...]
    row = lax.broadcasted_iota(jnp.int32, (tq, tq), 0)
    col = lax.broadcasted_iota(jnp.int32, (tq, tq), 1)
    outs = []
    for hh in range(2):
        in_head = (lane >= hh * dh) & (lane < (hh + 1) * dh)
        qh = jnp.where(in_head, q, jnp.zeros_like(q))
        fq = jnp.sum(jnp.where(lane == 2 * p + hh, fq_all, 0.0), axis=1, keepdims=True)
        m_ref[...] = jnp.full(m_ref.shape, NEG_INF, F32)
        l_ref[...] = jnp.zeros_like(l_ref)
        acc_ref[...] = jnp.zeros_like(acc_ref)

        def kv_step(j, diagonal, qh=qh, fq=fq, hh=hh):
            r0 = pl.multiple_of(j * tq, tq)
            k = k_ref[pl.ds(r0, tq), :]
            v = v_ref[pl.ds(r0, tq), :]
            s = lax.dot_general(qh, k, (((1,), (1,)), ((), ())), preferred_element_type=F32)
            s = (s * scale + fq) - fk_ref[hh:hh + 1, pl.ds(r0, tq)]
            if diagonal:
                s = jnp.where(col <= row, s, NEG_INF)
            m_prev = m_ref[...]
            m_new = jnp.maximum(m_prev, jnp.max(s, axis=1, keepdims=True))
            alpha = jnp.exp(m_prev - m_new)
            pr = jnp.exp(s - m_new)
            l_ref[...] = alpha * l_ref[...] + jnp.sum(pr, axis=1, keepdims=True)
            acc_ref[...] = alpha * acc_ref[...] + _dot(pr.astype(BF16), v)
            m_ref[...] = m_new

        def body(j, carry):
            kv_step(j, False)
            return carry

        lax.fori_loop(0, qi, body, 0)
        kv_step(qi, True)
        outs.append(acc_ref[...] / l_ref[...])
    o_ref[...] = jnp.where(lane < dh, outs[0], outs[1]).astype(o_ref.dtype)


def _fox_attention(qkv, f_rows, f_cols, nbat, s, d):
    dh = d // FOX_HEADS
    npair = FOX_HEADS // 2
    assert 2 * dh == V7X_LANES
    tq = _row_block(s, 512)
    nq = s // tq
    return pl.pallas_call(
        functools.partial(_fox_attn_kernel, tq=tq, dh=dh),
        grid=(nbat, npair, nq),
        in_specs=[pl.BlockSpec((tq, V7X_LANES), lambda bi, p, qi: (bi * nq + qi, p)),
                  pl.BlockSpec((s, V7X_LANES), lambda bi, p, qi: (bi, npair + p)),
                  pl.BlockSpec((s, V7X_LANES), lambda bi, p, qi: (bi, 2 * npair + p)),
                  pl.BlockSpec((tq, V7X_LANES), lambda bi, p, qi: (bi * nq + qi, 0)),
                  pl.BlockSpec((None, None, V7X_SUBLANES, s), lambda bi, p, qi: (bi, p, 0, 0))],
        out_specs=pl.BlockSpec((tq, V7X_LANES), lambda bi, p, qi: (bi * nq + qi, p)),
        out_shape=jax.ShapeDtypeStruct((nbat * s, d), BF16),
        scratch_shapes=[pltpu.VMEM((tq, 1), F32), pltpu.VMEM((tq, 1), F32), pltpu.VMEM((tq, V7X_LANES), F32)],
        compiler_params=_params(("parallel", "parallel", "arbitrary")),
        name="fox_attn",
    )(qkv, qkv, qkv, f_rows, f_cols)


def _fox_layer(xt, nbat, s, w_qkvf, b_f, w_o, g, b):
    m, d = xt.shape
    bm = _row_block(m, 1024)
    qkv = _matmul(xt, w_qkvf[:, :3 * d].astype(BF16), BF16, bm, d)
    logf = _logf(xt, w_qkvf[:, 3 * d:], b_f, bm)
    f_rows, f_t = _forget_cumsum(logf, nbat, s)
    npair = FOX_HEADS // 2
    f_cols = jnp.pad(f_t[:, :FOX_HEADS, :].reshape(nbat, npair, 2, s), ((0, 0), (0, 0), (0, V7X_SUBLANES - 2), (0, 0)))
    o = _fox_attention(qkv, f_rows, f_cols, nbat, s, d)
    return _mm_ln(o, w_o.astype(BF16), xt, g, b, bm)


MOE_ROW_BLOCK = 512
MOE_FF_CHUNK = 1792
PACK_E1, PACK_E2, PACK_G1, PACK_G2 = 0, 1, 2, 3


def _to_slabs(slab_ref, value):
    rows = value.shape[0]
    for c in range(V7X_SUBLANES):
        slab_ref[pl.ds(c, rows, stride=V7X_SUBLANES), :] = value[:, c * V7X_LANES:(c + 1) * V7X_LANES]


def _from_slabs(slab_ref, rows):
    return jnp.concatenate([slab_ref[pl.ds(c, rows, stride=V7X_SUBLANES), :] for c in range(V7X_SUBLANES)], axis=1)


def _router_kernel(x_ref, w_ref, r_ref, xs_ref):
    x = x_ref[...]
    _to_slabs(xs_ref, x)
    xh = x.astype(BF16)
    xl = (x - xh.astype(F32)).astype(BF16)
    w = w_ref[...]
    wh = w.astype(BF16)
    wl = (w - wh.astype(F32)).astype(BF16)
    logits = (_dot(xh, wh) + _dot(xl, wh)) + _dot(xh, wl)
    lane = lax.broadcasted_iota(jnp.int32, logits.shape, 1)
    logits = jnp.where(lane < N_EXPERTS, logits, NEG_INF)
    m1 = jnp.max(logits, axis=1, keepdims=True)
    i1 = jnp.min(jnp.where(logits == m1, lane, V7X_LANES), axis=1, keepdims=True)
    rest = jnp.where(lane == i1, NEG_INF, logits)
    m2 = jnp.max(rest, axis=1, keepdims=True)
    i2 = jnp.min(jnp.where(rest == m2, lane, V7X_LANES), axis=1, keepdims=True)
    e21 = jnp.exp(m2 - m1)
    g1 = 1.0 / (1.0 + e21)
    g2 = e21 * g1
    out = jnp.where(lane == PACK_E1, i1.astype(F32), 0.0)
    out = jnp.where(lane == PACK_E2, i2.astype(F32), out)
    out = jnp.where(lane == PACK_G1, g1, out)
    out = jnp.where(lane == PACK_G2, g2, out)
    r_ref[...] = out


def _router(xt, w_router, bm):
    m, d = xt.shape
    w_pad = jnp.pad(w_router, ((0, 0), (0, V7X_LANES - w_router.shape[1])))
    return pl.pallas_call(
        _router_kernel,
        grid=(m // bm,),
        in_specs=[pl.BlockSpec((bm, d), lambda i: (i, 0)),
                  pl.BlockSpec((d, V7X_LANES), lambda i: (0, 0))],
        out_specs=[pl.BlockSpec((bm, V7X_LANES), lambda i: (i, 0)),
                   pl.BlockSpec((bm * V7X_SUBLANES, V7X_LANES), lambda i: (i, 0))],
        out_shape=[jax.ShapeDtypeStruct((m, V7X_LANES), F32),
                   jax.ShapeDtypeStruct((m * V7X_SUBLANES, V7X_LANES), F32)],
        compiler_params=_params(("parallel",)),
        name="moe_router",
    )(xt, w_pad)


def _plan_kernel(r_ref, rank_ref, cnt_ref, carry_ref):
    i = pl.program_id(0)

    @pl.when(i == 0)
    def _():
        carry_ref[...] = jnp.zeros_like(carry_ref)

    r = r_ref[...]
    n = r.shape[0]
    lane = lax.broadcasted_iota(jnp.int32, r.shape, 1).astype(F32)
    oh1 = lane == r[:, PACK_E1:PACK_E1 + 1]
    oh2 = lane == r[:, PACK_E2:PACK_E2 + 1]
    hit = jnp.where(oh1 | oh2, 1.0, 0.0)
    row = lax.broadcasted_iota(jnp.int32, (n, n), 0)
    col = lax.broadcasted_iota(jnp.int32, (n, n), 1)
    strict = jnp.where(row > col, 1.0, 0.0).astype(BF16)
    before = _dot(strict, hit.astype(BF16)) + carry_ref[0:1, :]
    rank1 = jnp.sum(jnp.where(oh1, before, 0.0), axis=1, keepdims=True)
    rank2 = jnp.sum(jnp.where(oh2, before, 0.0), axis=1, keepdims=True)
    rank_ref[...] = jnp.where(lane == 0.0, rank1, jnp.where(lane == 1.0, rank2, 0.0))
    total = before[n - 1:n, :] + hit[n - 1:n, :]
    carry_ref[...] = jnp.broadcast_to(total, carry_ref.shape)
    cnt_ref[...] = jnp.broadcast_to(total, cnt_ref.shape)


def _plan(routed, tb):
    m = routed.shape[0]
    return pl.pallas_call(
        _plan_kernel,
        grid=(m // tb,),
        in_specs=[pl.BlockSpec((tb, V7X_LANES), lambda i: (i, 0))],
        out_specs=[pl.BlockSpec((tb, V7X_LANES), lambda i: (i, 0)),
                   pl.BlockSpec((V7X_SUBLANES, V7X_LANES), lambda i: (0, 0))],
        out_shape=[jax.ShapeDtypeStruct((m, V7X_LANES), F32),
                   jax.ShapeDtypeStruct((V7X_SUBLANES, V7X_LANES), F32)],
        scratch_shapes=[pltpu.VMEM((V7X_SUBLANES, V7X_LANES), F32)],
        compiler_params=_params(("arbitrary",)),
        name="moe_plan",
    )(routed)


def _slab_rows(ref, row):
    return ref.at[pl.ds(pl.multiple_of(row * V7X_SUBLANES, V7X_SUBLANES), V7X_SUBLANES)]


def _dispatch_kernel(dest_ref, xs_ref, rows_in_ref, rows_ref, sem, *, tb):
    del rows_in_ref
    t0 = pl.program_id(0) * tb

    def copy(i, k):
        return pltpu.make_async_copy(_slab_rows(xs_ref, t0 + i), _slab_rows(rows_ref, dest_ref[0, 0, TOP_K * i + k]), sem)

    def start(i, c):
        for k in range(TOP_K):
            copy(i, k).start()
        return c

    def wait(i, c):
        for k in range(TOP_K):
            copy(i, k).wait()
        return c

    lax.fori_loop(0, tb, start, 0)
    lax.fori_loop(0, tb, wait, 0)


def _dispatch(xs, dest, n_rows, tb):
    m = dest.shape[0]
    rows0 = jnp.zeros((n_rows * V7X_SUBLANES, V7X_LANES), xs.dtype)
    return pl.pallas_call(
        functools.partial(_dispatch_kernel, tb=tb),
        grid=(m // tb,),
        in_specs=[pl.BlockSpec((1, 1, TOP_K * tb), lambda i: (i, 0, 0), memory_space=pltpu.SMEM),
                  pl.BlockSpec(memory_space=pl.ANY),
                  pl.BlockSpec(memory_space=pl.ANY)],
        out_specs=pl.BlockSpec(memory_space=pl.ANY),
        out_shape=jax.ShapeDtypeStruct(rows0.shape, xs.dtype),
        scratch_shapes=[pltpu.SemaphoreType.DMA],
        input_output_aliases={2: 0},
        compiler_params=_params(("arbitrary",)),
        name="moe_dispatch",
    )(dest.reshape(m // tb, 1, TOP_K * tb), xs, rows0)


def _experts_kernel(be_ref, nu_ref, x_ref, wg_ref, wu_ref, wd_ref, o_ref, xb_ref, acc_ref, *, rb):
    i = pl.program_id(0)
    j = pl.program_id(1)
    last = pl.num_programs(1) - 1
    used = i < nu_ref[0]

    @pl.when(used)
    def _():
        @pl.when(j == 0)
        def _():
            xb_ref[...] = _from_slabs(x_ref, rb).astype(BF16)
            acc_ref[...] = jnp.zeros_like(acc_ref)

        xb = xb_ref[...]
        h = _silu_mul(_dot(xb, wg_ref[...]), _dot(xb, wu_ref[...])).astype(BF16)
        acc_ref[...] += _dot(h, wd_ref[...])

        @pl.when(j == last)
        def _():
            _to_slabs(o_ref, acc_ref[...])

    @pl.when(jnp.logical_not(used) & (j == last))
    def _():
        o_ref[...] = jnp.zeros_like(o_ref)


def _experts(rows, w_gu, w_down, block_expert, n_used, rb, fc):
    d = w_down.shape[2]
    n_rows = rows.shape[0] // V7X_SUBLANES
    f = w_down.shape[1]
    nf = f // fc
    nblk = n_rows // rb
    sb = rb * V7X_SUBLANES

    def blk(i, nu):
        return jnp.minimum(i, nu[0] - 1)

    def chunk(i, j, nu):
        return jnp.where(i < nu[0], j, nf - 1)

    grid_spec = pltpu.PrefetchScalarGridSpec(
        num_scalar_prefetch=2,
        grid=(nblk, nf),
        in_specs=[pl.BlockSpec((sb, V7X_LANES), lambda i, j, be, nu: (blk(i, nu), 0)),
                  pl.BlockSpec((None, d, fc), lambda i, j, be, nu: (be[blk(i, nu)], 0, chunk(i, j, nu))),
                  pl.BlockSpec((None, d, fc), lambda i, j, be, nu: (be[blk(i, nu)], 0, nf + chunk(i, j, nu))),
                  pl.BlockSpec((None, fc, d), lambda i, j, be, nu: (be[blk(i, nu)], chunk(i, j, nu), 0))],
        out_specs=pl.BlockSpec((sb, V7X_LANES), lambda i, j, be, nu: (i, 0)),
        scratch_shapes=[pltpu.VMEM((rb, d), BF16), pltpu.VMEM((rb, d), F32)],
    )
    return pl.pallas_call(
        functools.partial(_experts_kernel, rb=rb),
        grid_spec=grid_spec,
        out_shape=jax.ShapeDtypeStruct(rows.shape, F32),
        compiler_params=_params(("arbitrary", "arbitrary")),
        name="moe_experts",
    )(block_expert, n_used, rows, w_gu, w_gu, w_down)


def _combine_ln_kernel(dest_ref, x_ref, r_ref, g_ref, b_ref, y_hbm, o_ref, ybuf, sem, *, tb):
    def copy(i, k):
        return pltpu.make_async_copy(_slab_rows(y_hbm, dest_ref[0, 0, TOP_K * i + k]), _slab_rows(ybuf.at[k], i), sem)

    def start(i, c):
        for k in range(TOP_K):
            copy(i, k).start()
        return c

    def wait(i, c):
        for k in range(TOP_K):
            copy(i, k).wait()
        return c

    lax.fori_loop(0, tb, start, 0)
    lax.fori_loop(0, tb, wait, 0)
    r = r_ref[...]
    y = (r[:, PACK_G1:PACK_G1 + 1] * _from_slabs(ybuf.at[0], tb)
         + r[:, PACK_G2:PACK_G2 + 1] * _from_slabs(ybuf.at[1], tb))
    o_ref[...] = _layer_norm(ALPHA * x_ref[...] + y, g_ref[...], b_ref[...])


def _combine_ln(xt, routed, dest, yrows, g, b, tb):
    m, d = xt.shape
    return pl.pallas_call(
        functools.partial(_combine_ln_kernel, tb=tb),
        grid=(m // tb,),
        in_specs=[pl.BlockSpec((1, 1, TOP_K * tb), lambda i: (i, 0, 0), memory_space=pltpu.SMEM),
                  pl.BlockSpec((tb, d), lambda i: (i, 0)),
                  pl.BlockSpec((tb, V7X_LANES), lambda i: (i, 0)),
                  pl.BlockSpec((1, d), lambda i: (0, 0)),
                  pl.BlockSpec((1, d), lambda i: (0, 0)),
                  pl.BlockSpec(memory_space=pl.ANY)],
        out_specs=pl.BlockSpec((tb, d), lambda i: (i, 0)),
        out_shape=jax.ShapeDtypeStruct((m, d), F32),
        scratch_shapes=[pltpu.VMEM((TOP_K, tb * V7X_SUBLANES, V7X_LANES), F32), pltpu.SemaphoreType.DMA],
        compiler_params=_params(("arbitrary",)),
        name="moe_combine_ln",
    )(dest.reshape(m // tb, 1, TOP_K * tb), xt, routed, g.reshape(1, d), b.reshape(1, d), yrows)


def _moe_layer(xt, w_router, w_gu, w_down, g, b):
    m, d = xt.shape
    rb = MOE_ROW_BLOCK
    routed, xs = _router(xt, w_router, _row_block(m, 1024))
    rank, cnt = _plan(routed, _row_block(m, 512))
    experts = routed[:, PACK_E1:PACK_E2 + 1].astype(jnp.int32)
    counts = cnt[0, :N_EXPERTS].astype(jnp.int32)
    padded = (counts + rb - 1) // rb * rb
    pend = jnp.cumsum(padded)
    pstart = pend - padded
    dest = pstart[experts] + rank[:, :TOP_K].astype(jnp.int32)
    nblk = (m * TOP_K + N_EXPERTS * (rb - 1) + rb - 1) // rb
    block_start = jnp.arange(nblk, dtype=jnp.int32) * rb
    block_expert = jnp.minimum(jnp.searchsorted(pend, block_start, side="right"), N_EXPERTS - 1).astype(jnp.int32)
    n_used = (pend[-1:] // rb).astype(jnp.int32)
    tb = _row_block(m, 256)
    rows = _dispatch(xs, dest, nblk * rb, tb)
    yrows = _experts(rows, w_gu.astype(BF16), w_down.astype(BF16), block_expert, n_used, rb, MOE_FF_CHUNK)
    return _combine_ln(xt, routed, dest, yrows, g, b, tb)


def kernel(x, lru_w_in, lru_conv_w, lru_conv_b, lru_w_gates, lru_b_gates, lru_lambda, lru_w_out, pool_w, pool_scale,
           fox_w_qkvf, fox_b_f, fox_w_o, ffn_w_gu, ffn_w_down, moe_router, moe_w_gu, moe_w_down,
           ln_mix_g, ln_mix_b, ln_ffn_g, ln_ffn_b):
    nbat, s, d = x.shape
    xt = x.reshape(nbat * s, d)
    for i in range(DEPTH):
        mixer, j = i % 3, i // 3
        if mixer == 0:
            xt = _rglru_layer(xt, nbat, s, lru_w_in[j], lru_conv_w[j], lru_conv_b[j], lru_w_gates[j], lru_b_gates[j],
                              lru_lambda[j], lru_w_out[j], ln_mix_g[i], ln_mix_b[i])
        elif mixer == 1:
            xt = _pool_layer(xt, nbat, s, pool_w[j], pool_scale[j], ln_mix_g[i], ln_mix_b[i])
        else:
            xt = _fox_layer(xt, nbat, s, fox_w_qkvf[j], fox_b_f[j], fox_w_o[j], ln_mix_g[i], ln_mix_b[i])
        if i % 2 == 0:
            xt = _ffn_ln(xt, ffn_w_gu[i // 2].astype(BF16), ffn_w_down[i // 2].astype(BF16), ln_ffn_g[i], ln_ffn_b[i],
                         _row_block(nbat * s, 1024), 896)
        else:
            xt = _moe_layer(xt, moe_router[i // 2], moe_w_gu[i // 2], moe_w_down[i // 2], ln_ffn_g[i], ln_ffn_b[i])
    return xt.reshape(nbat, s, d)
```

```python
import functools

import jax
import jax.numpy as jnp
from jax import lax
from jax.experimental import pallas as pl
from jax.experimental.pallas import tpu as pltpu

F32 = jnp.float32
BF16 = jnp.bfloat16

DEPTH = 4
LRU_BLOCKS = 8
CONV_WIDTH = 4
LRU_C = 8.0
POOL_WINDOWS = (2, 4, 8, 16)
FOX_HEADS = 16
N_EXPERTS = 8
TOP_K = 2
LN_EPS = 1e-5
NEG_INF = -1e30
ALPHA = (2 * DEPTH) ** 0.25

V7X_LANES = 128
V7X_SUBLANES = 8
V7X_VMEM_LIMIT_BYTES = 56 * 1024 * 1024


def _params(semantics, vmem=V7X_VMEM_LIMIT_BYTES):
    return pltpu.CompilerParams(dimension_semantics=semantics, vmem_limit_bytes=vmem)


def _layer_norm(y, g, b):
    mu = jnp.mean(y, axis=-1, keepdims=True)
    yc = y - mu
    var = jnp.mean(yc * yc, axis=-1, keepdims=True)
    return yc * lax.rsqrt(var + LN_EPS) * g + b


def _dot(a, b):
    return jnp.dot(a, b, preferred_element_type=F32)


def _matmul_kernel(x_ref, w_ref, o_ref):
    o_ref[...] = _dot(x_ref[...].astype(BF16), w_ref[...]).astype(o_ref.dtype)


def _matmul(x, w, out_dtype, bm, bn):
    m, k = x.shape
    n = w.shape[1]
    return pl.pallas_call(
        _matmul_kernel,
        grid=(m // bm, n // bn),
        in_specs=[pl.BlockSpec((bm, k), lambda i, j: (i, 0)),
                  pl.BlockSpec((k, bn), lambda i, j: (0, j))],
        out_specs=pl.BlockSpec((bm, bn), lambda i, j: (i, j)),
        out_shape=jax.ShapeDtypeStruct((m, n), out_dtype),
        compiler_params=_params(("parallel", "arbitrary")),
        name="matmul",
    )(x, w)


def _mm_ln_kernel(a_ref, w_ref, x_ref, g_ref, b_ref, o_ref):
    y = _dot(a_ref[...].astype(BF16), w_ref[...])
    o_ref[...] = _layer_norm(ALPHA * x_ref[...] + y, g_ref[...], b_ref[...])


def _mm_ln(a, w, x, g, b, bm):
    m, k = a.shape
    d = w.shape[1]
    return pl.pallas_call(
        _mm_ln_kernel,
        grid=(m // bm,),
        in_specs=[pl.BlockSpec((bm, k), lambda i: (i, 0)),
                  pl.BlockSpec((k, d), lambda i: (0, 0)),
                  pl.BlockSpec((bm, d), lambda i: (i, 0)),
                  pl.BlockSpec((1, d), lambda i: (0, 0)),
                  pl.BlockSpec((1, d), lambda i: (0, 0))],
        out_specs=pl.BlockSpec((bm, d), lambda i: (i, 0)),
        out_shape=jax.ShapeDtypeStruct((m, d), F32),
        compiler_params=_params(("parallel",)),
        name="mm_ln",
    )(a, w, x, g.reshape(1, d), b.reshape(1, d))


def _silu_mul(g, u):
    return g * jax.nn.sigmoid(g) * u


def _ffn_ln_kernel(x_ref, wg_ref, wu_ref, wd_ref, g_ref, b_ref, o_ref, xb_ref, acc_ref):
    j = pl.program_id(1)

    @pl.when(j == 0)
    def _():
        xb_ref[...] = x_ref[...].astype(BF16)
        acc_ref[...] = jnp.zeros_like(acc_ref)

    xb = xb_ref[...]
    h = _silu_mul(_dot(xb, wg_ref[...]), _dot(xb, wu_ref[...])).astype(BF16)
    acc_ref[...] += _dot(h, wd_ref[...])

    @pl.when(j == pl.num_programs(1) - 1)
    def _():
        o_ref[...] = _layer_norm(ALPHA * x_ref[...] + acc_ref[...], g_ref[...], b_ref[...])


def _ffn_ln(x, w_gu, w_down, g, b, bm, fc):
    m, d = x.shape
    f = w_down.shape[0]
    nf = f // fc
    return pl.pallas_call(
        _ffn_ln_kernel,
        grid=(m // bm, nf),
        in_specs=[pl.BlockSpec((bm, d), lambda i, j: (i, 0)),
                  pl.BlockSpec((d, fc), lambda i, j: (0, j)),
                  pl.BlockSpec((d, fc), lambda i, j: (0, nf + j)),
                  pl.BlockSpec((fc, d), lambda i, j: (j, 0)),
                  pl.BlockSpec((1, d), lambda i, j: (0, 0)),
                  pl.BlockSpec((1, d), lambda i, j: (0, 0))],
        out_specs=pl.BlockSpec((bm, d), lambda i, j: (i, 0)),
        out_shape=jax.ShapeDtypeStruct((m, d), F32),
        scratch_shapes=[pltpu.VMEM((bm, d), BF16), pltpu.VMEM((bm, d), F32)],
        compiler_params=_params(("parallel", "arbitrary")),
        name="ffn_ln",
    )(x, w_gu, w_gu, w_down, g.reshape(1, d), b.reshape(1, d))


def _softplus(x):
    return jnp.maximum(x, 0.0) + jnp.log1p(jnp.exp(-jnp.abs(x)))


def _gelu_tanh(x):
    return 0.5 * x * (1.0 + jnp.tanh(0.7978845608028654 * (x + 0.044715 * (x * x * x))))


def _rglru_core_kernel(u_ref, cw_ref, cb_ref, wg_ref, bg_ref, lam_ref, o_ref,
                       tm_ref, a_ref, b_ref, hs_ref, h_ref, *, ts):
    nb = LRU_BLOCKS
    halo = (CONV_WIDTH - 1) * V7X_SUBLANES
    rows = ts * V7X_SUBLANES
    step = pl.program_id(0)

    @pl.when(step == 0)
    def _():
        h_ref[...] = jnp.zeros_like(h_ref)
        tm_ref[:, 0:halo, :] = jnp.zeros((2 * nb, halo, V7X_LANES), F32)

    @pl.when(step > 0)
    def _():
        tm_ref[:, 0:halo, :] = tm_ref[:, rows:rows + halo, :]

    for bi in range(V7X_SUBLANES):
        for c in range(2 * nb):
            tm_ref[c, pl.ds(halo + bi, ts, stride=V7X_SUBLANES), :] = u_ref[bi, :, c * V7X_LANES:(c + 1) * V7X_LANES]

    for n in range(nb):
        sl = slice(n * V7X_LANES, (n + 1) * V7X_LANES)
        xr = cb_ref[:, sl]
        for k in range(CONV_WIDTH):
            xr = xr + tm_ref[nb + n, k * V7X_SUBLANES:k * V7X_SUBLANES + rows, :] * cw_ref[k:k + 1, sl]
        z = _dot(xr.astype(BF16), wg_ref[n]) + bg_ref[n]
        r = jax.nn.sigmoid(z[:, :V7X_LANES])
        i = jax.nn.sigmoid(z[:, V7X_LANES:])
        log_a = (-LRU_C * r) * _softplus(-lam_ref[:, sl])
        a = jnp.exp(log_a)
        a_ref[n] = a
        b_ref[n] = jnp.sqrt(-jnp.tanh(log_a) * (a * a + 1.0)) * (i * xr)

    def scan_step(t, hs):
        r0 = pl.multiple_of(t * V7X_SUBLANES, V7X_SUBLANES)
        new = []
        for n in range(nb):
            h = a_ref[n, pl.ds(r0, V7X_SUBLANES), :] * hs[n] + b_ref[n, pl.ds(r0, V7X_SUBLANES), :]
            hs_ref[n, pl.ds(r0, V7X_SUBLANES), :] = h
            new.append(h)
        return tuple(new)

    hs = lax.fori_loop(0, ts, scan_step, tuple(h_ref[n] for n in range(nb)), unroll=8)
    for n in range(nb):
        h_ref[n] = hs[n]

    for n in range(nb):
        a_ref[n] = _gelu_tanh(tm_ref[n, halo:halo + rows, :]) * hs_ref[n]
        for bi in range(V7X_SUBLANES):
            o_ref[bi, :, n * V7X_LANES:(n + 1) * V7X_LANES] = (
                a_ref[n, pl.ds(bi, ts, stride=V7X_SUBLANES), :].astype(o_ref.dtype))


def _rglru_core(u, conv_w, conv_b, w_gates, b_gates, lam, ts):
    nbat, s, d2 = u.shape
    d = d2 // 2
    assert nbat == V7X_SUBLANES and d == LRU_BLOCKS * V7X_LANES and s % ts == 0
    rows = ts * V7X_SUBLANES
    halo = (CONV_WIDTH - 1) * V7X_SUBLANES
    return pl.pallas_call(
        functools.partial(_rglru_core_kernel, ts=ts),
        grid=(s // ts,),
        in_specs=[pl.BlockSpec((nbat, ts, d2), lambda t: (0, t, 0)),
                  pl.BlockSpec((CONV_WIDTH, d), lambda t: (0, 0)),
                  pl.BlockSpec((1, d), lambda t: (0, 0)),
                  pl.BlockSpec((LRU_BLOCKS, V7X_LANES, 2 * V7X_LANES), lambda t: (0, 0, 0)),
                  pl.BlockSpec((LRU_BLOCKS, 1, 2 * V7X_LANES), lambda t: (0, 0, 0)),
                  pl.BlockSpec((1, d), lambda t: (0, 0))],
        out_specs=pl.BlockSpec((nbat, ts, d), lambda t: (0, t, 0)),
        out_shape=jax.ShapeDtypeStruct((nbat, s, d), BF16),
        scratch_shapes=[pltpu.VMEM((2 * LRU_BLOCKS, halo + rows, V7X_LANES), F32),
                        pltpu.VMEM((LRU_BLOCKS, rows, V7X_LANES), F32),
                        pltpu.VMEM((LRU_BLOCKS, rows, V7X_LANES), F32),
                        pltpu.VMEM((LRU_BLOCKS, rows, V7X_LANES), F32),
                        pltpu.VMEM((LRU_BLOCKS, V7X_SUBLANES, V7X_LANES), F32)],
        compiler_params=_params(("arbitrary",)),
        name="rglru_core",
    )(u, conv_w, conv_b.reshape(1, d), w_gates.astype(BF16), b_gates.reshape(LRU_BLOCKS, 1, 2 * V7X_LANES),
      lam.reshape(1, d))


def _row_block(m, want):
    return want if m % want == 0 else m


def _rglru_layer(xt, nbat, s, w_in, conv_w, conv_b, w_gates, b_gates, lam, w_out, g, b):
    m, d = xt.shape
    bm = _row_block(m, 1024)
    u = _matmul(xt, w_in.astype(BF16), F32, bm, d)
    gh = _rglru_core(u.reshape(nbat, s, 2 * d), conv_w, conv_b, w_gates, b_gates, lam, ts=_row_block(s, 128))
    return _mm_ln(gh.reshape(m, d), w_out.astype(BF16), xt, g, b, bm)


POOL_HALO = 32


def _pool_ln_kernel(x_ref, prev_ref, w_ref, sc_ref, g_ref, b_ref, o_ref, e_ref, st_ref, y_ref, *, rc, tiles_per_seq):
    i = pl.program_id(0)
    first = (i % tiles_per_seq) == 0
    d = x_ref.shape[1]
    gw = d // len(POOL_WINDOWS)
    e_ref[0:POOL_HALO, :] = jnp.where(first, 0.0, prev_ref[...])
    e_ref[POOL_HALO:, :] = x_ref[...]
    t = ((i % tiles_per_seq) * rc + lax.broadcasted_iota(jnp.int32, (rc, 1), 0) + 1).astype(F32)
    for gi, wl in enumerate(POOL_WINDOWS):
        sl = slice(gi * gw, (gi + 1) * gw)
        lo = V7X_SUBLANES
        shift = 1
        cur = e_ref[lo:, sl] + e_ref[lo - shift:POOL_HALO + rc - shift, sl]
        while 2 * shift < wl:
            shift *= 2
            st_ref[lo:, :] = cur
            nlo = lo + V7X_SUBLANES
            cur = st_ref[nlo:, :] + st_ref[nlo - shift:POOL_HALO + rc - shift, :]
            lo = nlo
        win = cur[POOL_HALO - lo:, :]
        xg = x_ref[:, sl]
        p = win / jnp.minimum(t, float(wl)) - xg
        y_ref[:, sl] = _dot(p.astype(BF16), w_ref[gi])
    y = y_ref[...] * sc_ref[...]
    o_ref[...] = _layer_norm(ALPHA * x_ref[...] + y, g_ref[...], b_ref[...])


def _pool_layer(xt, nbat, s, pool_w, pool_scale, g, b):
    m, d = xt.shape
    rc = _row_block(s, 512)
    tiles_per_seq = s // rc
    ng = len(POOL_WINDOWS)
    gw = d // ng
    hb = rc // POOL_HALO
    return pl.pallas_call(
        functools.partial(_pool_ln_kernel, rc=rc, tiles_per_seq=tiles_per_seq),
        grid=(m // rc,),
        in_specs=[pl.BlockSpec((rc, d), lambda i: (i, 0)),
                  pl.BlockSpec((POOL_HALO, d), lambda i: (jnp.maximum(i * hb - 1, 0), 0)),
                  pl.BlockSpec((ng, gw, gw), lambda i: (0, 0, 0)),
                  pl.BlockSpec((1, d), lambda i: (0, 0)),
                  pl.BlockSpec((1, d), lambda i: (0, 0)),
                  pl.BlockSpec((1, d), lambda i: (0, 0))],
        out_specs=pl.BlockSpec((rc, d), lambda i: (i, 0)),
        out_shape=jax.ShapeDtypeStruct((m, d), F32),
        scratch_shapes=[pltpu.VMEM((POOL_HALO + rc, d), F32),
                        pltpu.VMEM((POOL_HALO + rc, gw), F32),
                        pltpu.VMEM((rc, d), F32)],
        compiler_params=_params(("parallel",)),
        name="pool_ln",
    )(xt, xt, pool_w.astype(BF16), pool_scale.reshape(1, d), g.reshape(1, d), b.reshape(1, d))


def _log_sigmoid(z):
    return -_softplus(-z)


def _logf_kernel(x_ref, w_ref, b_ref, o_ref):
    o_ref[...] = _log_sigmoid(_dot(x_ref[...].astype(BF16), w_ref[...]) + b_ref[...])


def _logf(xt, w_f, b_f, bm):
    m, d = xt.shape
    h = w_f.shape[1]
    w_pad = jnp.pad(w_f, ((0, 0), (0, V7X_LANES - h))).astype(BF16)
    b_pad = jnp.pad(b_f, (0, V7X_LANES - h)).reshape(1, V7X_LANES)
    return pl.pallas_call(
        _logf_kernel,
        grid=(m // bm,),
        in_specs=[pl.BlockSpec((bm, d), lambda i: (i, 0)),
                  pl.BlockSpec((d, V7X_LANES), lambda i: (0, 0)),
                  pl.BlockSpec((1, V7X_LANES), lambda i: (0, 0))],
        out_specs=pl.BlockSpec((bm, V7X_LANES), lambda i: (i, 0)),
        out_shape=jax.ShapeDtypeStruct((m, V7X_LANES), F32),
        compiler_params=_params(("parallel",)),
        name="fox_logf",
    )(xt, w_pad, b_pad)


CUMSUM_ROWS = 256


def _split3(v):
    hi = v.astype(BF16)
    r1 = v - hi.astype(F32)
    mid = r1.astype(BF16)
    lo = (r1 - mid.astype(F32)).astype(BF16)
    return hi, mid, lo


def _cumsum_kernel(lf_ref, f_ref, ft_ref, carry_ref):
    j = pl.program_id(1)

    @pl.when(j == 0)
    def _():
        carry_ref[...] = jnp.zeros_like(carry_ref)

    n = lf_ref.shape[0]
    row = lax.broadcasted_iota(jnp.int32, (n, n), 0)
    col = lax.broadcasted_iota(jnp.int32, (n, n), 1)
    tri = jnp.where(row >= col, 1.0, 0.0).astype(BF16)
    hi, mid, lo = _split3(lf_ref[...])
    cs = (_dot(tri, hi) + _dot(tri, mid) + _dot(tri, lo)) + carry_ref[0:1, :]
    f_ref[...] = cs
    ft_ref[...] = cs.T
    carry_ref[...] = jnp.broadcast_to(cs[n - 1:n, :], carry_ref.shape)


def _forget_cumsum(logf, nbat, s):
    n = _row_block(s, CUMSUM_ROWS)
    nj = s // n
    return pl.pallas_call(
        _cumsum_kernel,
        grid=(nbat, nj),
        in_specs=[pl.BlockSpec((n, V7X_LANES), lambda bi, j: (bi * nj + j, 0))],
        out_specs=[pl.BlockSpec((n, V7X_LANES), lambda bi, j: (bi * nj + j, 0)),
                   pl.BlockSpec((None, V7X_LANES, n), lambda bi, j: (bi, 0, j))],
        out_shape=[jax.ShapeDtypeStruct((nbat * s, V7X_LANES), F32),
                   jax.ShapeDtypeStruct((nbat, V7X_LANES, s), F32)],
        scratch_shapes=[pltpu.VMEM((V7X_SUBLANES, V7X_LANES), F32)],
        compiler_params=_params(("parallel", "arbitrary")),
        name="fox_cumsum",
    )(logf)


def _fox_attn_kernel(q_ref, k_ref, v_ref, fq_ref, fk_ref, o_ref, m_ref, l_ref, acc_ref, *, tq, dh):
    p = pl.program_id(1)
    qi = pl.program_id(2)
    lane = lax.broadcasted_iota(jnp.int32, (tq, V7X_LANES), 1)
    q = q_ref[...] * jnp.asarray(dh ** -0.5, q_ref.dtype)
    fq_all = fq_ref[...]
    row = lax.broadcasted_iota(jnp.int32, (tq, tq), 0)
    col = lax.broadcasted_iota(jnp.int32, (tq, tq), 1)
    reps = tq // V7X_LANES
    qh, fq = [], []
    for hh in range(2):
        in_head = (lane >= hh * dh) & (lane < (hh + 1) * dh)
        qh.append(jnp.where(in_head, q, jnp.zeros_like(q)))
        fq_col = jnp.sum(jnp.where(lane == 2 * p + hh, fq_all, 0.0), axis=1, keepdims=True)
        fq.append(jnp.broadcast_to(fq_col, (tq, V7X_LANES)))
    m_ref[...] = jnp.full(m_ref.shape, NEG_INF, F32)
    l_ref[...] = jnp.zeros_like(l_ref)
    acc_ref[...] = jnp.zeros_like(acc_ref)

    def kv_step(j, diagonal):
        r0 = pl.multiple_of(j * tq, tq)
        k = k_ref[pl.ds(r0, tq), :]
        v = v_ref[pl.ds(r0, tq), :]
        for hh in range(2):
            s = lax.dot_general(qh[hh], k, (((1,), (1,)), ((), ())), preferred_element_type=F32)
            t = s - fk_ref[hh:hh + 1, pl.ds(r0, tq)]
            if diagonal:
                t = jnp.where(col <= row, t, NEG_INF)
            m_prev = m_ref[hh]
            m_new = jnp.maximum(m_prev, fq[hh] + jnp.max(t, axis=1, keepdims=True))
            alpha = jnp.exp(m_prev - m_new)
            c = fq[hh] - m_new
            pr = jnp.exp(t + jnp.concatenate([c] * reps, axis=1))
            l_ref[hh] = alpha * l_ref[hh] + jnp.sum(pr, axis=1, keepdims=True)
            acc_ref[hh] = alpha * acc_ref[hh] + _dot(pr.astype(BF16), v)
            m_ref[hh] = m_new

    def body(j, carry):
        kv_step(j, False)
        return carry

    lax.fori_loop(0, qi, body, 0)
    kv_step(qi, True)
    o_ref[...] = jnp.where(lane < dh, acc_ref[0] / l_ref[0], acc_ref[1] / l_ref[1]).astype(o_ref.dtype)


def _fox_attention(qkv, f_rows, f_cols, nbat, s, d):
    dh = d // FOX_HEADS
    npair = FOX_HEADS // 2
    assert 2 * dh == V7X_LANES
    tq = _row_block(s, 512)
    nq = s // tq
    return pl.pallas_call(
        functools.partial(_fox_attn_kernel, tq=tq, dh=dh),
        grid=(nbat, npair, nq),
        in_specs=[pl.BlockSpec((tq, V7X_LANES), lambda bi, p, qi: (bi * nq + qi, p)),
                  pl.BlockSpec((s, V7X_LANES), lambda bi, p, qi: (bi, npair + p)),
                  pl.BlockSpec((s, V7X_LANES), lambda bi, p, qi: (bi, 2 * npair + p)),
                  pl.BlockSpec((tq, V7X_LANES), lambda bi, p, qi: (bi * nq + qi, 0)),
                  pl.BlockSpec((None, None, V7X_SUBLANES, s), lambda bi, p, qi: (bi, p, 0, 0))],
        out_specs=pl.BlockSpec((tq, V7X_LANES), lambda bi, p, qi: (bi * nq + qi, p)),
        out_shape=jax.ShapeDtypeStruct((nbat * s, d), BF16),
        scratch_shapes=[pltpu.VMEM((2, tq, V7X_LANES), F32)] * 3,
        compiler_params=_params(("parallel", "parallel", "arbitrary")),
        name="fox_attn",
    )(qkv, qkv, qkv, f_rows, f_cols)


def _fox_layer(xt, nbat, s, w_qkvf, b_f, w_o, g, b):
    m, d = xt.shape
    bm = _row_block(m, 1024)
    qkv = _matmul(xt, w_qkvf[:, :3 * d].astype(BF16), BF16, bm, d)
    logf = _logf(xt, w_qkvf[:, 3 * d:], b_f, bm)
    f_rows, f_t = _forget_cumsum(logf, nbat, s)
    npair = FOX_HEADS // 2
    f_cols = jnp.pad(f_t[:, :FOX_HEADS, :].reshape(nbat, npair, 2, s), ((0, 0), (0, 0), (0, V7X_SUBLANES - 2), (0, 0)))
    o = _fox_attention(qkv, f_rows, f_cols, nbat, s, d)
    return _mm_ln(o, w_o.astype(BF16), xt, g, b, bm)


MOE_ROW_BLOCK = 512
MOE_FF_CHUNK = 1792
PACK_E1, PACK_E2, PACK_G1, PACK_G2 = 0, 1, 2, 3


def _to_slabs(slab_ref, value):
    rows = value.shape[0]
    for c in range(V7X_SUBLANES):
        slab_ref[pl.ds(c, rows, stride=V7X_SUBLANES), :] = value[:, c * V7X_LANES:(c + 1) * V7X_LANES]


def _from_slabs(slab_ref, rows):
    return jnp.concatenate([slab_ref[pl.ds(c, rows, stride=V7X_SUBLANES), :] for c in range(V7X_SUBLANES)], axis=1)


def _router_kernel(x_ref, w_ref, r_ref, xs_ref):
    x = x_ref[...]
    _to_slabs(xs_ref, x)
    xh = x.astype(BF16)
    xl = (x - xh.astype(F32)).astype(BF16)
    w = w_ref[...]
    wh = w.astype(BF16)
    wl = (w - wh.astype(F32)).astype(BF16)
    logits = (_dot(xh, wh) + _dot(xl, wh)) + _dot(xh, wl)
    lane = lax.broadcasted_iota(jnp.int32, logits.shape, 1)
    logits = jnp.where(lane < N_EXPERTS, logits, NEG_INF)
    m1 = jnp.max(logits, axis=1, keepdims=True)
    i1 = jnp.min(jnp.where(logits == m1, lane, V7X_LANES), axis=1, keepdims=True)
    rest = jnp.where(lane == i1, NEG_INF, logits)
    m2 = jnp.max(rest, axis=1, keepdims=True)
    i2 = jnp.min(jnp.where(rest == m2, lane, V7X_LANES), axis=1, keepdims=True)
    e21 = jnp.exp(m2 - m1)
    g1 = 1.0 / (1.0 + e21)
    g2 = e21 * g1
    out = jnp.where(lane == PACK_E1, i1.astype(F32), 0.0)
    out = jnp.where(lane == PACK_E2, i2.astype(F32), out)
    out = jnp.where(lane == PACK_G1, g1, out)
    out = jnp.where(lane == PACK_G2, g2, out)
    r_ref[...] = out


def _router(xt, w_router, bm):
    m, d = xt.shape
    w_pad = jnp.pad(w_router, ((0, 0), (0, V7X_LANES - w_router.shape[1])))
    return pl.pallas_call(
        _router_kernel,
        grid=(m // bm,),
        in_specs=[pl.BlockSpec((bm, d), lambda i: (i, 0)),
                  pl.BlockSpec((d, V7X_LANES), lambda i: (0, 0))],
        out_specs=[pl.BlockSpec((bm, V7X_LANES), lambda i: (i, 0)),
                   pl.BlockSpec((bm * V7X_SUBLANES, V7X_LANES), lambda i: (i, 0))],
        out_shape=[jax.ShapeDtypeStruct((m, V7X_LANES), F32),
                   jax.ShapeDtypeStruct((m * V7X_SUBLANES, V7X_LANES), F32)],
        compiler_params=_params(("parallel",)),
        name="moe_router",
    )(xt, w_pad)


def _plan_kernel(r_ref, rank_ref, cnt_ref, carry_ref):
    i = pl.program_id(0)

    @pl.when(i == 0)
    def _():
        carry_ref[...] = jnp.zeros_like(carry_ref)

    r = r_ref[...]
    n = r.shape[0]
    lane = lax.broadcasted_iota(jnp.int32, r.shape, 1).astype(F32)
    oh1 = lane == r[:, PACK_E1:PACK_E1 + 1]
    oh2 = lane == r[:, PACK_E2:PACK_E2 + 1]
    hit = jnp.where(oh1 | oh2, 1.0, 0.0)
    row = lax.broadcasted_iota(jnp.int32, (n, n), 0)
    col = lax.broadcasted_iota(jnp.int32, (n, n), 1)
    strict = jnp.where(row > col, 1.0, 0.0).astype(BF16)
    before = _dot(strict, hit.astype(BF16)) + carry_ref[0:1, :]
    rank1 = jnp.sum(jnp.where(oh1, before, 0.0), axis=1, keepdims=True)
    rank2 = jnp.sum(jnp.where(oh2, before, 0.0), axis=1, keepdims=True)
    rank_ref[...] = jnp.where(lane == 0.0, rank1, jnp.where(lane == 1.0, rank2, 0.0))
    total = before[n - 1:n, :] + hit[n - 1:n, :]
    carry_ref[...] = jnp.broadcast_to(total, carry_ref.shape)
    cnt_ref[...] = jnp.broadcast_to(total, cnt_ref.shape)


def _plan(routed, tb):
    m = routed.shape[0]
    return pl.pallas_call(
        _plan_kernel,
        grid=(m // tb,),
        in_specs=[pl.BlockSpec((tb, V7X_LANES), lambda i: (i, 0))],
        out_specs=[pl.BlockSpec((tb, V7X_LANES), lambda i: (i, 0)),
                   pl.BlockSpec((V7X_SUBLANES, V7X_LANES), lambda i: (0, 0))],
        out_shape=[jax.ShapeDtypeStruct((m, V7X_LANES), F32),
                   jax.ShapeDtypeStruct((V7X_SUBLANES, V7X_LANES), F32)],
        scratch_shapes=[pltpu.VMEM((V7X_SUBLANES, V7X_LANES), F32)],
        compiler_params=_params(("arbitrary",)),
        name="moe_plan",
    )(routed)


def _slab_rows(ref, row):
    return ref.at[pl.ds(pl.multiple_of(row * V7X_SUBLANES, V7X_SUBLANES), V7X_SUBLANES)]


def _experts_kernel(be_ref, nu_ref, src_ref, xs_hbm, wg_ref, wu_ref, wd_ref, o_ref, xin_ref, xb_ref, acc_ref, sem, *, rb):
    i = pl.program_id(0)
    j = pl.program_id(1)
    last = pl.num_programs(1) - 1
    used = i < nu_ref[0]

    @pl.when(used)
    def _():
        @pl.when(j == 0)
        def _():
            def copy(r):
                return pltpu.make_async_copy(_slab_rows(xs_hbm, src_ref[0, 0, r]), _slab_rows(xin_ref, r), sem)

            def start(r, c):
                copy(r).start()
                return c

            def wait(r, c):
                copy(r).wait()
                return c

            lax.fori_loop(0, rb, start, 0)
            lax.fori_loop(0, rb, wait, 0)
            xb_ref[...] = _from_slabs(xin_ref, rb).astype(BF16)
            acc_ref[...] = jnp.zeros_like(acc_ref)

        xb = xb_ref[...]
        h = _silu_mul(_dot(xb, wg_ref[...]), _dot(xb, wu_ref[...])).astype(BF16)
        acc_ref[...] += _dot(h, wd_ref[...])

        @pl.when(j == last)
        def _():
            _to_slabs(o_ref, acc_ref[...])

    @pl.when(jnp.logical_not(used) & (j == last))
    def _():
        o_ref[...] = jnp.zeros_like(o_ref)


def _experts(xs, src, w_gu, w_down, block_expert, n_used, rb, fc):
    d = w_down.shape[2]
    n_rows = src.shape[0]
    f = w_down.shape[1]
    nf = f // fc
    nblk = n_rows // rb
    sb = rb * V7X_SUBLANES

    def blk(i, nu):
        return jnp.minimum(i, nu[0] - 1)

    def chunk(i, j, nu):
        return jnp.where(i < nu[0], j, nf - 1)

    grid_spec = pltpu.PrefetchScalarGridSpec(
        num_scalar_prefetch=2,
        grid=(nblk, nf),
        in_specs=[pl.BlockSpec((1, 1, rb), lambda i, j, be, nu: (blk(i, nu), 0, 0), memory_space=pltpu.SMEM),
                  pl.BlockSpec(memory_space=pl.ANY),
                  pl.BlockSpec((None, d, fc), lambda i, j, be, nu: (be[blk(i, nu)], 0, chunk(i, j, nu))),
                  pl.BlockSpec((None, d, fc), lambda i, j, be, nu: (be[blk(i, nu)], 0, nf + chunk(i, j, nu))),
                  pl.BlockSpec((None, fc, d), lambda i, j, be, nu: (be[blk(i, nu)], chunk(i, j, nu), 0))],
        out_specs=pl.BlockSpec((sb, V7X_LANES), lambda i, j, be, nu: (i, 0)),
        scratch_shapes=[pltpu.VMEM((sb, V7X_LANES), F32), pltpu.VMEM((rb, d), BF16), pltpu.VMEM((rb, d), F32),
                        pltpu.SemaphoreType.DMA],
    )
    return pl.pallas_call(
        functools.partial(_experts_kernel, rb=rb),
        grid_spec=grid_spec,
        out_shape=jax.ShapeDtypeStruct((n_rows * V7X_SUBLANES, V7X_LANES), F32),
        compiler_params=_params(("arbitrary", "arbitrary")),
        name="moe_experts",
    )(block_expert, n_used, src.reshape(nblk, 1, rb), xs, w_gu, w_gu, w_down)


def _combine_ln_kernel(dest_ref, x_ref, r_ref, g_ref, b_ref, y_hbm, o_ref, ybuf, sem, *, tb):
    def copy(i, k):
        return pltpu.make_async_copy(_slab_rows(y_hbm, dest_ref[0, 0, TOP_K * i + k]), _slab_rows(ybuf.at[k], i), sem)

    def start(i, c):
        for k in range(TOP_K):
            copy(i, k).start()
        return c

    def wait(i, c):
        for k in range(TOP_K):
            copy(i, k).wait()
        return c

    lax.fori_loop(0, tb, start, 0)
    lax.fori_loop(0, tb, wait, 0)
    r = r_ref[...]
    y = (r[:, PACK_G1:PACK_G1 + 1] * _from_slabs(ybuf.at[0], tb)
         + r[:, PACK_G2:PACK_G2 + 1] * _from_slabs(ybuf.at[1], tb))
    o_ref[...] = _layer_norm(ALPHA * x_ref[...] + y, g_ref[...], b_ref[...])


def _combine_ln(xt, routed, dest, yrows, g, b, tb):
    m, d = xt.shape
    return pl.pallas_call(
        functools.partial(_combine_ln_kernel, tb=tb),
        grid=(m // tb,),
        in_specs=[pl.BlockSpec((1, 1, TOP_K * tb), lambda i: (i, 0, 0), memory_space=pltpu.SMEM),
                  pl.BlockSpec((tb, d), lambda i: (i, 0)),
                  pl.BlockSpec((tb, V7X_LANES), lambda i: (i, 0)),
                  pl.BlockSpec((1, d), lambda i: (0, 0)),
                  pl.BlockSpec((1, d), lambda i: (0, 0)),
                  pl.BlockSpec(memory_space=pl.ANY)],
        out_specs=pl.BlockSpec((tb, d), lambda i: (i, 0)),
        out_shape=jax.ShapeDtypeStruct((m, d), F32),
        scratch_shapes=[pltpu.VMEM((TOP_K, tb * V7X_SUBLANES, V7X_LANES), F32), pltpu.SemaphoreType.DMA],
        compiler_params=_params(("arbitrary",)),
        name="moe_combine_ln",
    )(dest.reshape(m // tb, 1, TOP_K * tb), xt, routed, g.reshape(1, d), b.reshape(1, d), yrows)


def _moe_layer(xt, w_router, w_gu, w_down, g, b):
    m, d = xt.shape
    rb = MOE_ROW_BLOCK
    routed, xs = _router(xt, w_router, _row_block(m, 1024))
    rank, cnt = _plan(routed, _row_block(m, 512))
    experts = routed[:, PACK_E1:PACK_E2 + 1].astype(jnp.int32)
    counts = cnt[0, :N_EXPERTS].astype(jnp.int32)
    padded = (counts + rb - 1) // rb * rb
    pend = jnp.cumsum(padded)
    pstart = pend - padded
    dest = pstart[experts] + rank[:, :TOP_K].astype(jnp.int32)
    nblk = (m * TOP_K + N_EXPERTS * (rb - 1) + rb - 1) // rb
    block_start = jnp.arange(nblk, dtype=jnp.int32) * rb
    block_expert = jnp.minimum(jnp.searchsorted(pend, block_start, side="right"), N_EXPERTS - 1).astype(jnp.int32)
    n_used = (pend[-1:] // rb).astype(jnp.int32)
    token = jnp.arange(m * TOP_K, dtype=jnp.int32) // TOP_K
    src = jnp.zeros((nblk * rb,), jnp.int32).at[dest.reshape(-1)].set(token, unique_indices=True)
    yrows = _experts(xs, src, w_gu.astype(BF16), w_down.astype(BF16), block_expert, n_used, rb, MOE_FF_CHUNK)
    return _combine_ln(xt, routed, dest, yrows, g, b, _row_block(m, 256))


def kernel(x, lru_w_in, lru_conv_w, lru_conv_b, lru_w_gates, lru_b_gates, lru_lambda, lru_w_out, pool_w, pool_scale,
           fox_w_qkvf, fox_b_f, fox_w_o, ffn_w_gu, ffn_w_down, moe_router, moe_w_gu, moe_w_down,
           ln_mix_g, ln_mix_b, ln_ffn_g, ln_ffn_b):
    nbat, s, d = x.shape
    xt = x.reshape(nbat * s, d)
    for i in range(DEPTH):
        mixer, j = i % 3, i // 3
        if mixer == 0:
            xt = _rglru_layer(xt, nbat, s, lru_w_in[j], lru_conv_w[j], lru_conv_b[j], lru_w_gates[j], lru_b_gates[j],
                              lru_lambda[j], lru_w_out[j], ln_mix_g[i], ln_mix_b[i])
        elif mixer == 1:
            xt = _pool_layer(xt, nbat, s, pool_w[j], pool_scale[j], ln_mix_g[i], ln_mix_b[i])
        else:
            xt = _fox_layer(xt, nbat, s, fox_w_qkvf[j], fox_b_f[j], fox_w_o[j], ln_mix_g[i], ln_mix_b[i])
        if i % 2 == 0:
            xt = _ffn_ln(xt, ffn_w_gu[i // 2].astype(BF16), ffn_w_down[i // 2].astype(BF16), ln_ffn_g[i], ln_ffn_b[i],
                         _row_block(nbat * s, 1024), 896)
        else:
            xt = _moe_layer(xt, moe_router[i // 2], moe_w_gu[i // 2], moe_w_down[i // 2], ln_ffn_g[i], ln_ffn_b[i])
    return xt.reshape(nbat, s, d)
```

```python
import functools

import jax
import jax.numpy as jnp
from jax import lax
from jax.experimental import pallas as pl
from jax.experimental.pallas import tpu as pltpu

F32 = jnp.float32
BF16 = jnp.bfloat16

DEPTH = 4
LRU_BLOCKS = 8
CONV_WIDTH = 4
LRU_C = 8.0
POOL_WINDOWS = (2, 4, 8, 16)
FOX_HEADS = 16
N_EXPERTS = 8
TOP_K = 2
LN_EPS = 1e-5
NEG_INF = -1e30
ALPHA = (2 * DEPTH) ** 0.25

V7X_LANES = 128
V7X_SUBLANES = 8
V7X_VMEM_LIMIT_BYTES = 58 * 1024 * 1024

ROW_TILE = 1024
FFN_CHUNK = 512
RGLRU_TIME_TILE = 128
POOL_ROWS = 512
POOL_HALO = 32
ATTN_TILE = 512
CUMSUM_ROWS = 256
ROUTE_ROWS = 512
MOE_SUPER_ROWS = 1024
MOE_ROW_BLOCK = 512
MOE_FF_CHUNK = 896
COMBINE_ROWS = 256
WAIT_UNROLL = 32


def _params(semantics, vmem=V7X_VMEM_LIMIT_BYTES):
    return pltpu.CompilerParams(dimension_semantics=semantics, vmem_limit_bytes=vmem)


def _row_block(m, want):
    return want if m % want == 0 else m


def _layer_norm(y, g, b):
    mu = jnp.mean(y, axis=-1, keepdims=True)
    yc = y - mu
    var = jnp.mean(yc * yc, axis=-1, keepdims=True)
    return yc * lax.rsqrt(var + LN_EPS) * g + b


def _dot(a, b):
    return jnp.dot(a, b, preferred_element_type=F32)


def _softplus(x):
    return jnp.maximum(x, 0.0) + jnp.log1p(jnp.exp(-jnp.abs(x)))


def _log_sigmoid(z):
    return -_softplus(-z)


def _gelu_tanh(x):
    return 0.5 * x * (1.0 + jnp.tanh(0.7978845608028654 * (x + 0.044715 * (x * x * x))))


def _silu_mul(g, u):
    return g * jax.nn.sigmoid(g) * u


def _matmul_kernel(x_ref, w_ref, o_ref):
    o_ref[...] = _dot(x_ref[...].astype(BF16), w_ref[...].astype(BF16)).astype(o_ref.dtype)


def _matmul(x, w_stack, layer, n_out, out_dtype, bm, bn):
    m, k = x.shape
    return pl.pallas_call(
        _matmul_kernel,
        grid=(m // bm, n_out // bn),
        in_specs=[pl.BlockSpec((bm, k), lambda i, j: (i, 0)),
                  pl.BlockSpec((None, k, bn), lambda i, j: (layer, 0, j))],
        out_specs=pl.BlockSpec((bm, bn), lambda i, j: (i, j)),
        out_shape=jax.ShapeDtypeStruct((m, n_out), out_dtype),
        compiler_params=_params(("parallel", "arbitrary")),
        name="matmul",
    )(x, w_stack)


def _mm_ln_kernel(a_ref, w_ref, x_ref, g_ref, b_ref, o_ref):
    y = _dot(a_ref[...].astype(BF16), w_ref[...].astype(BF16))
    o_ref[...] = _layer_norm(ALPHA * x_ref[...] + y, g_ref[...], b_ref[...])


def _mm_ln(a, w_stack, layer, x, g, b, bm):
    m, k = a.shape
    d = w_stack.shape[2]
    return pl.pallas_call(
        _mm_ln_kernel,
        grid=(m // bm,),
        in_specs=[pl.BlockSpec((bm, k), lambda i: (i, 0)),
                  pl.BlockSpec((None, k, d), lambda i: (layer, 0, 0)),
                  pl.BlockSpec((bm, d), lambda i: (i, 0)),
                  pl.BlockSpec((1, d), lambda i: (0, 0)),
                  pl.BlockSpec((1, d), lambda i: (0, 0))],
        out_specs=pl.BlockSpec((bm, d), lambda i: (i, 0)),
        out_shape=jax.ShapeDtypeStruct((m, d), F32),
        compiler_params=_params(("parallel",)),
        name="mm_ln",
    )(a, w_stack, x, g.reshape(1, d), b.reshape(1, d))


def _ffn_ln_kernel(x_ref, wg_ref, wu_ref, wd_ref, g_ref, b_ref, o_ref, xb_ref, acc_ref):
    j = pl.program_id(1)

    @pl.when(j == 0)
    def _():
        xb_ref[...] = x_ref[...].astype(BF16)
        acc_ref[...] = jnp.zeros_like(acc_ref)

    xb = xb_ref[...]
    h = _silu_mul(_dot(xb, wg_ref[...].astype(BF16)), _dot(xb, wu_ref[...].astype(BF16))).astype(BF16)
    acc_ref[...] += _dot(h, wd_ref[...].astype(BF16))

    @pl.when(j == pl.num_programs(1) - 1)
    def _():
        o_ref[...] = _layer_norm(ALPHA * x_ref[...] + acc_ref[...], g_ref[...], b_ref[...])


def _ffn_ln(x, w_gu_stack, w_down_stack, layer, g, b, bm, fc):
    m, d = x.shape
    f = w_down_stack.shape[1]
    nf = f // fc
    return pl.pallas_call(
        _ffn_ln_kernel,
        grid=(m // bm, nf),
        in_specs=[pl.BlockSpec((bm, d), lambda i, j: (i, 0)),
                  pl.BlockSpec((None, d, fc), lambda i, j: (layer, 0, j)),
                  pl.BlockSpec((None, d, fc), lambda i, j: (layer, 0, nf + j)),
                  pl.BlockSpec((None, fc, d), lambda i, j: (layer, j, 0)),
                  pl.BlockSpec((1, d), lambda i, j: (0, 0)),
                  pl.BlockSpec((1, d), lambda i, j: (0, 0))],
        out_specs=pl.BlockSpec((bm, d), lambda i, j: (i, 0)),
        out_shape=jax.ShapeDtypeStruct((m, d), F32),
        scratch_shapes=[pltpu.VMEM((bm, d), BF16), pltpu.VMEM((bm, d), F32)],
        compiler_params=_params(("parallel", "arbitrary")),
        name="ffn_ln",
    )(x, w_gu_stack, w_gu_stack, w_down_stack, g.reshape(1, d), b.reshape(1, d))


def _rglru_core_kernel(u_ref, cw_ref, cb_ref, wg_ref, bg_ref, lam_ref, o_ref,
                       tm_ref, a_ref, b_ref, hs_ref, h_ref, *, ts):
    nb = LRU_BLOCKS
    halo = (CONV_WIDTH - 1) * V7X_SUBLANES
    rows = ts * V7X_SUBLANES
    step = pl.program_id(0)

    @pl.when(step == 0)
    def _():
        h_ref[...] = jnp.zeros_like(h_ref)
        tm_ref[:, 0:halo, :] = jnp.zeros((2 * nb, halo, V7X_LANES), F32)

    @pl.when(step > 0)
    def _():
        tm_ref[:, 0:halo, :] = tm_ref[:, rows:rows + halo, :]

    for bi in range(V7X_SUBLANES):
        for c in range(2 * nb):
            tm_ref[c, pl.ds(halo + bi, ts, stride=V7X_SUBLANES), :] = u_ref[bi, :, c * V7X_LANES:(c + 1) * V7X_LANES]

    for n in range(nb):
        sl = slice(n * V7X_LANES, (n + 1) * V7X_LANES)
        xr = cb_ref[:, sl]
        for k in range(CONV_WIDTH):
            xr = xr + tm_ref[nb + n, k * V7X_SUBLANES:k * V7X_SUBLANES + rows, :] * cw_ref[k:k + 1, sl]
        z = _dot(xr.astype(BF16), wg_ref[n].astype(BF16)) + bg_ref[n]
        r = jax.nn.sigmoid(z[:, :V7X_LANES])
        i = jax.nn.sigmoid(z[:, V7X_LANES:])
        log_a = (-LRU_C * r) * _softplus(-lam_ref[:, sl])
        a = jnp.exp(log_a)
        a_ref[n] = a
        b_ref[n] = jnp.sqrt(-jnp.tanh(log_a) * (a * a + 1.0)) * (i * xr)

    def scan_step(t, hs):
        r0 = pl.multiple_of(t * V7X_SUBLANES, V7X_SUBLANES)
        new = []
        for n in range(nb):
            h = a_ref[n, pl.ds(r0, V7X_SUBLANES), :] * hs[n] + b_ref[n, pl.ds(r0, V7X_SUBLANES), :]
            hs_ref[n, pl.ds(r0, V7X_SUBLANES), :] = h
            new.append(h)
        return tuple(new)

    hs = lax.fori_loop(0, ts, scan_step, tuple(h_ref[n] for n in range(nb)), unroll=8)
    for n in range(nb):
        h_ref[n] = hs[n]

    for n in range(nb):
        a_ref[n] = _gelu_tanh(tm_ref[n, halo:halo + rows, :]) * hs_ref[n]
        for bi in range(V7X_SUBLANES):
            o_ref[bi, :, n * V7X_LANES:(n + 1) * V7X_LANES] = (
                a_ref[n, pl.ds(bi, ts, stride=V7X_SUBLANES), :].astype(o_ref.dtype))


def _rglru_core(u, conv_w, conv_b, w_gates, b_gates, lam, layer, ts):
    nbat, s, d2 = u.shape
    d = d2 // 2
    assert nbat == V7X_SUBLANES and d == LRU_BLOCKS * V7X_LANES and s % ts == 0
    rows = ts * V7X_SUBLANES
    halo = (CONV_WIDTH - 1) * V7X_SUBLANES
    gw = 2 * V7X_LANES
    return pl.pallas_call(
        functools.partial(_rglru_core_kernel, ts=ts),
        grid=(s // ts,),
        in_specs=[pl.BlockSpec((nbat, ts, d2), lambda t: (0, t, 0)),
                  pl.BlockSpec((CONV_WIDTH, d), lambda t: (0, 0)),
                  pl.BlockSpec((1, d), lambda t: (0, 0)),
                  pl.BlockSpec((None, LRU_BLOCKS, V7X_LANES, gw), lambda t: (layer, 0, 0, 0)),
                  pl.BlockSpec((LRU_BLOCKS, 1, gw), lambda t: (0, 0, 0)),
                  pl.BlockSpec((1, d), lambda t: (0, 0))],
        out_specs=pl.BlockSpec((nbat, ts, d), lambda t: (0, t, 0)),
        out_shape=jax.ShapeDtypeStruct((nbat, s, d), BF16),
        scratch_shapes=[pltpu.VMEM((2 * LRU_BLOCKS, halo + rows, V7X_LANES), F32),
                        pltpu.VMEM((LRU_BLOCKS, rows, V7X_LANES), F32),
                        pltpu.VMEM((LRU_BLOCKS, rows, V7X_LANES), F32),
                        pltpu.VMEM((LRU_BLOCKS, rows, V7X_LANES), F32),
                        pltpu.VMEM((LRU_BLOCKS, V7X_SUBLANES, V7X_LANES), F32)],
        compiler_params=_params(("arbitrary",)),
        name="rglru_core",
    )(u, conv_w[layer], conv_b[layer].reshape(1, d), w_gates, b_gates[layer].reshape(LRU_BLOCKS, 1, gw),
      lam[layer].reshape(1, d))


def _rglru_layer(xt, nbat, s, layer, w_in, conv_w, conv_b, w_gates, b_gates, lam, w_out, g, b):
    m, d = xt.shape
    bm = _row_block(m, ROW_TILE)
    u = _matmul(xt, w_in, layer, 2 * d, F32, bm, d)
    gh = _rglru_core(u.reshape(nbat, s, 2 * d), conv_w, conv_b, w_gates, b_gates, lam, layer,
                     ts=_row_block(s, RGLRU_TIME_TILE))
    return _mm_ln(gh.reshape(m, d), w_out, layer, xt, g, b, bm)


def _pool_ln_kernel(x_ref, prev_ref, w_ref, sc_ref, g_ref, b_ref, o_ref, e_ref, st_ref, y_ref, *, rc, tiles_per_seq):
    i = pl.program_id(0)
    first = (i % tiles_per_seq) == 0
    d = x_ref.shape[1]
    gw = d // len(POOL_WINDOWS)
    e_ref[0:POOL_HALO, :] = jnp.where(first, 0.0, prev_ref[...])
    e_ref[POOL_HALO:, :] = x_ref[...]
    t = ((i % tiles_per_seq) * rc + lax.broadcasted_iota(jnp.int32, (rc, 1), 0) + 1).astype(F32)
    for gi, wl in enumerate(POOL_WINDOWS):
        sl = slice(gi * gw, (gi + 1) * gw)
        lo = V7X_SUBLANES
        shift = 1
        cur = e_ref[lo:, sl] + e_ref[lo - shift:POOL_HALO + rc - shift, sl]
        while 2 * shift < wl:
            shift *= 2
            st_ref[lo:, :] = cur
            nlo = lo + V7X_SUBLANES
            cur = st_ref[nlo:, :] + st_ref[nlo - shift:POOL_HALO + rc - shift, :]
            lo = nlo
        win = cur[POOL_HALO - lo:, :]
        xg = x_ref[:, sl]
        p = win / jnp.minimum(t, float(wl)) - xg
        y_ref[:, sl] = _dot(p.astype(BF16), w_ref[gi].astype(BF16))
    y = y_ref[...] * sc_ref[...]
    o_ref[...] = _layer_norm(ALPHA * x_ref[...] + y, g_ref[...], b_ref[...])


def _pool_layer(xt, nbat, s, layer, pool_w, pool_scale, g, b):
    m, d = xt.shape
    rc = _row_block(s, POOL_ROWS)
    tiles_per_seq = s // rc
    ng = len(POOL_WINDOWS)
    gw = d // ng
    hb = rc // POOL_HALO
    return pl.pallas_call(
        functools.partial(_pool_ln_kernel, rc=rc, tiles_per_seq=tiles_per_seq),
        grid=(m // rc,),
        in_specs=[pl.BlockSpec((rc, d), lambda i: (i, 0)),
                  pl.BlockSpec((POOL_HALO, d), lambda i: (jnp.maximum(i * hb - 1, 0), 0)),
                  pl.BlockSpec((None, ng, gw, gw), lambda i: (layer, 0, 0, 0)),
                  pl.BlockSpec((1, d), lambda i: (0, 0)),
                  pl.BlockSpec((1, d), lambda i: (0, 0)),
                  pl.BlockSpec((1, d), lambda i: (0, 0))],
        out_specs=pl.BlockSpec((rc, d), lambda i: (i, 0)),
        out_shape=jax.ShapeDtypeStruct((m, d), F32),
        scratch_shapes=[pltpu.VMEM((POOL_HALO + rc, d), F32),
                        pltpu.VMEM((POOL_HALO + rc, gw), F32),
                        pltpu.VMEM((rc, d), F32)],
        compiler_params=_params(("parallel",)),
        name="pool_ln",
    )(xt, xt, pool_w, pool_scale[layer].reshape(1, d), g.reshape(1, d), b.reshape(1, d))


def _logf_kernel(x_ref, w_ref, b_ref, o_ref):
    o_ref[...] = _log_sigmoid(_dot(x_ref[...].astype(BF16), w_ref[...]) + b_ref[...])


def _logf(xt, w_f, b_f, bm):
    m, d = xt.shape
    h = w_f.shape[1]
    w_pad = jnp.pad(w_f, ((0, 0), (0, V7X_LANES - h))).astype(BF16)
    b_pad = jnp.pad(b_f, (0, V7X_LANES - h)).reshape(1, V7X_LANES)
    return pl.pallas_call(
        _logf_kernel,
        grid=(m // bm,),
        in_specs=[pl.BlockSpec((bm, d), lambda i: (i, 0)),
                  pl.BlockSpec((d, V7X_LANES), lambda i: (0, 0)),
                  pl.BlockSpec((1, V7X_LANES), lambda i: (0, 0))],
        out_specs=pl.BlockSpec((bm, V7X_LANES), lambda i: (i, 0)),
        out_shape=jax.ShapeDtypeStruct((m, V7X_LANES), F32),
        compiler_params=_params(("parallel",)),
        name="fox_logf",
    )(xt, w_pad, b_pad)


def _split3(v):
    hi = v.astype(BF16)
    r1 = v - hi.astype(F32)
    mid = r1.astype(BF16)
    lo = (r1 - mid.astype(F32)).astype(BF16)
    return hi, mid, lo


def _cumsum_kernel(lf_ref, f_ref, ft_ref, carry_ref):
    j = pl.program_id(1)

    @pl.when(j == 0)
    def _():
        carry_ref[...] = jnp.zeros_like(carry_ref)

    n = lf_ref.shape[0]
    row = lax.broadcasted_iota(jnp.int32, (n, n), 0)
    col = lax.broadcasted_iota(jnp.int32, (n, n), 1)
    tri = jnp.where(row >= col, 1.0, 0.0).astype(BF16)
    hi, mid, lo = _split3(lf_ref[...])
    cs = (_dot(tri, hi) + _dot(tri, mid) + _dot(tri, lo)) + carry_ref[0:1, :]
    f_ref[...] = cs
    ft_ref[...] = cs.T
    carry_ref[...] = jnp.broadcast_to(cs[n - 1:n, :], carry_ref.shape)


def _forget_cumsum(logf, nbat, s):
    n = _row_block(s, CUMSUM_ROWS)
    nj = s // n
    return pl.pallas_call(
        _cumsum_kernel,
        grid=(nbat, nj),
        in_specs=[pl.BlockSpec((n, V7X_LANES), lambda bi, j: (bi * nj + j, 0))],
        out_specs=[pl.BlockSpec((n, V7X_LANES), lambda bi, j: (bi * nj + j, 0)),
                   pl.BlockSpec((None, V7X_LANES, n), lambda bi, j: (bi, 0, j))],
        out_shape=[jax.ShapeDtypeStruct((nbat * s, V7X_LANES), F32),
                   jax.ShapeDtypeStruct((nbat, V7X_LANES, s), F32)],
        scratch_shapes=[pltpu.VMEM((V7X_SUBLANES, V7X_LANES), F32)],
        compiler_params=_params(("parallel", "arbitrary")),
        name="fox_cumsum",
    )(logf)


def _fox_attn_kernel(q_ref, k_ref, v_ref, fq_ref, fk_ref, o_ref, m_ref, l_ref, acc_ref, *, tq, dh):
    p = pl.program_id(1)
    qi = pl.program_id(2)
    lane = lax.broadcasted_iota(jnp.int32, (tq, V7X_LANES), 1)
    q = q_ref[...] * jnp.asarray(dh ** -0.5, q_ref.dtype)
    fq_all = fq_ref[...]
    row = lax.broadcasted_iota(jnp.int32, (tq, tq), 0)
    col = lax.broadcasted_iota(jnp.int32, (tq, tq), 1)
    reps = tq // V7X_LANES
    qh, fq = [], []
    for hh in range(2):
        in_head = (lane >= hh * dh) & (lane < (hh + 1) * dh)
        qh.append(jnp.where(in_head, q, jnp.zeros_like(q)))
        fq_col = jnp.sum(jnp.where(lane == 2 * p + hh, fq_all, 0.0), axis=1, keepdims=True)
        fq.append(jnp.broadcast_to(fq_col, (tq, V7X_LANES)))
    m_ref[...] = jnp.full(m_ref.shape, NEG_INF, F32)
    l_ref[...] = jnp.zeros_like(l_ref)
    acc_ref[...] = jnp.zeros_like(acc_ref)

    def kv_step(j, diagonal):
        r0 = pl.multiple_of(j * tq, tq)
        k = k_ref[pl.ds(r0, tq), :]
        v = v_ref[pl.ds(r0, tq), :]
        for hh in range(2):
            s = lax.dot_general(qh[hh], k, (((1,), (1,)), ((), ())), preferred_element_type=F32)
            t = s - fk_ref[hh:hh + 1, pl.ds(r0, tq)]
            if diagonal:
                t = jnp.where(col <= row, t, NEG_INF)
            m_prev = m_ref[hh]
            m_new = jnp.maximum(m_prev, fq[hh] + jnp.max(t, axis=1, keepdims=True))
            alpha = jnp.exp(m_prev - m_new)
            c = fq[hh] - m_new
            pr = jnp.exp(t + jnp.concatenate([c] * reps, axis=1))
            l_ref[hh] = alpha * l_ref[hh] + jnp.sum(pr, axis=1, keepdims=True)
            acc_ref[hh] = alpha * acc_ref[hh] + _dot(pr.astype(BF16), v)
            m_ref[hh] = m_new

    def body(j, carry):
        kv_step(j, False)
        return carry

    lax.fori_loop(0, qi, body, 0)
    kv_step(qi, True)
    o_ref[...] = jnp.where(lane < dh, acc_ref[0] / l_ref[0], acc_ref[1] / l_ref[1]).astype(o_ref.dtype)


def _fox_attention(qkv, f_rows, f_cols, nbat, s, d):
    dh = d // FOX_HEADS
    npair = FOX_HEADS // 2
    assert 2 * dh == V7X_LANES
    tq = _row_block(s, ATTN_TILE)
    nq = s // tq
    return pl.pallas_call(
        functools.partial(_fox_attn_kernel, tq=tq, dh=dh),
        grid=(nbat, npair, nq),
        in_specs=[pl.BlockSpec((tq, V7X_LANES), lambda bi, p, qi: (bi * nq + qi, p)),
                  pl.BlockSpec((s, V7X_LANES), lambda bi, p, qi: (bi, npair + p)),
                  pl.BlockSpec((s, V7X_LANES), lambda bi, p, qi: (bi, 2 * npair + p)),
                  pl.BlockSpec((tq, V7X_LANES), lambda bi, p, qi: (bi * nq + qi, 0)),
                  pl.BlockSpec((None, None, V7X_SUBLANES, s), lambda bi, p, qi: (bi, p, 0, 0))],
        out_specs=pl.BlockSpec((tq, V7X_LANES), lambda bi, p, qi: (bi * nq + qi, p)),
        out_shape=jax.ShapeDtypeStruct((nbat * s, d), BF16),
        scratch_shapes=[pltpu.VMEM((2, tq, V7X_LANES), F32)] * 3,
        compiler_params=_params(("parallel", "parallel", "arbitrary")),
        name="fox_attn",
    )(qkv, qkv, qkv, f_rows, f_cols)


def _fox_layer(xt, nbat, s, layer, w_qkvf, b_f, w_o, g, b):
    m, d = xt.shape
    bm = _row_block(m, ROW_TILE)
    qkv = _matmul(xt, w_qkvf, layer, 3 * d, BF16, bm, d)
    logf = _logf(xt, w_qkvf[layer][:, 3 * d:], b_f[layer], bm)
    f_rows, f_t = _forget_cumsum(logf, nbat, s)
    npair = FOX_HEADS // 2
    f_cols = jnp.pad(f_t[:, :FOX_HEADS, :].reshape(nbat, npair, 2, s), ((0, 0), (0, 0), (0, V7X_SUBLANES - 2), (0, 0)))
    o = _fox_attention(qkv, f_rows, f_cols, nbat, s, d)
    return _mm_ln(o, w_o, layer, xt, g, b, bm)


PACK_E1, PACK_E2, PACK_G1, PACK_G2 = 0, 1, 2, 3


def _to_slabs(slab_ref, value):
    rows = value.shape[0]
    for c in range(V7X_SUBLANES):
        slab_ref[pl.ds(c, rows, stride=V7X_SUBLANES), :] = value[:, c * V7X_LANES:(c + 1) * V7X_LANES]


def _from_slabs(slab_ref, rows):
    return jnp.concatenate([slab_ref[pl.ds(c, rows, stride=V7X_SUBLANES), :] for c in range(V7X_SUBLANES)], axis=1)


def _slab_rows(ref, row):
    return ref.at[pl.ds(pl.multiple_of(row * V7X_SUBLANES, V7X_SUBLANES), V7X_SUBLANES)]


def _wait_many(copy, count):
    assert count % WAIT_UNROLL == 0

    def body(_, c):
        for _ in range(WAIT_UNROLL):
            copy.wait()
        return c

    lax.fori_loop(0, count // WAIT_UNROLL, body, 0)


def _router_kernel(x_ref, w_ref, r_ref, cnt_ref):
    i = pl.program_id(0)

    @pl.when(i == 0)
    def _():
        cnt_ref[...] = jnp.zeros_like(cnt_ref)

    x = x_ref[...]
    xh = x.astype(BF16)
    xl = (x - xh.astype(F32)).astype(BF16)
    w = w_ref[...]
    wh = w.astype(BF16)
    wl = (w - wh.astype(F32)).astype(BF16)
    logits = (_dot(xh, wh) + _dot(xl, wh)) + _dot(xh, wl)
    lane = lax.broadcasted_iota(jnp.int32, logits.shape, 1)
    logits = jnp.where(lane < N_EXPERTS, logits, NEG_INF)
    m1 = jnp.max(logits, axis=1, keepdims=True)
    i1 = jnp.min(jnp.where(logits == m1, lane, V7X_LANES), axis=1, keepdims=True)
    rest = jnp.where(lane == i1, NEG_INF, logits)
    m2 = jnp.max(rest, axis=1, keepdims=True)
    i2 = jnp.min(jnp.where(rest == m2, lane, V7X_LANES), axis=1, keepdims=True)
    e21 = jnp.exp(m2 - m1)
    g1 = 1.0 / (1.0 + e21)
    g2 = e21 * g1
    out = jnp.where(lane == PACK_E1, i1.astype(F32), 0.0)
    out = jnp.where(lane == PACK_E2, i2.astype(F32), out)
    out = jnp.where(lane == PACK_G1, g1, out)
    out = jnp.where(lane == PACK_G2, g2, out)
    r_ref[...] = out
    hit = jnp.where((lane == i1) | (lane == i2), 1.0, 0.0)
    cnt_ref[...] += jnp.broadcast_to(jnp.sum(hit, axis=0, keepdims=True), cnt_ref.shape)


def _router(xt, w_router, bm):
    m, d = xt.shape
    w_pad = jnp.pad(w_router, ((0, 0), (0, V7X_LANES - w_router.shape[1])))
    return pl.pallas_call(
        _router_kernel,
        grid=(m // bm,),
        in_specs=[pl.BlockSpec((bm, d), lambda i: (i, 0)),
                  pl.BlockSpec((d, V7X_LANES), lambda i: (0, 0))],
        out_specs=[pl.BlockSpec((bm, V7X_LANES), lambda i: (i, 0)),
                   pl.BlockSpec((V7X_SUBLANES, V7X_LANES), lambda i: (0, 0))],
        out_shape=[jax.ShapeDtypeStruct((m, V7X_LANES), F32),
                   jax.ShapeDtypeStruct((V7X_SUBLANES, V7X_LANES), F32)],
        compiler_params=_params(("arbitrary",)),
        name="moe_router",
    )(xt, w_pad)


def _dispatch_kernel(cnt_ref, start_ref, x_ref, r_ref, ps_ref, d_ref, rows_hbm, xs_ref, zero_ref, dvm_ref, dsm_ref,
                     carry_ref, sem_rows, sem_idx, *, bm):
    i = pl.program_id(0)
    nsteps = pl.num_programs(0)

    @pl.when(i == 0)
    def _():
        carry_ref[...] = jnp.zeros_like(carry_ref)

    x = x_ref[...]
    r = r_ref[...]
    lane = lax.broadcasted_iota(jnp.int32, r.shape, 1)
    lane_f = lane.astype(F32)
    oh1 = lane_f == r[:, PACK_E1:PACK_E1 + 1]
    oh2 = lane_f == r[:, PACK_E2:PACK_E2 + 1]
    hit = jnp.where(oh1 | oh2, 1.0, 0.0)
    row = lax.broadcasted_iota(jnp.int32, (bm, bm), 0)
    col = lax.broadcasted_iota(jnp.int32, (bm, bm), 1)
    strict = jnp.where(row > col, 1.0, 0.0).astype(BF16)
    before = _dot(strict, hit.astype(BF16)) + carry_ref[0:1, :]
    total = before[bm - 1:bm, :] + hit[bm - 1:bm, :]
    carry_ref[...] = jnp.broadcast_to(total, carry_ref.shape)
    place = before + ps_ref[...]
    d1 = jnp.sum(jnp.where(oh1, place, 0.0), axis=1, keepdims=True)
    d2 = jnp.sum(jnp.where(oh2, place, 0.0), axis=1, keepdims=True)
    dmat = jnp.where(lane == 0, d1, jnp.where(lane == 1, d2, 0.0))
    d_ref[...] = dmat

    dvm_ref[...] = dmat.T[0:V7X_SUBLANES, :].astype(jnp.int32)
    idx_copy = pltpu.make_async_copy(dvm_ref, dsm_ref, sem_idx)
    idx_copy.start()
    idx_copy.wait()

    def row_copy(slot, t, k):
        return pltpu.make_async_copy(_slab_rows(xs_ref.at[slot], t), _slab_rows(rows_hbm, dsm_ref[k, t]), sem_rows.at[slot])

    def wait_rows(slot):
        _wait_many(pltpu.make_async_copy(_slab_rows(xs_ref.at[slot], 0), _slab_rows(rows_hbm, 0), sem_rows.at[slot]),
                   TOP_K * bm)

    for slot in range(2):
        @pl.when(i % 2 == slot)
        def _(slot=slot):
            @pl.when(i >= 2)
            def _():
                wait_rows(slot)

            _to_slabs(xs_ref.at[slot], x)

            def start(t, c):
                for k in range(TOP_K):
                    row_copy(slot, t, k).start()
                return c

            lax.fori_loop(0, bm, start, 0)

            @pl.when(i == nsteps - 1)
            def _():
                wait_rows(slot)

            @pl.when((i == nsteps - 1) & (i >= 1))
            def _():
                wait_rows(1 - slot)

    @pl.when(i == nsteps - 1)
    def _():
        zero_ref[...] = jnp.zeros_like(zero_ref)
        for e in range(N_EXPERTS):
            c0 = start_ref[e] + cnt_ref[e]
            c1 = start_ref[e + 1]

            def pad_copy(row):
                return pltpu.make_async_copy(zero_ref, _slab_rows(rows_hbm, row), sem_idx)

            def start_pad(row, c):
                pad_copy(row).start()
                return c

            def wait_pad(row, c):
                pad_copy(row).wait()
                return c

            lax.fori_loop(c0, c1, start_pad, 0)
            lax.fori_loop(c0, c1, wait_pad, 0)


def _dispatch(xt, routed, counts, starts, n_rows, bm):
    m, d = xt.shape
    start_row = jnp.pad(starts[:N_EXPERTS].astype(F32), (0, V7X_LANES - N_EXPERTS)).reshape(1, V7X_LANES)
    grid_spec = pltpu.PrefetchScalarGridSpec(
        num_scalar_prefetch=2,
        grid=(m // bm,),
        in_specs=[pl.BlockSpec((bm, d), lambda i, cnt, st: (i, 0)),
                  pl.BlockSpec((bm, V7X_LANES), lambda i, cnt, st: (i, 0)),
                  pl.BlockSpec((1, V7X_LANES), lambda i, cnt, st: (0, 0))],
        out_specs=[pl.BlockSpec((bm, V7X_LANES), lambda i, cnt, st: (i, 0)),
                   pl.BlockSpec(memory_space=pl.ANY)],
        scratch_shapes=[pltpu.VMEM((2, bm * V7X_SUBLANES, V7X_LANES), F32),
                        pltpu.VMEM((V7X_SUBLANES, V7X_LANES), F32),
                        pltpu.VMEM((V7X_SUBLANES, bm), jnp.int32),
                        pltpu.SMEM((V7X_SUBLANES, bm), jnp.int32),
                        pltpu.VMEM((V7X_SUBLANES, V7X_LANES), F32),
                        pltpu.SemaphoreType.DMA((2,)),
                        pltpu.SemaphoreType.DMA],
    )
    return pl.pallas_call(
        functools.partial(_dispatch_kernel, bm=bm),
        grid_spec=grid_spec,
        out_shape=[jax.ShapeDtypeStruct((m, V7X_LANES), F32),
                   jax.ShapeDtypeStruct((n_rows * V7X_SUBLANES, V7X_LANES), F32)],
        compiler_params=_params(("arbitrary",)),
        name="moe_dispatch",
    )(counts, starts, xt, routed, start_row)


def _experts_kernel(ie_ref, iv_ref, ni_ref, x_ref, wg_ref, wu_ref, wd_ref, o_ref, xb_ref, acc_ref, *, sb, rb):
    i = pl.program_id(0)
    j = pl.program_id(1)
    last = pl.num_programs(1) - 1
    valid = iv_ref[i]
    used = i < ni_ref[0]

    @pl.when(jnp.logical_not(used) & (j == last))
    def _():
        o_ref[...] = jnp.zeros_like(o_ref)

    @pl.when(used)
    def _():
        @pl.when(j == 0)
        def _():
            xb_ref[...] = _from_slabs(x_ref, sb).astype(BF16)
            acc_ref[...] = jnp.zeros_like(acc_ref)

        def run(nrows):
            xb = xb_ref[0:nrows, :]
            h = _silu_mul(_dot(xb, wg_ref[...].astype(BF16)), _dot(xb, wu_ref[...].astype(BF16))).astype(BF16)
            acc_ref[0:nrows, :] += _dot(h, wd_ref[...].astype(BF16))

        @pl.when(valid > rb)
        def _():
            run(sb)

        @pl.when(valid <= rb)
        def _():
            run(rb)

        @pl.when(j == last)
        def _():
            _to_slabs(o_ref, acc_ref[...])


def _experts(rows, w_gu, w_down, layer, item_expert, item_valid, n_items, sb, rb, fc):
    assert sb == 2 * rb
    d = w_down.shape[3]
    f = w_down.shape[2]
    nf = f // fc
    n_max = item_expert.shape[0]
    ssb = sb * V7X_SUBLANES

    def item(i, ni):
        return jnp.minimum(i, ni[0] - 1)

    def chunk(i, j, ni):
        return jnp.where(i < ni[0], j, nf - 1)

    grid_spec = pltpu.PrefetchScalarGridSpec(
        num_scalar_prefetch=3,
        grid=(n_max, nf),
        in_specs=[pl.BlockSpec((ssb, V7X_LANES), lambda i, j, ie, iv, ni: (item(i, ni), 0)),
                  pl.BlockSpec((None, None, d, fc), lambda i, j, ie, iv, ni: (layer, ie[item(i, ni)], 0, chunk(i, j, ni))),
                  pl.BlockSpec((None, None, d, fc), lambda i, j, ie, iv, ni: (layer, ie[item(i, ni)], 0, nf + chunk(i, j, ni))),
                  pl.BlockSpec((None, None, fc, d), lambda i, j, ie, iv, ni: (layer, ie[item(i, ni)], chunk(i, j, ni), 0))],
        out_specs=pl.BlockSpec((ssb, V7X_LANES), lambda i, j, ie, iv, ni: (i, 0)),
        scratch_shapes=[pltpu.VMEM((sb, d), BF16), pltpu.VMEM((sb, d), F32)],
    )
    return pl.pallas_call(
        functools.partial(_experts_kernel, sb=sb, rb=rb),
        grid_spec=grid_spec,
        out_shape=jax.ShapeDtypeStruct(rows.shape, F32),
        compiler_params=_params(("arbitrary", "arbitrary")),
        name="moe_experts",
    )(item_expert, item_valid, n_items, rows, w_gu, w_gu, w_down)


def _combine_ln_kernel(dcur_ref, dnext_ref, x_ref, r_ref, g_ref, b_ref, y_hbm, o_ref, ybuf, sem, *, tb):
    i = pl.program_id(0)
    nsteps = pl.num_programs(0)

    def issue(dref, slot):
        def start(t, c):
            for k in range(TOP_K):
                pltpu.make_async_copy(_slab_rows(y_hbm, dref[0, 0, TOP_K * t + k]), _slab_rows(ybuf.at[slot, k], t),
                                      sem.at[slot]).start()
            return c

        lax.fori_loop(0, tb, start, 0)

    for slot in range(2):
        @pl.when(i % 2 == slot)
        def _(slot=slot):
            if slot == 0:
                @pl.when(i == 0)
                def _():
                    issue(dcur_ref, 0)

            @pl.when(i + 1 < nsteps)
            def _():
                issue(dnext_ref, 1 - slot)

            _wait_many(pltpu.make_async_copy(_slab_rows(y_hbm, 0), _slab_rows(ybuf.at[slot, 0], 0), sem.at[slot]),
                       TOP_K * tb)
            r = r_ref[...]
            y = (r[:, PACK_G1:PACK_G1 + 1] * _from_slabs(ybuf.at[slot, 0], tb)
                 + r[:, PACK_G2:PACK_G2 + 1] * _from_slabs(ybuf.at[slot, 1], tb))
            o_ref[...] = _layer_norm(ALPHA * x_ref[...] + y, g_ref[...], b_ref[...])


def _combine_ln(xt, routed, dest, yrows, g, b, tb):
    m, d = xt.shape
    nsteps = m // tb
    dest3 = dest.reshape(nsteps, 1, TOP_K * tb)
    return pl.pallas_call(
        functools.partial(_combine_ln_kernel, tb=tb),
        grid=(nsteps,),
        in_specs=[pl.BlockSpec((1, 1, TOP_K * tb), lambda i: (i, 0, 0), memory_space=pltpu.SMEM),
                  pl.BlockSpec((1, 1, TOP_K * tb), lambda i: (jnp.minimum(i + 1, nsteps - 1), 0, 0), memory_space=pltpu.SMEM),
                  pl.BlockSpec((tb, d), lambda i: (i, 0)),
                  pl.BlockSpec((tb, V7X_LANES), lambda i: (i, 0)),
                  pl.BlockSpec((1, d), lambda i: (0, 0)),
                  pl.BlockSpec((1, d), lambda i: (0, 0)),
                  pl.BlockSpec(memory_space=pl.ANY)],
        out_specs=pl.BlockSpec((tb, d), lambda i: (i, 0)),
        out_shape=jax.ShapeDtypeStruct((m, d), F32),
        scratch_shapes=[pltpu.VMEM((2, TOP_K, tb * V7X_SUBLANES, V7X_LANES), F32), pltpu.SemaphoreType.DMA((2,))],
        compiler_params=_params(("arbitrary",)),
        name="moe_combine_ln",
    )(dest3, dest3, xt, routed, g.reshape(1, d), b.reshape(1, d), yrows)


def _moe_layer(xt, layer, w_router, w_gu, w_down, g, b):
    m, d = xt.shape
    sb = _row_block(m, MOE_SUPER_ROWS)
    rb = _row_block(sb, MOE_ROW_BLOCK)
    bm = _row_block(m, ROUTE_ROWS)
    routed, cnt = _router(xt, w_router[layer], bm)
    counts = cnt[0, :N_EXPERTS].astype(jnp.int32)
    nsb = (counts + sb - 1) // sb
    iend = jnp.cumsum(nsb)
    n_max = (m * TOP_K) // sb + N_EXPERTS
    starts = jnp.concatenate([(iend - nsb) * sb, jnp.full((1,), n_max * sb, jnp.int32)]).astype(jnp.int32)
    item = jnp.arange(n_max, dtype=jnp.int32)
    item_expert = jnp.minimum(jnp.searchsorted(iend, item, side="right"), N_EXPERTS - 1).astype(jnp.int32)
    item_valid = jnp.clip(starts[item_expert] + counts[item_expert] - item * sb, 0, sb).astype(jnp.int32)
    n_items = iend[-1:].astype(jnp.int32)
    dest_f, rows = _dispatch(xt, routed, counts, starts, n_max * sb, bm)
    yrows = _experts(rows, w_gu, w_down, layer, item_expert, item_valid, n_items, sb, rb, MOE_FF_CHUNK)
    dest = dest_f[:, :TOP_K].astype(jnp.int32)
    return _combine_ln(xt, routed, dest, yrows, g, b, _row_block(m, COMBINE_ROWS))


def kernel(x, lru_w_in, lru_conv_w, lru_conv_b, lru_w_gates, lru_b_gates, lru_lambda, lru_w_out, pool_w, pool_scale,
           fox_w_qkvf, fox_b_f, fox_w_o, ffn_w_gu, ffn_w_down, moe_router, moe_w_gu, moe_w_down,
           ln_mix_g, ln_mix_b, ln_ffn_g, ln_ffn_b):
    nbat, s, d = x.shape
    xt = x.reshape(nbat * s, d)
    for i in range(DEPTH):
        mixer, j = i % 3, i // 3
        if mixer == 0:
            xt = _rglru_layer(xt, nbat, s, j, lru_w_in, lru_conv_w, lru_conv_b, lru_w_gates, lru_b_gates,
                              lru_lambda, lru_w_out, ln_mix_g[i], ln_mix_b[i])
        elif mixer == 1:
            xt = _pool_layer(xt, nbat, s, j, pool_w, pool_scale, ln_mix_g[i], ln_mix_b[i])
        else:
            xt = _fox_layer(xt, nbat, s, j, fox_w_qkvf, fox_b_f, fox_w_o, ln_mix_g[i], ln_mix_b[i])
        if i % 2 == 0:
            xt = _ffn_ln(xt, ffn_w_gu, ffn_w_down, i // 2, ln_ffn_g[i], ln_ffn_b[i], _row_block(nbat * s, ROW_TILE), FFN_CHUNK)
        else:
            xt = _moe_layer(xt, i // 2, moe_router, moe_w_gu, moe_w_down, ln_ffn_g[i], ln_ffn_b[i])
    return xt.reshape(nbat, s, d)
```

```python
import functools

import jax
import jax.numpy as jnp
from jax import lax
from jax.experimental import pallas as pl
from jax.experimental.pallas import tpu as pltpu

F32 = jnp.float32
BF16 = jnp.bfloat16

DEPTH = 4
LRU_BLOCKS = 8
CONV_WIDTH = 4
LRU_C = 8.0
POOL_WINDOWS = (2, 4, 8, 16)
FOX_HEADS = 16
N_EXPERTS = 8
TOP_K = 2
LN_EPS = 1e-5
NEG_INF = -1e30
ALPHA = (2 * DEPTH) ** 0.25

V7X_LANES = 128
V7X_SUBLANES = 8
V7X_VMEM_LIMIT_BYTES = 58 * 1024 * 1024

ROW_TILE = 1024
FFN_CHUNK = 512
RGLRU_TIME_TILE = 64
POOL_ROWS = 512
POOL_HALO = 32
ATTN_TILE = 512
CUMSUM_ROWS = 256
ROUTE_ROWS = 512
MOE_SUPER_ROWS = 1024
MOE_ROW_BLOCK = 512
MOE_FF_CHUNK = 896
COMBINE_ROWS = 256
WAIT_UNROLL = 32


def _params(semantics, vmem=V7X_VMEM_LIMIT_BYTES):
    return pltpu.CompilerParams(dimension_semantics=semantics, vmem_limit_bytes=vmem)


def _row_block(m, want):
    return want if m % want == 0 else m


def _layer_norm(y, g, b):
    mu = jnp.mean(y, axis=-1, keepdims=True)
    yc = y - mu
    var = jnp.mean(yc * yc, axis=-1, keepdims=True)
    return yc * lax.rsqrt(var + LN_EPS) * g + b


def _dot(a, b):
    return jnp.dot(a, b, preferred_element_type=F32)


def _softplus(x):
    return jnp.maximum(x, 0.0) + jnp.log1p(jnp.exp(-jnp.abs(x)))


def _log_sigmoid(z):
    return -_softplus(-z)


def _gelu_tanh(x):
    return 0.5 * x * (1.0 + jnp.tanh(0.7978845608028654 * (x + 0.044715 * (x * x * x))))


def _silu_mul(g, u):
    return g * jax.nn.sigmoid(g) * u


def _matmul_kernel(x_ref, w_ref, o_ref):
    o_ref[...] = _dot(x_ref[...].astype(BF16), w_ref[...].astype(BF16)).astype(o_ref.dtype)


def _matmul(x, w_stack, layer, n_out, out_dtype, bm, bn):
    m, k = x.shape
    return pl.pallas_call(
        _matmul_kernel,
        grid=(m // bm, n_out // bn),
        in_specs=[pl.BlockSpec((bm, k), lambda i, j: (i, 0)),
                  pl.BlockSpec((None, k, bn), lambda i, j: (layer, 0, j))],
        out_specs=pl.BlockSpec((bm, bn), lambda i, j: (i, j)),
        out_shape=jax.ShapeDtypeStruct((m, n_out), out_dtype),
        compiler_params=_params(("parallel", "arbitrary")),
        name="matmul",
    )(x, w_stack)


def _mm_ln_kernel(a_ref, w_ref, x_ref, g_ref, b_ref, o_ref):
    y = _dot(a_ref[...].astype(BF16), w_ref[...].astype(BF16))
    o_ref[...] = _layer_norm(ALPHA * x_ref[...] + y, g_ref[...], b_ref[...])


def _mm_ln(a, w_stack, layer, x, g, b, bm):
    m, k = a.shape
    d = w_stack.shape[2]
    return pl.pallas_call(
        _mm_ln_kernel,
        grid=(m // bm,),
        in_specs=[pl.BlockSpec((bm, k), lambda i: (i, 0)),
                  pl.BlockSpec((None, k, d), lambda i: (layer, 0, 0)),
                  pl.BlockSpec((bm, d), lambda i: (i, 0)),
                  pl.BlockSpec((1, d), lambda i: (0, 0)),
                  pl.BlockSpec((1, d), lambda i: (0, 0))],
        out_specs=pl.BlockSpec((bm, d), lambda i: (i, 0)),
        out_shape=jax.ShapeDtypeStruct((m, d), F32),
        compiler_params=_params(("parallel",)),
        name="mm_ln",
    )(a, w_stack, x, g.reshape(1, d), b.reshape(1, d))


def _ffn_ln_kernel(x_ref, wg_ref, wu_ref, wd_ref, g_ref, b_ref, o_ref, xb_ref, acc_ref):
    j = pl.program_id(1)

    @pl.when(j == 0)
    def _():
        xb_ref[...] = x_ref[...].astype(BF16)
        acc_ref[...] = jnp.zeros_like(acc_ref)

    xb = xb_ref[...]
    h = _silu_mul(_dot(xb, wg_ref[...].astype(BF16)), _dot(xb, wu_ref[...].astype(BF16))).astype(BF16)
    acc_ref[...] += _dot(h, wd_ref[...].astype(BF16))

    @pl.when(j == pl.num_programs(1) - 1)
    def _():
        o_ref[...] = _layer_norm(ALPHA * x_ref[...] + acc_ref[...], g_ref[...], b_ref[...])


def _ffn_ln(x, w_gu_stack, w_down_stack, layer, g, b, bm, fc):
    m, d = x.shape
    f = w_down_stack.shape[1]
    nf = f // fc
    return pl.pallas_call(
        _ffn_ln_kernel,
        grid=(m // bm, nf),
        in_specs=[pl.BlockSpec((bm, d), lambda i, j: (i, 0)),
                  pl.BlockSpec((None, d, fc), lambda i, j: (layer, 0, j)),
                  pl.BlockSpec((None, d, fc), lambda i, j: (layer, 0, nf + j)),
                  pl.BlockSpec((None, fc, d), lambda i, j: (layer, j, 0)),
                  pl.BlockSpec((1, d), lambda i, j: (0, 0)),
                  pl.BlockSpec((1, d), lambda i, j: (0, 0))],
        out_specs=pl.BlockSpec((bm, d), lambda i, j: (i, 0)),
        out_shape=jax.ShapeDtypeStruct((m, d), F32),
        scratch_shapes=[pltpu.VMEM((bm, d), BF16), pltpu.VMEM((bm, d), F32)],
        compiler_params=_params(("parallel", "arbitrary")),
        name="ffn_ln",
    )(x, w_gu_stack, w_gu_stack, w_down_stack, g.reshape(1, d), b.reshape(1, d))


def _rglru_kernel(x_ref, win_ref, cw_ref, cb_ref, wg_ref, bg_ref, lam_ref, wout_ref, g_ref, bt_ref, o_ref,
                  xs_ref, tm_ref, a_ref, b_ref, hs_ref, h_ref, *, ts):
    nb = LRU_BLOCKS
    halo = (CONV_WIDTH - 1) * V7X_SUBLANES
    rows = ts * V7X_SUBLANES
    step = pl.program_id(0)

    @pl.when(step == 0)
    def _():
        h_ref[...] = jnp.zeros_like(h_ref)
        tm_ref[:, 0:halo, :] = jnp.zeros((2 * nb, halo, V7X_LANES), F32)

    @pl.when(step > 0)
    def _():
        tm_ref[:, 0:halo, :] = tm_ref[:, rows:rows + halo, :]

    for bi in range(V7X_SUBLANES):
        for c in range(nb):
            xs_ref[c, pl.ds(bi, ts, stride=V7X_SUBLANES), :] = x_ref[bi, :, c * V7X_LANES:(c + 1) * V7X_LANES]
    x_tm = jnp.concatenate([xs_ref[c] for c in range(nb)], axis=1)
    xb = x_tm.astype(BF16)
    for c2 in range(nb):
        u2 = _dot(xb, win_ref[:, 2 * c2 * V7X_LANES:(2 * c2 + 2) * V7X_LANES])
        tm_ref[2 * c2, halo:halo + rows, :] = u2[:, :V7X_LANES]
        tm_ref[2 * c2 + 1, halo:halo + rows, :] = u2[:, V7X_LANES:]

    for n in range(nb):
        sl = slice(n * V7X_LANES, (n + 1) * V7X_LANES)
        xr = cb_ref[:, sl]
        for k in range(CONV_WIDTH):
            xr = xr + tm_ref[nb + n, k * V7X_SUBLANES:k * V7X_SUBLANES + rows, :] * cw_ref[k:k + 1, sl]
        z = _dot(xr.astype(BF16), wg_ref[n].astype(BF16)) + bg_ref[n]
        r = jax.nn.sigmoid(z[:, :V7X_LANES])
        i = jax.nn.sigmoid(z[:, V7X_LANES:])
        log_a = (-LRU_C * r) * _softplus(-lam_ref[:, sl])
        a = jnp.exp(log_a)
        a_ref[n] = a
        b_ref[n] = jnp.sqrt(-jnp.tanh(log_a) * (a * a + 1.0)) * (i * xr)

    def scan_step(t, hs):
        r0 = pl.multiple_of(t * V7X_SUBLANES, V7X_SUBLANES)
        new = []
        for n in range(nb):
            h = a_ref[n, pl.ds(r0, V7X_SUBLANES), :] * hs[n] + b_ref[n, pl.ds(r0, V7X_SUBLANES), :]
            hs_ref[n, pl.ds(r0, V7X_SUBLANES), :] = h
            new.append(h)
        return tuple(new)

    hs = lax.fori_loop(0, ts, scan_step, tuple(h_ref[n] for n in range(nb)), unroll=8)
    for n in range(nb):
        h_ref[n] = hs[n]

    for n in range(nb):
        a_ref[n] = _gelu_tanh(tm_ref[n, halo:halo + rows, :]) * hs_ref[n]
    gh = jnp.concatenate([a_ref[n] for n in range(nb)], axis=1).astype(BF16)
    y = _layer_norm(ALPHA * x_tm + _dot(gh, wout_ref[...]), g_ref[...], bt_ref[...])
    for c in range(nb):
        b_ref[c] = y[:, c * V7X_LANES:(c + 1) * V7X_LANES]
    for bi in range(V7X_SUBLANES):
        for c in range(nb):
            o_ref[bi, :, c * V7X_LANES:(c + 1) * V7X_LANES] = b_ref[c, pl.ds(bi, ts, stride=V7X_SUBLANES), :]


def _rglru_layer(xt, nbat, s, layer, w_in, conv_w, conv_b, w_gates, b_gates, lam, w_out, g, b):
    m, d = xt.shape
    ts = _row_block(s, RGLRU_TIME_TILE)
    assert nbat == V7X_SUBLANES and d == LRU_BLOCKS * V7X_LANES
    rows = ts * V7X_SUBLANES
    halo = (CONV_WIDTH - 1) * V7X_SUBLANES
    gw = 2 * V7X_LANES
    slab = pltpu.VMEM((LRU_BLOCKS, rows, V7X_LANES), F32)
    out = pl.pallas_call(
        functools.partial(_rglru_kernel, ts=ts),
        grid=(s // ts,),
        in_specs=[pl.BlockSpec((nbat, ts, d), lambda t: (0, t, 0)),
                  pl.BlockSpec((d, 2 * d), lambda t: (0, 0)),
                  pl.BlockSpec((CONV_WIDTH, d), lambda t: (0, 0)),
                  pl.BlockSpec((1, d), lambda t: (0, 0)),
                  pl.BlockSpec((None, LRU_BLOCKS, V7X_LANES, gw), lambda t: (layer, 0, 0, 0)),
                  pl.BlockSpec((LRU_BLOCKS, 1, gw), lambda t: (0, 0, 0)),
                  pl.BlockSpec((1, d), lambda t: (0, 0)),
                  pl.BlockSpec((d, d), lambda t: (0, 0)),
                  pl.BlockSpec((1, d), lambda t: (0, 0)),
                  pl.BlockSpec((1, d), lambda t: (0, 0))],
        out_specs=pl.BlockSpec((nbat, ts, d), lambda t: (0, t, 0)),
        out_shape=jax.ShapeDtypeStruct((nbat, s, d), F32),
        scratch_shapes=[slab,
                        pltpu.VMEM((2 * LRU_BLOCKS, halo + rows, V7X_LANES), F32),
                        slab, slab, slab,
                        pltpu.VMEM((LRU_BLOCKS, V7X_SUBLANES, V7X_LANES), F32)],
        compiler_params=_params(("arbitrary",)),
        name="rglru",
    )(xt.reshape(nbat, s, d), w_in[layer].astype(BF16), conv_w[layer], conv_b[layer].reshape(1, d), w_gates,
      b_gates[layer].reshape(LRU_BLOCKS, 1, gw), lam[layer].reshape(1, d), w_out[layer].astype(BF16),
      g.reshape(1, d), b.reshape(1, d))
    return out.reshape(m, d)


def _pool_ln_kernel(x_ref, prev_ref, w_ref, sc_ref, g_ref, b_ref, o_ref, e_ref, st_ref, y_ref, *, rc, tiles_per_seq):
    i = pl.program_id(0)
    first = (i % tiles_per_seq) == 0
    d = x_ref.shape[1]
    gw = d // len(POOL_WINDOWS)
    e_ref[0:POOL_HALO, :] = jnp.where(first, 0.0, prev_ref[...])
    e_ref[POOL_HALO:, :] = x_ref[...]
    t = ((i % tiles_per_seq) * rc + lax.broadcasted_iota(jnp.int32, (rc, 1), 0) + 1).astype(F32)
    for gi, wl in enumerate(POOL_WINDOWS):
        sl = slice(gi * gw, (gi + 1) * gw)
        lo = V7X_SUBLANES
        shift = 1
        cur = e_ref[lo:, sl] + e_ref[lo - shift:POOL_HALO + rc - shift, sl]
        while 2 * shift < wl:
            shift *= 2
            st_ref[lo:, :] = cur
            nlo = lo + V7X_SUBLANES
            cur = st_ref[nlo:, :] + st_ref[nlo - shift:POOL_HALO + rc - shift, :]
            lo = nlo
        win = cur[POOL_HALO - lo:, :]
        xg = x_ref[:, sl]
        p = win / jnp.minimum(t, float(wl)) - xg
        y_ref[:, sl] = _dot(p.astype(BF16), w_ref[gi].astype(BF16))
    y = y_ref[...] * sc_ref[...]
    o_ref[...] = _layer_norm(ALPHA * x_ref[...] + y, g_ref[...], b_ref[...])


def _pool_layer(xt, nbat, s, layer, pool_w, pool_scale, g, b):
    m, d = xt.shape
    rc = _row_block(s, POOL_ROWS)
    tiles_per_seq = s // rc
    ng = len(POOL_WINDOWS)
    gw = d // ng
    hb = rc // POOL_HALO
    return pl.pallas_call(
        functools.partial(_pool_ln_kernel, rc=rc, tiles_per_seq=tiles_per_seq),
        grid=(m // rc,),
        in_specs=[pl.BlockSpec((rc, d), lambda i: (i, 0)),
                  pl.BlockSpec((POOL_HALO, d), lambda i: (jnp.maximum(i * hb - 1, 0), 0)),
                  pl.BlockSpec((None, ng, gw, gw), lambda i: (layer, 0, 0, 0)),
                  pl.BlockSpec((1, d), lambda i: (0, 0)),
                  pl.BlockSpec((1, d), lambda i: (0, 0)),
                  pl.BlockSpec((1, d), lambda i: (0, 0))],
        out_specs=pl.BlockSpec((rc, d), lambda i: (i, 0)),
        out_shape=jax.ShapeDtypeStruct((m, d), F32),
        scratch_shapes=[pltpu.VMEM((POOL_HALO + rc, d), F32),
                        pltpu.VMEM((POOL_HALO + rc, gw), F32),
                        pltpu.VMEM((rc, d), F32)],
        compiler_params=_params(("parallel",)),
        name="pool_ln",
    )(xt, xt, pool_w, pool_scale[layer].reshape(1, d), g.reshape(1, d), b.reshape(1, d))


def _logf_kernel(x_ref, w_ref, b_ref, o_ref):
    o_ref[...] = _log_sigmoid(_dot(x_ref[...].astype(BF16), w_ref[...]) + b_ref[...])


def _logf(xt, w_f, b_f, bm):
    m, d = xt.shape
    h = w_f.shape[1]
    w_pad = jnp.pad(w_f, ((0, 0), (0, V7X_LANES - h))).astype(BF16)
    b_pad = jnp.pad(b_f, (0, V7X_LANES - h)).reshape(1, V7X_LANES)
    return pl.pallas_call(
        _logf_kernel,
        grid=(m // bm,),
        in_specs=[pl.BlockSpec((bm, d), lambda i: (i, 0)),
                  pl.BlockSpec((d, V7X_LANES), lambda i: (0, 0)),
                  pl.BlockSpec((1, V7X_LANES), lambda i: (0, 0))],
        out_specs=pl.BlockSpec((bm, V7X_LANES), lambda i: (i, 0)),
        out_shape=jax.ShapeDtypeStruct((m, V7X_LANES), F32),
        compiler_params=_params(("parallel",)),
        name="fox_logf",
    )(xt, w_pad, b_pad)


def _split3(v):
    hi = v.astype(BF16)
    r1 = v - hi.astype(F32)
    mid = r1.astype(BF16)
    lo = (r1 - mid.astype(F32)).astype(BF16)
    return hi, mid, lo


def _cumsum_kernel(lf_ref, f_ref, ft_ref, carry_ref):
    j = pl.program_id(1)

    @pl.when(j == 0)
    def _():
        carry_ref[...] = jnp.zeros_like(carry_ref)

    n = lf_ref.shape[0]
    row = lax.broadcasted_iota(jnp.int32, (n, n), 0)
    col = lax.broadcasted_iota(jnp.int32, (n, n), 1)
    tri = jnp.where(row >= col, 1.0, 0.0).astype(BF16)
    hi, mid, lo = _split3(lf_ref[...])
    cs = (_dot(tri, hi) + _dot(tri, mid) + _dot(tri, lo)) + carry_ref[0:1, :]
    f_ref[...] = cs
    ft_ref[...] = cs.T
    carry_ref[...] = jnp.broadcast_to(cs[n - 1:n, :], carry_ref.shape)


def _forget_cumsum(logf, nbat, s):
    n = _row_block(s, CUMSUM_ROWS)
    nj = s // n
    return pl.pallas_call(
        _cumsum_kernel,
        grid=(nbat, nj),
        in_specs=[pl.BlockSpec((n, V7X_LANES), lambda bi, j: (bi * nj + j, 0))],
        out_specs=[pl.BlockSpec((n, V7X_LANES), lambda bi, j: (bi * nj + j, 0)),
                   pl.BlockSpec((None, V7X_LANES, n), lambda bi, j: (bi, 0, j))],
        out_shape=[jax.ShapeDtypeStruct((nbat * s, V7X_LANES), F32),
                   jax.ShapeDtypeStruct((nbat, V7X_LANES, s), F32)],
        scratch_shapes=[pltpu.VMEM((V7X_SUBLANES, V7X_LANES), F32)],
        compiler_params=_params(("parallel", "arbitrary")),
        name="fox_cumsum",
    )(logf)


def _fox_attn_kernel(q_ref, k_ref, v_ref, fq_ref, fk_ref, o_ref, m_ref, l_ref, acc_ref, *, tq, dh):
    p = pl.program_id(1)
    qi = pl.program_id(2)
    lane = lax.broadcasted_iota(jnp.int32, (tq, V7X_LANES), 1)
    q = q_ref[...] * jnp.asarray(dh ** -0.5, q_ref.dtype)
    fq_all = fq_ref[...]
    row = lax.broadcasted_iota(jnp.int32, (tq, tq), 0)
    col = lax.broadcasted_iota(jnp.int32, (tq, tq), 1)
    reps = tq // V7X_LANES
    qh, fq = [], []
    for hh in range(2):
        in_head = (lane >= hh * dh) & (lane < (hh + 1) * dh)
        qh.append(jnp.where(in_head, q, jnp.zeros_like(q)))
        fq_col = jnp.sum(jnp.where(lane == 2 * p + hh, fq_all, 0.0), axis=1, keepdims=True)
        fq.append(jnp.broadcast_to(fq_col, (tq, V7X_LANES)))
    m_ref[...] = jnp.full(m_ref.shape, NEG_INF, F32)
    l_ref[...] = jnp.zeros_like(l_ref)
    acc_ref[...] = jnp.zeros_like(acc_ref)

    def kv_step(j, diagonal):
        r0 = pl.multiple_of(j * tq, tq)
        k = k_ref[pl.ds(r0, tq), :]
        v = v_ref[pl.ds(r0, tq), :]
        for hh in range(2):
            s = lax.dot_general(qh[hh], k, (((1,), (1,)), ((), ())), preferred_element_type=F32)
            t = s - fk_ref[hh:hh + 1, pl.ds(r0, tq)]
            if diagonal:
                t = jnp.where(col <= row, t, NEG_INF)
            m_prev = m_ref[hh]
            m_new = jnp.maximum(m_prev, fq[hh] + jnp.max(t, axis=1, keepdims=True))
            alpha = jnp.exp(m_prev - m_new)
            c = fq[hh] - m_new
            pr = jnp.exp(t + jnp.concatenate([c] * reps, axis=1))
            l_ref[hh] = alpha * l_ref[hh] + jnp.sum(pr, axis=1, keepdims=True)
            acc_ref[hh] = alpha * acc_ref[hh] + _dot(pr.astype(BF16), v)
            m_ref[hh] = m_new

    def body(j, carry):
        kv_step(j, False)
        return carry

    lax.fori_loop(0, qi, body, 0)
    kv_step(qi, True)
    o_ref[...] = jnp.where(lane < dh, acc_ref[0] / l_ref[0], acc_ref[1] / l_ref[1]).astype(o_ref.dtype)


def _fox_attention(qkv, f_rows, f_cols, nbat, s, d):
    dh = d // FOX_HEADS
    npair = FOX_HEADS // 2
    assert 2 * dh == V7X_LANES
    tq = _row_block(s, ATTN_TILE)
    nq = s // tq
    return pl.pallas_call(
        functools.partial(_fox_attn_kernel, tq=tq, dh=dh),
        grid=(nbat, npair, nq),
        in_specs=[pl.BlockSpec((tq, V7X_LANES), lambda bi, p, qi: (bi * nq + qi, p)),
                  pl.BlockSpec((s, V7X_LANES), lambda bi, p, qi: (bi, npair + p)),
                  pl.BlockSpec((s, V7X_LANES), lambda bi, p, qi: (bi, 2 * npair + p)),
                  pl.BlockSpec((tq, V7X_LANES), lambda bi, p, qi: (bi * nq + qi, 0)),
                  pl.BlockSpec((None, None, V7X_SUBLANES, s), lambda bi, p, qi: (bi, p, 0, 0))],
        out_specs=pl.BlockSpec((tq, V7X_LANES), lambda bi, p, qi: (bi * nq + qi, p)),
        out_shape=jax.ShapeDtypeStruct((nbat * s, d), BF16),
        scratch_shapes=[pltpu.VMEM((2, tq, V7X_LANES), F32)] * 3,
        compiler_params=_params(("parallel", "parallel", "arbitrary")),
        name="fox_attn",
    )(qkv, qkv, qkv, f_rows, f_cols)


def _fox_layer(xt, nbat, s, layer, w_qkvf, b_f, w_o, g, b):
    m, d = xt.shape
    bm = _row_block(m, ROW_TILE)
    qkv = _matmul(xt, w_qkvf, layer, 3 * d, BF16, bm, d)
    logf = _logf(xt, w_qkvf[layer][:, 3 * d:], b_f[layer], bm)
    f_rows, f_t = _forget_cumsum(logf, nbat, s)
    npair = FOX_HEADS // 2
    f_cols = jnp.pad(f_t[:, :FOX_HEADS, :].reshape(nbat, npair, 2, s), ((0, 0), (0, 0), (0, V7X_SUBLANES - 2), (0, 0)))
    o = _fox_attention(qkv, f_rows, f_cols, nbat, s, d)
    return _mm_ln(o, w_o, layer, xt, g, b, bm)


PACK_E1, PACK_E2, PACK_G1, PACK_G2 = 0, 1, 2, 3


def _to_slabs(slab_ref, value):
    rows = value.shape[0]
    for c in range(V7X_SUBLANES):
        slab_ref[pl.ds(c, rows, stride=V7X_SUBLANES), :] = value[:, c * V7X_LANES:(c + 1) * V7X_LANES]


def _from_slabs(slab_ref, rows):
    return jnp.concatenate([slab_ref[pl.ds(c, rows, stride=V7X_SUBLANES), :] for c in range(V7X_SUBLANES)], axis=1)


def _slab_rows(ref, row):
    return ref.at[pl.ds(pl.multiple_of(row * V7X_SUBLANES, V7X_SUBLANES), V7X_SUBLANES)]


def _wait_many(copy, count):
    assert count % WAIT_UNROLL == 0

    def body(_, c):
        for _ in range(WAIT_UNROLL):
            copy.wait()
        return c

    lax.fori_loop(0, count // WAIT_UNROLL, body, 0)


def _router_kernel(x_ref, w_ref, r_ref, cnt_ref):
    i = pl.program_id(0)

    @pl.when(i == 0)
    def _():
        cnt_ref[...] = jnp.zeros_like(cnt_ref)

    x = x_ref[...]
    xh = x.astype(BF16)
    xl = (x - xh.astype(F32)).astype(BF16)
    w = w_ref[...]
    wh = w.astype(BF16)
    wl = (w - wh.astype(F32)).astype(BF16)
    logits = (_dot(xh, wh) + _dot(xl, wh)) + _dot(xh, wl)
    lane = lax.broadcasted_iota(jnp.int32, logits.shape, 1)
    logits = jnp.where(lane < N_EXPERTS, logits, NEG_INF)
    m1 = jnp.max(logits, axis=1, keepdims=True)
    i1 = jnp.min(jnp.where(logits == m1, lane, V7X_LANES), axis=1, keepdims=True)
    rest = jnp.where(lane == i1, NEG_INF, logits)
    m2 = jnp.max(rest, axis=1, keepdims=True)
    i2 = jnp.min(jnp.where(rest == m2, lane, V7X_LANES), axis=1, keepdims=True)
    e21 = jnp.exp(m2 - m1)
    g1 = 1.0 / (1.0 + e21)
    g2 = e21 * g1
    out = jnp.where(lane == PACK_E1, i1.astype(F32), 0.0)
    out = jnp.where(lane == PACK_E2, i2.astype(F32), out)
    out = jnp.where(lane == PACK_G1, g1, out)
    out = jnp.where(lane == PACK_G2, g2, out)
    r_ref[...] = out
    hit = jnp.where((lane == i1) | (lane == i2), 1.0, 0.0)
    cnt_ref[...] += jnp.broadcast_to(jnp.sum(hit, axis=0, keepdims=True), cnt_ref.shape)


def _router(xt, w_router, bm):
    m, d = xt.shape
    w_pad = jnp.pad(w_router, ((0, 0), (0, V7X_LANES - w_router.shape[1])))
    return pl.pallas_call(
        _router_kernel,
        grid=(m // bm,),
        in_specs=[pl.BlockSpec((bm, d), lambda i: (i, 0)),
                  pl.BlockSpec((d, V7X_LANES), lambda i: (0, 0))],
        out_specs=[pl.BlockSpec((bm, V7X_LANES), lambda i: (i, 0)),
                   pl.BlockSpec((V7X_SUBLANES, V7X_LANES), lambda i: (0, 0))],
        out_shape=[jax.ShapeDtypeStruct((m, V7X_LANES), F32),
                   jax.ShapeDtypeStruct((V7X_SUBLANES, V7X_LANES), F32)],
        compiler_params=_params(("arbitrary",)),
        name="moe_router",
    )(xt, w_pad)


def _dispatch_kernel(cnt_ref, start_ref, x_ref, r_ref, ps_ref, d_ref, rows_hbm, xs_ref, zero_ref, dvm_ref, dsm_ref,
                     carry_ref, sem_rows, sem_idx, *, bm):
    i = pl.program_id(0)
    nsteps = pl.num_programs(0)

    @pl.when(i == 0)
    def _():
        carry_ref[...] = jnp.zeros_like(carry_ref)

    x = x_ref[...]
    r = r_ref[...]
    lane = lax.broadcasted_iota(jnp.int32, r.shape, 1)
    lane_f = lane.astype(F32)
    oh1 = lane_f == r[:, PACK_E1:PACK_E1 + 1]
    oh2 = lane_f == r[:, PACK_E2:PACK_E2 + 1]
    hit = jnp.where(oh1 | oh2, 1.0, 0.0)
    row = lax.broadcasted_iota(jnp.int32, (bm, bm), 0)
    col = lax.broadcasted_iota(jnp.int32, (bm, bm), 1)
    strict = jnp.where(row > col, 1.0, 0.0).astype(BF16)
    before = _dot(strict, hit.astype(BF16)) + carry_ref[0:1, :]
    total = before[bm - 1:bm, :] + hit[bm - 1:bm, :]
    carry_ref[...] = jnp.broadcast_to(total, carry_ref.shape)
    place = before + ps_ref[...]
    d1 = jnp.sum(jnp.where(oh1, place, 0.0), axis=1, keepdims=True)
    d2 = jnp.sum(jnp.where(oh2, place, 0.0), axis=1, keepdims=True)
    dmat = jnp.where(lane == 0, d1, jnp.where(lane == 1, d2, 0.0))
    d_ref[...] = dmat

    dvm_ref[...] = dmat.T[0:V7X_SUBLANES, :].astype(jnp.int32)
    idx_copy = pltpu.make_async_copy(dvm_ref, dsm_ref, sem_idx)
    idx_copy.start()
    idx_copy.wait()

    def row_copy(slot, t, k):
        return pltpu.make_async_copy(_slab_rows(xs_ref.at[slot], t), _slab_rows(rows_hbm, dsm_ref[k, t]), sem_rows.at[slot])

    def wait_rows(slot):
        _wait_many(pltpu.make_async_copy(_slab_rows(xs_ref.at[slot], 0), _slab_rows(rows_hbm, 0), sem_rows.at[slot]),
                   TOP_K * bm)

    for slot in range(2):
        @pl.when(i % 2 == slot)
        def _(slot=slot):
            @pl.when(i >= 2)
            def _():
                wait_rows(slot)

            _to_slabs(xs_ref.at[slot], x)

            def start(t, c):
                for k in range(TOP_K):
                    row_copy(slot, t, k).start(priority=k)
                return c

            lax.fori_loop(0, bm, start, 0)

            @pl.when(i == nsteps - 1)
            def _():
                wait_rows(slot)

            @pl.when((i == nsteps - 1) & (i >= 1))
            def _():
                wait_rows(1 - slot)

    @pl.when(i == nsteps - 1)
    def _():
        zero_ref[...] = jnp.zeros_like(zero_ref)
        for e in range(N_EXPERTS):
            c0 = start_ref[e] + cnt_ref[e]
            c1 = start_ref[e + 1]

            def pad_copy(row):
                return pltpu.make_async_copy(zero_ref, _slab_rows(rows_hbm, row), sem_idx)

            def start_pad(row, c):
                pad_copy(row).start()
                return c

            def wait_pad(row, c):
                pad_copy(row).wait()
                return c

            lax.fori_loop(c0, c1, start_pad, 0)
            lax.fori_loop(c0, c1, wait_pad, 0)


def _dispatch(xt, routed, counts, starts, n_rows, bm):
    m, d = xt.shape
    start_row = jnp.pad(starts[:N_EXPERTS].astype(F32), (0, V7X_LANES - N_EXPERTS)).reshape(1, V7X_LANES)
    grid_spec = pltpu.PrefetchScalarGridSpec(
        num_scalar_prefetch=2,
        grid=(m // bm,),
        in_specs=[pl.BlockSpec((bm, d), lambda i, cnt, st: (i, 0)),
                  pl.BlockSpec((bm, V7X_LANES), lambda i, cnt, st: (i, 0)),
                  pl.BlockSpec((1, V7X_LANES), lambda i, cnt, st: (0, 0))],
        out_specs=[pl.BlockSpec((bm, V7X_LANES), lambda i, cnt, st: (i, 0)),
                   pl.BlockSpec(memory_space=pl.ANY)],
        scratch_shapes=[pltpu.VMEM((2, bm * V7X_SUBLANES, V7X_LANES), F32),
                        pltpu.VMEM((V7X_SUBLANES, V7X_LANES), F32),
                        pltpu.VMEM((V7X_SUBLANES, bm), jnp.int32),
                        pltpu.SMEM((V7X_SUBLANES, bm), jnp.int32),
                        pltpu.VMEM((V7X_SUBLANES, V7X_LANES), F32),
                        pltpu.SemaphoreType.DMA((2,)),
                        pltpu.SemaphoreType.DMA],
    )
    return pl.pallas_call(
        functools.partial(_dispatch_kernel, bm=bm),
        grid_spec=grid_spec,
        out_shape=[jax.ShapeDtypeStruct((m, V7X_LANES), F32),
                   jax.ShapeDtypeStruct((n_rows * V7X_SUBLANES, V7X_LANES), F32)],
        compiler_params=_params(("arbitrary",)),
        name="moe_dispatch",
    )(counts, starts, xt, routed, start_row)


def _experts_kernel(ie_ref, iv_ref, ni_ref, x_ref, wg_ref, wu_ref, wd_ref, o_ref, xb_ref, acc_ref, *, sb, rb):
    i = pl.program_id(0)
    j = pl.program_id(1)
    last = pl.num_programs(1) - 1
    valid = iv_ref[i]
    used = i < ni_ref[0]

    @pl.when(jnp.logical_not(used) & (j == last))
    def _():
        o_ref[...] = jnp.zeros_like(o_ref)

    @pl.when(used)
    def _():
        @pl.when(j == 0)
        def _():
            xb_ref[...] = _from_slabs(x_ref, sb).astype(BF16)
            acc_ref[...] = jnp.zeros_like(acc_ref)

        def run(nrows):
            xb = xb_ref[0:nrows, :]
            h = _silu_mul(_dot(xb, wg_ref[...].astype(BF16)), _dot(xb, wu_ref[...].astype(BF16))).astype(BF16)
            acc_ref[0:nrows, :] += _dot(h, wd_ref[...].astype(BF16))

        @pl.when(valid > rb)
        def _():
            run(sb)

        @pl.when(valid <= rb)
        def _():
            run(rb)

        @pl.when(j == last)
        def _():
            _to_slabs(o_ref, acc_ref[...])


def _experts(rows, w_gu, w_down, layer, item_expert, item_valid, n_items, sb, rb, fc):
    assert sb == 2 * rb
    d = w_down.shape[3]
    f = w_down.shape[2]
    nf = f // fc
    n_max = item_expert.shape[0]
    ssb = sb * V7X_SUBLANES

    def item(i, ni):
        return jnp.minimum(i, ni[0] - 1)

    def chunk(i, j, ni):
        return jnp.where(i < ni[0], j, nf - 1)

    grid_spec = pltpu.PrefetchScalarGridSpec(
        num_scalar_prefetch=3,
        grid=(n_max, nf),
        in_specs=[pl.BlockSpec((ssb, V7X_LANES), lambda i, j, ie, iv, ni: (item(i, ni), 0)),
                  pl.BlockSpec((None, None, d, fc), lambda i, j, ie, iv, ni: (layer, ie[item(i, ni)], 0, chunk(i, j, ni))),
                  pl.BlockSpec((None, None, d, fc), lambda i, j, ie, iv, ni: (layer, ie[item(i, ni)], 0, nf + chunk(i, j, ni))),
                  pl.BlockSpec((None, None, fc, d), lambda i, j, ie, iv, ni: (layer, ie[item(i, ni)], chunk(i, j, ni), 0))],
        out_specs=pl.BlockSpec((ssb, V7X_LANES), lambda i, j, ie, iv, ni: (i, 0)),
        scratch_shapes=[pltpu.VMEM((sb, d), BF16), pltpu.VMEM((sb, d), F32)],
    )
    return pl.pallas_call(
        functools.partial(_experts_kernel, sb=sb, rb=rb),
        grid_spec=grid_spec,
        out_shape=jax.ShapeDtypeStruct(rows.shape, F32),
        compiler_params=_params(("arbitrary", "arbitrary")),
        name="moe_experts",
    )(item_expert, item_valid, n_items, rows, w_gu, w_gu, w_down)


def _combine_ln_kernel(dcur_ref, dnext_ref, x_ref, r_ref, g_ref, b_ref, y_hbm, o_ref, ybuf, sem, *, tb):
    i = pl.program_id(0)
    nsteps = pl.num_programs(0)

    def issue(dref, slot):
        def start(t, c):
            for k in range(TOP_K):
                pltpu.make_async_copy(_slab_rows(y_hbm, dref[0, 0, TOP_K * t + k]), _slab_rows(ybuf.at[slot, k], t),
                                      sem.at[slot]).start(priority=k)
            return c

        lax.fori_loop(0, tb, start, 0)

    for slot in range(2):
        @pl.when(i % 2 == slot)
        def _(slot=slot):
            if slot == 0:
                @pl.when(i == 0)
                def _():
                    issue(dcur_ref, 0)

            @pl.when(i + 1 < nsteps)
            def _():
                issue(dnext_ref, 1 - slot)

            _wait_many(pltpu.make_async_copy(_slab_rows(y_hbm, 0), _slab_rows(ybuf.at[slot, 0], 0), sem.at[slot]),
                       TOP_K * tb)
            r = r_ref[...]
            y = (r[:, PACK_G1:PACK_G1 + 1] * _from_slabs(ybuf.at[slot, 0], tb)
                 + r[:, PACK_G2:PACK_G2 + 1] * _from_slabs(ybuf.at[slot, 1], tb))
            o_ref[...] = _layer_norm(ALPHA * x_ref[...] + y, g_ref[...], b_ref[...])


def _combine_ln(xt, routed, dest, yrows, g, b, tb):
    m, d = xt.shape
    nsteps = m // tb
    dest3 = dest.reshape(nsteps, 1, TOP_K * tb)
    return pl.pallas_call(
        functools.partial(_combine_ln_kernel, tb=tb),
        grid=(nsteps,),
        in_specs=[pl.BlockSpec((1, 1, TOP_K * tb), lambda i: (i, 0, 0), memory_space=pltpu.SMEM),
                  pl.BlockSpec((1, 1, TOP_K * tb), lambda i: (jnp.minimum(i + 1, nsteps - 1), 0, 0), memory_space=pltpu.SMEM),
                  pl.BlockSpec((tb, d), lambda i: (i, 0)),
                  pl.BlockSpec((tb, V7X_LANES), lambda i: (i, 0)),
                  pl.BlockSpec((1, d), lambda i: (0, 0)),
                  pl.BlockSpec((1, d), lambda i: (0, 0)),
                  pl.BlockSpec(memory_space=pl.ANY)],
        out_specs=pl.BlockSpec((tb, d), lambda i: (i, 0)),
        out_shape=jax.ShapeDtypeStruct((m, d), F32),
        scratch_shapes=[pltpu.VMEM((2, TOP_K, tb * V7X_SUBLANES, V7X_LANES), F32), pltpu.SemaphoreType.DMA((2,))],
        compiler_params=_params(("arbitrary",)),
        name="moe_combine_ln",
    )(dest3, dest3, xt, routed, g.reshape(1, d), b.reshape(1, d), yrows)


def _moe_layer(xt, layer, w_router, w_gu, w_down, g, b):
    m, d = xt.shape
    sb = _row_block(m, MOE_SUPER_ROWS)
    rb = _row_block(sb, MOE_ROW_BLOCK)
    bm = _row_block(m, ROUTE_ROWS)
    routed, cnt = _router(xt, w_router[layer], bm)
    counts = cnt[0, :N_EXPERTS].astype(jnp.int32)
    nsb = (counts + sb - 1) // sb
    iend = jnp.cumsum(nsb)
    n_max = (m * TOP_K) // sb + N_EXPERTS
    starts = jnp.concatenate([(iend - nsb) * sb, jnp.full((1,), n_max * sb, jnp.int32)]).astype(jnp.int32)
    item = jnp.arange(n_max, dtype=jnp.int32)
    item_expert = jnp.minimum(jnp.searchsorted(iend, item, side="right"), N_EXPERTS - 1).astype(jnp.int32)
    item_valid = jnp.clip(starts[item_expert] + counts[item_expert] - item * sb, 0, sb).astype(jnp.int32)
    n_items = iend[-1:].astype(jnp.int32)
    dest_f, rows = _dispatch(xt, routed, counts, starts, n_max * sb, bm)
    yrows = _experts(rows, w_gu, w_down, layer, item_expert, item_valid, n_items, sb, rb, MOE_FF_CHUNK)
    dest = dest_f[:, :TOP_K].astype(jnp.int32)
    return _combine_ln(xt, routed, dest, yrows, g, b, _row_block(m, COMBINE_ROWS))


def kernel(x, lru_w_in, lru_conv_w, lru_conv_b, lru_w_gates, lru_b_gates, lru_lambda, lru_w_out, pool_w, pool_scale,
           fox_w_qkvf, fox_b_f, fox_w_o, ffn_w_gu, ffn_w_down, moe_router, moe_w_gu, moe_w_down,
           ln_mix_g, ln_mix_b, ln_ffn_g, ln_ffn_b):
    nbat, s, d = x.shape
    xt = x.reshape(nbat * s, d)
    for i in range(DEPTH):
        mixer, j = i % 3, i // 3
        if mixer == 0:
            xt = _rglru_layer(xt, nbat, s, j, lru_w_in, lru_conv_w, lru_conv_b, lru_w_gates, lru_b_gates,
                              lru_lambda, lru_w_out, ln_mix_g[i], ln_mix_b[i])
        elif mixer == 1:
            xt = _pool_layer(xt, nbat, s, j, pool_w, pool_scale, ln_mix_g[i], ln_mix_b[i])
        else:
            xt = _fox_layer(xt, nbat, s, j, fox_w_qkvf, fox_b_f, fox_w_o, ln_mix_g[i], ln_mix_b[i])
        if i % 2 == 0:
            xt = _ffn_ln(xt, ffn_w_gu, ffn_w_down, i // 2, ln_ffn_g[i], ln_ffn_b[i], _row_block(nbat * s, ROW_TILE), FFN_CHUNK)
        else:
            xt = _moe_layer(xt, i // 2, moe_router, moe_w_gu, moe_w_down, ln_ffn_g[i], ln_ffn_b[i])
    return xt.reshape(nbat, s, d)
```

```python
import functools

import jax
import jax.numpy as jnp
from jax import lax
from jax.experimental import pallas as pl
from jax.experimental.pallas import tpu as pltpu

F32 = jnp.float32
BF16 = jnp.bfloat16

DEPTH = 4
LRU_BLOCKS = 8
CONV_WIDTH = 4
LRU_C = 8.0
POOL_WINDOWS = (2, 4, 8, 16)
FOX_HEADS = 16
N_EXPERTS = 8
TOP_K = 2
LN_EPS = 1e-5
NEG_INF = -1e30
ALPHA = (2 * DEPTH) ** 0.25

V7X_LANES = 128
V7X_SUBLANES = 8
V7X_VMEM_LIMIT_BYTES = 58 * 1024 * 1024

ROW_TILE = 1024
FFN_CHUNK = 512
RGLRU_TIME_TILE = 64
POOL_ROWS = 512
POOL_HALO = 32
ATTN_TILE = 512
ATTN_HEADS_PER_STEP = 2
CUMSUM_ROWS = 256
ROUTE_ROWS = 512
MOE_SUPER_ROWS = 1024
MOE_ROW_BLOCK = 512
MOE_FF_CHUNK = 896


def _params(semantics, vmem=V7X_VMEM_LIMIT_BYTES):
    return pltpu.CompilerParams(dimension_semantics=semantics, vmem_limit_bytes=vmem)


def _row_block(m, want):
    return want if m % want == 0 else m


def _layer_norm(y, g, b):
    mu = jnp.mean(y, axis=-1, keepdims=True)
    yc = y - mu
    var = jnp.mean(yc * yc, axis=-1, keepdims=True)
    return yc * lax.rsqrt(var + LN_EPS) * g + b


def _dot(a, b):
    return jnp.dot(a, b, preferred_element_type=F32)


def _softplus(x):
    return jnp.maximum(x, 0.0) + jnp.log1p(jnp.exp(-jnp.abs(x)))


def _log_sigmoid(z):
    return -_softplus(-z)


def _gelu_tanh(x):
    return 0.5 * x * (1.0 + jnp.tanh(0.7978845608028654 * (x + 0.044715 * (x * x * x))))


def _silu_mul(g, u):
    return g * jax.nn.sigmoid(g) * u


def _matmul_kernel(x_ref, w_ref, o_ref):
    o_ref[...] = _dot(x_ref[...].astype(BF16), w_ref[...].astype(BF16)).astype(o_ref.dtype)


def _matmul(x, w_stack, layer, n_out, out_dtype, bm, bn):
    m, k = x.shape
    return pl.pallas_call(
        _matmul_kernel,
        grid=(m // bm, n_out // bn),
        in_specs=[pl.BlockSpec((bm, k), lambda i, j: (i, 0)),
                  pl.BlockSpec((None, k, bn), lambda i, j: (layer, 0, j))],
        out_specs=pl.BlockSpec((bm, bn), lambda i, j: (i, j)),
        out_shape=jax.ShapeDtypeStruct((m, n_out), out_dtype),
        compiler_params=_params(("parallel", "arbitrary")),
        name="matmul",
    )(x, w_stack)


def _mm_ln_kernel(a_ref, w_ref, x_ref, g_ref, b_ref, o_ref):
    y = _dot(a_ref[...].astype(BF16), w_ref[...].astype(BF16))
    o_ref[...] = _layer_norm(ALPHA * x_ref[...] + y, g_ref[...], b_ref[...])


def _mm_ln(a, w_stack, layer, x, g, b, bm):
    m, k = a.shape
    d = w_stack.shape[2]
    return pl.pallas_call(
        _mm_ln_kernel,
        grid=(m // bm,),
        in_specs=[pl.BlockSpec((bm, k), lambda i: (i, 0)),
                  pl.BlockSpec((None, k, d), lambda i: (layer, 0, 0)),
                  pl.BlockSpec((bm, d), lambda i: (i, 0)),
                  pl.BlockSpec((1, d), lambda i: (0, 0)),
                  pl.BlockSpec((1, d), lambda i: (0, 0))],
        out_specs=pl.BlockSpec((bm, d), lambda i: (i, 0)),
        out_shape=jax.ShapeDtypeStruct((m, d), F32),
        compiler_params=_params(("parallel",)),
        name="mm_ln",
    )(a, w_stack, x, g.reshape(1, d), b.reshape(1, d))


def _ffn_ln_kernel(x_ref, wg_ref, wu_ref, wd_ref, g_ref, b_ref, o_ref, xb_ref, acc_ref):
    j = pl.program_id(1)

    @pl.when(j == 0)
    def _():
        xb_ref[...] = x_ref[...].astype(BF16)
        acc_ref[...] = jnp.zeros_like(acc_ref)

    xb = xb_ref[...]
    h = _silu_mul(_dot(xb, wg_ref[...].astype(BF16)), _dot(xb, wu_ref[...].astype(BF16))).astype(BF16)
    acc_ref[...] += _dot(h, wd_ref[...].astype(BF16))

    @pl.when(j == pl.num_programs(1) - 1)
    def _():
        o_ref[...] = _layer_norm(ALPHA * x_ref[...] + acc_ref[...], g_ref[...], b_ref[...])


def _ffn_ln(x, w_gu_stack, w_down_stack, layer, g, b, bm, fc):
    m, d = x.shape
    f = w_down_stack.shape[1]
    nf = f // fc
    return pl.pallas_call(
        _ffn_ln_kernel,
        grid=(m // bm, nf),
        in_specs=[pl.BlockSpec((bm, d), lambda i, j: (i, 0)),
                  pl.BlockSpec((None, d, fc), lambda i, j: (layer, 0, j)),
                  pl.BlockSpec((None, d, fc), lambda i, j: (layer, 0, nf + j)),
                  pl.BlockSpec((None, fc, d), lambda i, j: (layer, j, 0)),
                  pl.BlockSpec((1, d), lambda i, j: (0, 0)),
                  pl.BlockSpec((1, d), lambda i, j: (0, 0))],
        out_specs=pl.BlockSpec((bm, d), lambda i, j: (i, 0)),
        out_shape=jax.ShapeDtypeStruct((m, d), F32),
        scratch_shapes=[pltpu.VMEM((bm, d), BF16), pltpu.VMEM((bm, d), F32)],
        compiler_params=_params(("parallel", "arbitrary")),
        name="ffn_ln",
    )(x, w_gu_stack, w_gu_stack, w_down_stack, g.reshape(1, d), b.reshape(1, d))


def _rglru_kernel(x_ref, win_ref, cw_ref, cb_ref, wg_ref, bg_ref, lam_ref, wout_ref, g_ref, bt_ref, o_ref,
                  xs_ref, tm_ref, a_ref, b_ref, hs_ref, h_ref, *, ts):
    nb = LRU_BLOCKS
    halo = (CONV_WIDTH - 1) * V7X_SUBLANES
    rows = ts * V7X_SUBLANES
    step = pl.program_id(0)

    @pl.when(step == 0)
    def _():
        h_ref[...] = jnp.zeros_like(h_ref)
        tm_ref[:, 0:halo, :] = jnp.zeros((2 * nb, halo, V7X_LANES), F32)

    @pl.when(step > 0)
    def _():
        tm_ref[:, 0:halo, :] = tm_ref[:, rows:rows + halo, :]

    for bi in range(V7X_SUBLANES):
        for c in range(nb):
            xs_ref[c, pl.ds(bi, ts, stride=V7X_SUBLANES), :] = x_ref[bi, :, c * V7X_LANES:(c + 1) * V7X_LANES]
    x_tm = jnp.concatenate([xs_ref[c] for c in range(nb)], axis=1)
    xb = x_tm.astype(BF16)
    for c2 in range(nb):
        u2 = _dot(xb, win_ref[:, 2 * c2 * V7X_LANES:(2 * c2 + 2) * V7X_LANES])
        tm_ref[2 * c2, halo:halo + rows, :] = u2[:, :V7X_LANES]
        tm_ref[2 * c2 + 1, halo:halo + rows, :] = u2[:, V7X_LANES:]

    for n in range(nb):
        sl = slice(n * V7X_LANES, (n + 1) * V7X_LANES)
        xr = cb_ref[:, sl]
        for k in range(CONV_WIDTH):
            xr = xr + tm_ref[nb + n, k * V7X_SUBLANES:k * V7X_SUBLANES + rows, :] * cw_ref[k:k + 1, sl]
        z = _dot(xr.astype(BF16), wg_ref[n].astype(BF16)) + bg_ref[n]
        r = jax.nn.sigmoid(z[:, :V7X_LANES])
        i = jax.nn.sigmoid(z[:, V7X_LANES:])
        log_a = (-LRU_C * r) * _softplus(-lam_ref[:, sl])
        a = jnp.exp(log_a)
        a_ref[n] = a
        b_ref[n] = jnp.sqrt(-jnp.tanh(log_a) * (a * a + 1.0)) * (i * xr)

    def scan_step(t, hs):
        r0 = pl.multiple_of(t * V7X_SUBLANES, V7X_SUBLANES)
        new = []
        for n in range(nb):
            h = a_ref[n, pl.ds(r0, V7X_SUBLANES), :] * hs[n] + b_ref[n, pl.ds(r0, V7X_SUBLANES), :]
            hs_ref[n, pl.ds(r0, V7X_SUBLANES), :] = h
            new.append(h)
        return tuple(new)

    hs = lax.fori_loop(0, ts, scan_step, tuple(h_ref[n] for n in range(nb)), unroll=8)
    for n in range(nb):
        h_ref[n] = hs[n]

    for n in range(nb):
        a_ref[n] = _gelu_tanh(tm_ref[n, halo:halo + rows, :]) * hs_ref[n]
    gh = jnp.concatenate([a_ref[n] for n in range(nb)], axis=1).astype(BF16)
    y = _layer_norm(ALPHA * x_tm + _dot(gh, wout_ref[...]), g_ref[...], bt_ref[...])
    for c in range(nb):
        b_ref[c] = y[:, c * V7X_LANES:(c + 1) * V7X_LANES]
    for bi in range(V7X_SUBLANES):
        for c in range(nb):
            o_ref[bi, :, c * V7X_LANES:(c + 1) * V7X_LANES] = b_ref[c, pl.ds(bi, ts, stride=V7X_SUBLANES), :]


def _rglru_layer(xt, nbat, s, layer, w_in, conv_w, conv_b, w_gates, b_gates, lam, w_out, g, b):
    m, d = xt.shape
    ts = _row_block(s, RGLRU_TIME_TILE)
    assert nbat == V7X_SUBLANES and d == LRU_BLOCKS * V7X_LANES
    rows = ts * V7X_SUBLANES
    halo = (CONV_WIDTH - 1) * V7X_SUBLANES
    gw = 2 * V7X_LANES
    slab = pltpu.VMEM((LRU_BLOCKS, rows, V7X_LANES), F32)
    out = pl.pallas_call(
        functools.partial(_rglru_kernel, ts=ts),
        grid=(s // ts,),
        in_specs=[pl.BlockSpec((nbat, ts, d), lambda t: (0, t, 0)),
                  pl.BlockSpec((d, 2 * d), lambda t: (0, 0)),
                  pl.BlockSpec((CONV_WIDTH, d), lambda t: (0, 0)),
                  pl.BlockSpec((1, d), lambda t: (0, 0)),
                  pl.BlockSpec((None, LRU_BLOCKS, V7X_LANES, gw), lambda t: (layer, 0, 0, 0)),
                  pl.BlockSpec((LRU_BLOCKS, 1, gw), lambda t: (0, 0, 0)),
                  pl.BlockSpec((1, d), lambda t: (0, 0)),
                  pl.BlockSpec((d, d), lambda t: (0, 0)),
                  pl.BlockSpec((1, d), lambda t: (0, 0)),
                  pl.BlockSpec((1, d), lambda t: (0, 0))],
        out_specs=pl.BlockSpec((nbat, ts, d), lambda t: (0, t, 0)),
        out_shape=jax.ShapeDtypeStruct((nbat, s, d), F32),
        scratch_shapes=[slab,
                        pltpu.VMEM((2 * LRU_BLOCKS, halo + rows, V7X_LANES), F32),
                        slab, slab, slab,
                        pltpu.VMEM((LRU_BLOCKS, V7X_SUBLANES, V7X_LANES), F32)],
        compiler_params=_params(("arbitrary",)),
        name="rglru",
    )(xt.reshape(nbat, s, d), w_in[layer].astype(BF16), conv_w[layer], conv_b[layer].reshape(1, d), w_gates,
      b_gates[layer].reshape(LRU_BLOCKS, 1, gw), lam[layer].reshape(1, d), w_out[layer].astype(BF16),
      g.reshape(1, d), b.reshape(1, d))
    return out.reshape(m, d)


def _pool_ln_kernel(x_ref, prev_ref, w_ref, sc_ref, g_ref, b_ref, o_ref, e_ref, st_ref, y_ref, *, rc, tiles_per_seq):
    i = pl.program_id(0)
    first = (i % tiles_per_seq) == 0
    d = x_ref.shape[1]
    gw = d // len(POOL_WINDOWS)
    e_ref[0:POOL_HALO, :] = jnp.where(first, 0.0, prev_ref[...])
    e_ref[POOL_HALO:, :] = x_ref[...]
    t = ((i % tiles_per_seq) * rc + lax.broadcasted_iota(jnp.int32, (rc, 1), 0) + 1).astype(F32)
    for gi, wl in enumerate(POOL_WINDOWS):
        sl = slice(gi * gw, (gi + 1) * gw)
        lo = V7X_SUBLANES
        shift = 1
        cur = e_ref[lo:, sl] + e_ref[lo - shift:POOL_HALO + rc - shift, sl]
        while 2 * shift < wl:
            shift *= 2
            st_ref[lo:, :] = cur
            nlo = lo + V7X_SUBLANES
            cur = st_ref[nlo:, :] + st_ref[nlo - shift:POOL_HALO + rc - shift, :]
            lo = nlo
        win = cur[POOL_HALO - lo:, :]
        xg = x_ref[:, sl]
        p = win / jnp.minimum(t, float(wl)) - xg
        y_ref[:, sl] = _dot(p.astype(BF16), w_ref[gi].astype(BF16))
    y = y_ref[...] * sc_ref[...]
    o_ref[...] = _layer_norm(ALPHA * x_ref[...] + y, g_ref[...], b_ref[...])


def _pool_layer(xt, nbat, s, layer, pool_w, pool_scale, g, b):
    m, d = xt.shape
    rc = _row_block(s, POOL_ROWS)
    tiles_per_seq = s // rc
    ng = len(POOL_WINDOWS)
    gw = d // ng
    hb = rc // POOL_HALO
    return pl.pallas_call(
        functools.partial(_pool_ln_kernel, rc=rc, tiles_per_seq=tiles_per_seq),
        grid=(m // rc,),
        in_specs=[pl.BlockSpec((rc, d), lambda i: (i, 0)),
                  pl.BlockSpec((POOL_HALO, d), lambda i: (jnp.maximum(i * hb - 1, 0), 0)),
                  pl.BlockSpec((None, ng, gw, gw), lambda i: (layer, 0, 0, 0)),
                  pl.BlockSpec((1, d), lambda i: (0, 0)),
                  pl.BlockSpec((1, d), lambda i: (0, 0)),
                  pl.BlockSpec((1, d), lambda i: (0, 0))],
        out_specs=pl.BlockSpec((rc, d), lambda i: (i, 0)),
        out_shape=jax.ShapeDtypeStruct((m, d), F32),
        scratch_shapes=[pltpu.VMEM((POOL_HALO + rc, d), F32),
                        pltpu.VMEM((POOL_HALO + rc, gw), F32),
                        pltpu.VMEM((rc, d), F32)],
        compiler_params=_params(("parallel",)),
        name="pool_ln",
    )(xt, xt, pool_w, pool_scale[layer].reshape(1, d), g.reshape(1, d), b.reshape(1, d))


def _logf_kernel(x_ref, w_ref, b_ref, o_ref):
    o_ref[...] = _log_sigmoid(_dot(x_ref[...].astype(BF16), w_ref[...]) + b_ref[...])


def _logf(xt, w_f, b_f, bm):
    m, d = xt.shape
    h = w_f.shape[1]
    w_pad = jnp.pad(w_f, ((0, 0), (0, V7X_LANES - h))).astype(BF16)
    b_pad = jnp.pad(b_f, (0, V7X_LANES - h)).reshape(1, V7X_LANES)
    return pl.pallas_call(
        _logf_kernel,
        grid=(m // bm,),
        in_specs=[pl.BlockSpec((bm, d), lambda i: (i, 0)),
                  pl.BlockSpec((d, V7X_LANES), lambda i: (0, 0)),
                  pl.BlockSpec((1, V7X_LANES), lambda i: (0, 0))],
        out_specs=pl.BlockSpec((bm, V7X_LANES), lambda i: (i, 0)),
        out_shape=jax.ShapeDtypeStruct((m, V7X_LANES), F32),
        compiler_params=_params(("parallel",)),
        name="fox_logf",
    )(xt, w_pad, b_pad)


def _split3(v):
    hi = v.astype(BF16)
    r1 = v - hi.astype(F32)
    mid = r1.astype(BF16)
    lo = (r1 - mid.astype(F32)).astype(BF16)
    return hi, mid, lo


def _cumsum_kernel(lf_ref, f_ref, ft_ref, carry_ref):
    j = pl.program_id(1)

    @pl.when(j == 0)
    def _():
        carry_ref[...] = jnp.zeros_like(carry_ref)

    n = lf_ref.shape[0]
    row = lax.broadcasted_iota(jnp.int32, (n, n), 0)
    col = lax.broadcasted_iota(jnp.int32, (n, n), 1)
    tri = jnp.where(row >= col, 1.0, 0.0).astype(BF16)
    hi, mid, lo = _split3(lf_ref[...])
    cs = (_dot(tri, hi) + _dot(tri, mid) + _dot(tri, lo)) + carry_ref[0:1, :]
    f_ref[...] = cs
    ft_ref[...] = cs.T
    carry_ref[...] = jnp.broadcast_to(cs[n - 1:n, :], carry_ref.shape)


def _forget_cumsum(logf, nbat, s):
    n = _row_block(s, CUMSUM_ROWS)
    nj = s // n
    return pl.pallas_call(
        _cumsum_kernel,
        grid=(nbat, nj),
        in_specs=[pl.BlockSpec((n, V7X_LANES), lambda bi, j: (bi * nj + j, 0))],
        out_specs=[pl.BlockSpec((n, V7X_LANES), lambda bi, j: (bi * nj + j, 0)),
                   pl.BlockSpec((None, V7X_LANES, n), lambda bi, j: (bi, 0, j))],
        out_shape=[jax.ShapeDtypeStruct((nbat * s, V7X_LANES), F32),
                   jax.ShapeDtypeStruct((nbat, V7X_LANES, s), F32)],
        scratch_shapes=[pltpu.VMEM((V7X_SUBLANES, V7X_LANES), F32)],
        compiler_params=_params(("parallel", "arbitrary")),
        name="fox_cumsum",
    )(logf)


def _fox_attn_kernel(q_ref, k_ref, v_ref, fq_ref, fk_ref, o_ref, m_ref, l_ref, acc_ref, *, tq, dh, nh):
    grp = pl.program_id(1)
    qi = pl.program_id(2)
    width = nh * dh
    lane = lax.broadcasted_iota(jnp.int32, (tq, width), 1)
    lane_f = lax.broadcasted_iota(jnp.int32, (tq, V7X_LANES), 1)
    q = q_ref[...] * jnp.asarray(dh ** -0.5, q_ref.dtype)
    fq_all = fq_ref[...]
    row = lax.broadcasted_iota(jnp.int32, (tq, tq), 0)
    col = lax.broadcasted_iota(jnp.int32, (tq, tq), 1)
    reps = tq // V7X_LANES
    qh, fq = [], []
    for hh in range(nh):
        in_head = (lane >= hh * dh) & (lane < (hh + 1) * dh)
        qh.append(jnp.where(in_head, q, jnp.zeros_like(q)))
        fq_col = jnp.sum(jnp.where(lane_f == nh * grp + hh, fq_all, 0.0), axis=1, keepdims=True)
        fq.append(jnp.broadcast_to(fq_col, (tq, V7X_LANES)))
    m_ref[...] = jnp.full(m_ref.shape, NEG_INF, F32)
    l_ref[...] = jnp.zeros_like(l_ref)
    acc_ref[...] = jnp.zeros_like(acc_ref)

    def kv_step(j, diagonal):
        r0 = pl.multiple_of(j * tq, tq)
        k = k_ref[pl.ds(r0, tq), :]
        v = v_ref[pl.ds(r0, tq), :]
        for hh in range(nh):
            s = lax.dot_general(qh[hh], k, (((1,), (1,)), ((), ())), preferred_element_type=F32)
            t = s - fk_ref[hh:hh + 1, pl.ds(r0, tq)]
            if diagonal:
                t = jnp.where(col <= row, t, NEG_INF)
            m_prev = m_ref[hh]
            m_new = jnp.maximum(m_prev, fq[hh] + jnp.max(t, axis=1, keepdims=True))
            alpha = jnp.exp(m_prev - m_new)
            c = fq[hh] - m_new
            pr = jnp.exp(t + jnp.concatenate([c] * reps, axis=1))
            l_ref[hh] = alpha * l_ref[hh] + jnp.sum(pr, axis=1, keepdims=True)
            acc_ref[hh] = jnp.concatenate([alpha] * (width // V7X_LANES), axis=1) * acc_ref[hh] + _dot(pr.astype(BF16), v)
            m_ref[hh] = m_new

    def body(j, carry):
        kv_step(j, False)
        return carry

    lax.fori_loop(0, qi, body, 0)
    kv_step(qi, True)
    out = jnp.zeros((tq, width), F32)
    for hh in range(nh):
        in_head = (lane >= hh * dh) & (lane < (hh + 1) * dh)
        out = jnp.where(in_head, acc_ref[hh] / jnp.concatenate([l_ref[hh]] * (width // V7X_LANES), axis=1), out)
    o_ref[...] = out.astype(o_ref.dtype)


def _fox_attention(qkv, f_rows, f_cols, nbat, s, d, nh):
    dh = d // FOX_HEADS
    ngrp = FOX_HEADS // nh
    width = nh * dh
    assert width % V7X_LANES == 0 and nh <= V7X_SUBLANES
    tq = _row_block(s, ATTN_TILE)
    nq = s // tq
    return pl.pallas_call(
        functools.partial(_fox_attn_kernel, tq=tq, dh=dh, nh=nh),
        grid=(nbat, ngrp, nq),
        in_specs=[pl.BlockSpec((tq, width), lambda bi, p, qi: (bi * nq + qi, p)),
                  pl.BlockSpec((s, width), lambda bi, p, qi: (bi, ngrp + p)),
                  pl.BlockSpec((s, width), lambda bi, p, qi: (bi, 2 * ngrp + p)),
                  pl.BlockSpec((tq, V7X_LANES), lambda bi, p, qi: (bi * nq + qi, 0)),
                  pl.BlockSpec((None, None, V7X_SUBLANES, s), lambda bi, p, qi: (bi, p, 0, 0))],
        out_specs=pl.BlockSpec((tq, width), lambda bi, p, qi: (bi * nq + qi, p)),
        out_shape=jax.ShapeDtypeStruct((nbat * s, d), BF16),
        scratch_shapes=[pltpu.VMEM((nh, tq, V7X_LANES), F32), pltpu.VMEM((nh, tq, V7X_LANES), F32),
                        pltpu.VMEM((nh, tq, width), F32)],
        compiler_params=_params(("parallel", "parallel", "arbitrary")),
        name="fox_attn",
    )(qkv, qkv, qkv, f_rows, f_cols)


def _fox_layer(xt, nbat, s, layer, w_qkvf, b_f, w_o, g, b):
    m, d = xt.shape
    bm = _row_block(m, ROW_TILE)
    qkv = _matmul(xt, w_qkvf, layer, 3 * d, BF16, bm, d)
    logf = _logf(xt, w_qkvf[layer][:, 3 * d:], b_f[layer], bm)
    f_rows, f_t = _forget_cumsum(logf, nbat, s)
    nh = ATTN_HEADS_PER_STEP
    f_cols = jnp.pad(f_t[:, :FOX_HEADS, :].reshape(nbat, FOX_HEADS // nh, nh, s),
                     ((0, 0), (0, 0), (0, V7X_SUBLANES - nh), (0, 0)))
    o = _fox_attention(qkv, f_rows, f_cols, nbat, s, d, nh)
    return _mm_ln(o, w_o, layer, xt, g, b, bm)


PACK_E1, PACK_E2, PACK_G1, PACK_G2 = 0, 1, 2, 3
PLAN_LEN, PLAN_OFF, PLAN_DST = 0, 1, 2


def _to_slabs(slab_ref, value):
    rows = value.shape[0]
    for c in range(V7X_SUBLANES):
        slab_ref[pl.ds(c, rows, stride=V7X_SUBLANES), :] = value[:, c * V7X_LANES:(c + 1) * V7X_LANES]


def _from_slabs(slab_ref, rows):
    return jnp.concatenate([slab_ref[pl.ds(c, rows, stride=V7X_SUBLANES), :] for c in range(V7X_SUBLANES)], axis=1)


def _slab_rows(ref, row):
    return ref.at[pl.ds(pl.multiple_of(row * V7X_SUBLANES, V7X_SUBLANES), V7X_SUBLANES)]


def _run_copies(src_ref, dst_ref, src_row, dst_row, length, sem, max_len, wait):
    bit = max_len
    while bit >= 1:
        @pl.when((length & bit) != 0)
        def _(bit=bit):
            done = length & (-2 * bit)
            n = bit * V7X_SUBLANES
            s0 = pl.multiple_of((src_row + done) * V7X_SUBLANES, V7X_SUBLANES)
            d0 = pl.multiple_of((dst_row + done) * V7X_SUBLANES, V7X_SUBLANES)
            cp = pltpu.make_async_copy(src_ref.at[pl.ds(s0, n)], dst_ref.at[pl.ds(d0, n)], sem)
            if wait:
                cp.wait()
            else:
                cp.start()

        bit //= 2


def _router_kernel(x_ref, w_ref, r_ref, cnt_ref):
    i = pl.program_id(0)

    @pl.when(i == 0)
    def _():
        cnt_ref[...] = jnp.zeros_like(cnt_ref)

    x = x_ref[...]
    xh = x.astype(BF16)
    xl = (x - xh.astype(F32)).astype(BF16)
    w = w_ref[...]
    wh = w.astype(BF16)
    wl = (w - wh.astype(F32)).astype(BF16)
    logits = (_dot(xh, wh) + _dot(xl, wh)) + _dot(xh, wl)
    lane = lax.broadcasted_iota(jnp.int32, logits.shape, 1)
    logits = jnp.where(lane < N_EXPERTS, logits, NEG_INF)
    m1 = jnp.max(logits, axis=1, keepdims=True)
    i1 = jnp.min(jnp.where(logits == m1, lane, V7X_LANES), axis=1, keepdims=True)
    rest = jnp.where(lane == i1, NEG_INF, logits)
    m2 = jnp.max(rest, axis=1, keepdims=True)
    i2 = jnp.min(jnp.where(rest == m2, lane, V7X_LANES), axis=1, keepdims=True)
    e21 = jnp.exp(m2 - m1)
    g1 = 1.0 / (1.0 + e21)
    g2 = e21 * g1
    out = jnp.where(lane == PACK_E1, i1.astype(F32), 0.0)
    out = jnp.where(lane == PACK_E2, i2.astype(F32), out)
    out = jnp.where(lane == PACK_G1, g1, out)
    out = jnp.where(lane == PACK_G2, g2, out)
    r_ref[...] = out
    hit = jnp.where((lane == i1) | (lane == i2), 1.0, 0.0)
    cnt_ref[...] += jnp.broadcast_to(jnp.sum(hit, axis=0, keepdims=True), cnt_ref.shape)


def _router(xt, w_router, bm):
    m, d = xt.shape
    w_pad = jnp.pad(w_router, ((0, 0), (0, V7X_LANES - w_router.shape[1])))
    return pl.pallas_call(
        _router_kernel,
        grid=(m // bm,),
        in_specs=[pl.BlockSpec((bm, d), lambda i: (i, 0)),
                  pl.BlockSpec((d, V7X_LANES), lambda i: (0, 0))],
        out_specs=[pl.BlockSpec((bm, V7X_LANES), lambda i: (i, 0)),
                   pl.BlockSpec((V7X_SUBLANES, V7X_LANES), lambda i: (0, 0))],
        out_shape=[jax.ShapeDtypeStruct((m, V7X_LANES), F32),
                   jax.ShapeDtypeStruct((V7X_SUBLANES, V7X_LANES), F32)],
        compiler_params=_params(("arbitrary",)),
        name="moe_router",
    )(xt, w_pad)


def _dispatch_kernel(cnt_ref, start_ref, x_ref, r_ref, ps_ref, q_ref, gt_ref, t_ref, rows_hbm,
                     xs_ref, srt_ref, zero_ref, qvm_ref, tvm_ref, qsm_ref, tsm_ref, carry_ref, sem_rows, sem_idx, *, bm):
    i = pl.program_id(0)
    nsteps = pl.num_programs(0)

    @pl.when(i == 0)
    def _():
        carry_ref[...] = jnp.zeros_like(carry_ref)

    x = x_ref[...]
    r = r_ref[...]
    lane = lax.broadcasted_iota(jnp.int32, r.shape, 1)
    lane_f = lane.astype(F32)
    oh1 = lane_f == r[:, PACK_E1:PACK_E1 + 1]
    oh2 = lane_f == r[:, PACK_E2:PACK_E2 + 1]
    hit = jnp.where(oh1 | oh2, 1.0, 0.0)
    row = lax.broadcasted_iota(jnp.int32, (bm, bm), 0)
    col = lax.broadcasted_iota(jnp.int32, (bm, bm), 1)
    strict = jnp.where(row > col, 1.0, 0.0).astype(BF16)
    local = _dot(strict, hit.astype(BF16))
    cnt8 = jnp.broadcast_to(local[bm - 1:bm, :] + hit[bm - 1:bm, :], (V7X_SUBLANES, V7X_LANES))
    lane8 = lax.broadcasted_iota(jnp.int32, cnt8.shape, 1)
    incl = cnt8
    shift = 1
    while shift < N_EXPERTS:
        incl = incl + jnp.where(lane8 >= shift, pltpu.roll(incl, shift, 1), 0.0)
        shift *= 2
    off8 = incl - cnt8
    place = local + off8[0:1, :]
    q1 = jnp.sum(jnp.where(oh1, place, 0.0), axis=1, keepdims=True)
    q2 = jnp.sum(jnp.where(oh2, place, 0.0), axis=1, keepdims=True)
    qvm_ref[...] = jnp.where(lane == 0, q1, jnp.where(lane == 1, q2, 0.0)).T[0:V7X_SUBLANES, :].astype(jnp.int32)
    q_ref[...] = qvm_ref[...]
    gt_ref[...] = jnp.where(lane == 0, r[:, PACK_G1:PACK_G1 + 1],
                            jnp.where(lane == 1, r[:, PACK_G2:PACK_G2 + 1], 0.0)).T[0:V7X_SUBLANES, :]
    dst8 = carry_ref[...] + ps_ref[...]
    row8 = lax.broadcasted_iota(jnp.int32, cnt8.shape, 0)
    tvm_ref[...] = jnp.where(row8 == PLAN_LEN, cnt8,
                             jnp.where(row8 == PLAN_OFF, off8, jnp.where(row8 == PLAN_DST, dst8, 0.0))).astype(jnp.int32)
    t_ref[...] = tvm_ref[...]
    carry_ref[...] += cnt8

    def runs(slot, wait):
        for e in range(N_EXPERTS):
            _run_copies(srt_ref.at[slot], rows_hbm, tsm_ref[slot, PLAN_OFF, e], tsm_ref[slot, PLAN_DST, e],
                        tsm_ref[slot, PLAN_LEN, e], sem_rows.at[slot], bm, wait)

    for slot in range(2):
        @pl.when(i % 2 == slot)
        def _(slot=slot):
            @pl.when(i >= 2)
            def _():
                runs(slot, True)

            for src, dst in ((qvm_ref, qsm_ref), (tvm_ref, tsm_ref.at[slot])):
                cp = pltpu.make_async_copy(src, dst, sem_idx)
                cp.start()
                cp.wait()
            _to_slabs(xs_ref, x)

            def place_rows(t, c):
                tile = xs_ref[pl.ds(pl.multiple_of(t * V7X_SUBLANES, V7X_SUBLANES), V7X_SUBLANES), :]
                for k in range(TOP_K):
                    p0 = pl.multiple_of(qsm_ref[k, t] * V7X_SUBLANES, V7X_SUBLANES)
                    srt_ref[slot, pl.ds(p0, V7X_SUBLANES), :] = tile
                return c

            lax.fori_loop(0, bm, place_rows, 0, unroll=8)
            runs(slot, False)

            @pl.when(i == nsteps - 1)
            def _():
                runs(slot, True)

            @pl.when((i == nsteps - 1) & (i >= 1))
            def _():
                runs(1 - slot, True)

    @pl.when(i == nsteps - 1)
    def _():
        zero_ref[...] = jnp.zeros_like(zero_ref)
        for e in range(N_EXPERTS):
            c0 = start_ref[e] + cnt_ref[e]
            c1 = start_ref[e + 1]

            def pad_copy(row):
                return pltpu.make_async_copy(zero_ref, _slab_rows(rows_hbm, row), sem_idx)

            def start_pad(row, c):
                pad_copy(row).start()
                return c

            def wait_pad(row, c):
                pad_copy(row).wait()
                return c

            lax.fori_loop(c0, c1, start_pad, 0)
            lax.fori_loop(c0, c1, wait_pad, 0)


def _dispatch(xt, routed, counts, starts, n_rows, bm):
    m, d = xt.shape
    nsteps = m // bm
    start_row = jnp.pad(starts[:N_EXPERTS].astype(F32), (0, V7X_LANES - N_EXPERTS)).reshape(1, V7X_LANES)
    grid_spec = pltpu.PrefetchScalarGridSpec(
        num_scalar_prefetch=2,
        grid=(nsteps,),
        in_specs=[pl.BlockSpec((bm, d), lambda i, cnt, st: (i, 0)),
                  pl.BlockSpec((bm, V7X_LANES), lambda i, cnt, st: (i, 0)),
                  pl.BlockSpec((1, V7X_LANES), lambda i, cnt, st: (0, 0))],
        out_specs=[pl.BlockSpec((None, V7X_SUBLANES, bm), lambda i, cnt, st: (i, 0, 0)),
                   pl.BlockSpec((None, V7X_SUBLANES, bm), lambda i, cnt, st: (i, 0, 0)),
                   pl.BlockSpec((None, V7X_SUBLANES, V7X_LANES), lambda i, cnt, st: (i, 0, 0)),
                   pl.BlockSpec(memory_space=pl.ANY)],
        scratch_shapes=[pltpu.VMEM((bm * V7X_SUBLANES, V7X_LANES), F32),
                        pltpu.VMEM((2, TOP_K * bm * V7X_SUBLANES, V7X_LANES), F32),
                        pltpu.VMEM((V7X_SUBLANES, V7X_LANES), F32),
                        pltpu.VMEM((V7X_SUBLANES, bm), jnp.int32),
                        pltpu.VMEM((V7X_SUBLANES, V7X_LANES), jnp.int32),
                        pltpu.SMEM((V7X_SUBLANES, bm), jnp.int32),
                        pltpu.SMEM((2, V7X_SUBLANES, V7X_LANES), jnp.int32),
                        pltpu.VMEM((V7X_SUBLANES, V7X_LANES), F32),
                        pltpu.SemaphoreType.DMA((2,)),
                        pltpu.SemaphoreType.DMA],
    )
    return pl.pallas_call(
        functools.partial(_dispatch_kernel, bm=bm),
        grid_spec=grid_spec,
        out_shape=[jax.ShapeDtypeStruct((nsteps, V7X_SUBLANES, bm), jnp.int32),
                   jax.ShapeDtypeStruct((nsteps, V7X_SUBLANES, bm), F32),
                   jax.ShapeDtypeStruct((nsteps, V7X_SUBLANES, V7X_LANES), jnp.int32),
                   jax.ShapeDtypeStruct((n_rows * V7X_SUBLANES, V7X_LANES), F32)],
        compiler_params=_params(("arbitrary",)),
        name="moe_dispatch",
    )(counts, starts, xt, routed, start_row)


def _experts_kernel(ie_ref, iv_ref, ni_ref, x_ref, wg_ref, wu_ref, wd_ref, o_ref, xb_ref, acc_ref, *, sb, rb):
    i = pl.program_id(0)
    j = pl.program_id(1)
    last = pl.num_programs(1) - 1
    valid = iv_ref[i]
    used = i < ni_ref[0]

    @pl.when(jnp.logical_not(used) & (j == last))
    def _():
        o_ref[...] = jnp.zeros_like(o_ref)

    @pl.when(used)
    def _():
        @pl.when(j == 0)
        def _():
            xb_ref[...] = _from_slabs(x_ref, sb).astype(BF16)
            acc_ref[...] = jnp.zeros_like(acc_ref)

        def run(nrows):
            xb = xb_ref[0:nrows, :]
            h = _silu_mul(_dot(xb, wg_ref[...].astype(BF16)), _dot(xb, wu_ref[...].astype(BF16))).astype(BF16)
            acc_ref[0:nrows, :] += _dot(h, wd_ref[...].astype(BF16))

        @pl.when(valid > rb)
        def _():
            run(sb)

        @pl.when(valid <= rb)
        def _():
            run(rb)

        @pl.when(j == last)
        def _():
            _to_slabs(o_ref, acc_ref[...])


def _experts(rows, w_gu, w_down, layer, item_expert, item_valid, n_items, sb, rb, fc):
    assert sb == 2 * rb
    d = w_down.shape[3]
    f = w_down.shape[2]
    nf = f // fc
    n_max = item_expert.shape[0]
    ssb = sb * V7X_SUBLANES

    def item(i, ni):
        return jnp.minimum(i, ni[0] - 1)

    def chunk(i, j, ni):
        return jnp.where(i < ni[0], j, nf - 1)

    grid_spec = pltpu.PrefetchScalarGridSpec(
        num_scalar_prefetch=3,
        grid=(n_max, nf),
        in_specs=[pl.BlockSpec((ssb, V7X_LANES), lambda i, j, ie, iv, ni: (item(i, ni), 0)),
                  pl.BlockSpec((None, None, d, fc), lambda i, j, ie, iv, ni: (layer, ie[item(i, ni)], 0, chunk(i, j, ni))),
                  pl.BlockSpec((None, None, d, fc), lambda i, j, ie, iv, ni: (layer, ie[item(i, ni)], 0, nf + chunk(i, j, ni))),
                  pl.BlockSpec((None, None, fc, d), lambda i, j, ie, iv, ni: (layer, ie[item(i, ni)], chunk(i, j, ni), 0))],
        out_specs=pl.BlockSpec((ssb, V7X_LANES), lambda i, j, ie, iv, ni: (i, 0)),
        scratch_shapes=[pltpu.VMEM((sb, d), BF16), pltpu.VMEM((sb, d), F32)],
    )
    return pl.pallas_call(
        functools.partial(_experts_kernel, sb=sb, rb=rb),
        grid_spec=grid_spec,
        out_shape=jax.ShapeDtypeStruct(rows.shape, F32),
        compiler_params=_params(("arbitrary", "arbitrary")),
        name="moe_experts",
    )(item_expert, item_valid, n_items, rows, w_gu, w_gu, w_down)


def _combine_ln_kernel(q_ref, gt_ref, tcur_ref, tnext_ref, x_ref, g_ref, b_ref, y_hbm, o_ref, ybuf, mix_ref, sem, *, bm):
    i = pl.program_id(0)
    nsteps = pl.num_programs(0)

    def runs(tref, slot, wait):
        for e in range(N_EXPERTS):
            _run_copies(y_hbm, ybuf.at[slot], tref[PLAN_DST, e], tref[PLAN_OFF, e], tref[PLAN_LEN, e],
                        sem.at[slot], bm, wait)

    for slot in range(2):
        @pl.when(i % 2 == slot)
        def _(slot=slot):
            if slot == 0:
                @pl.when(i == 0)
                def _():
                    runs(tcur_ref, 0, False)

            @pl.when(i + 1 < nsteps)
            def _():
                runs(tnext_ref, 1 - slot, False)

            runs(tcur_ref, slot, True)

            def mix(t, c):
                p1 = pl.multiple_of(q_ref[0, t] * V7X_SUBLANES, V7X_SUBLANES)
                p2 = pl.multiple_of(q_ref[1, t] * V7X_SUBLANES, V7X_SUBLANES)
                t0 = pl.multiple_of(t * V7X_SUBLANES, V7X_SUBLANES)
                mix_ref[pl.ds(t0, V7X_SUBLANES), :] = (gt_ref[0, t] * ybuf[slot, pl.ds(p1, V7X_SUBLANES), :]
                                                       + gt_ref[1, t] * ybuf[slot, pl.ds(p2, V7X_SUBLANES), :])
                return c

            lax.fori_loop(0, bm, mix, 0, unroll=8)
            o_ref[...] = _layer_norm(ALPHA * x_ref[...] + _from_slabs(mix_ref, bm), g_ref[...], b_ref[...])


def _combine_ln(xt, plan_q, plan_g, plan_t, yrows, g, b, bm):
    m, d = xt.shape
    nsteps = m // bm
    smem = pltpu.SMEM
    return pl.pallas_call(
        functools.partial(_combine_ln_kernel, bm=bm),
        grid=(nsteps,),
        in_specs=[pl.BlockSpec((None, V7X_SUBLANES, bm), lambda i: (i, 0, 0), memory_space=smem),
                  pl.BlockSpec((None, V7X_SUBLANES, bm), lambda i: (i, 0, 0), memory_space=smem),
                  pl.BlockSpec((None, V7X_SUBLANES, V7X_LANES), lambda i: (i, 0, 0), memory_space=smem),
                  pl.BlockSpec((None, V7X_SUBLANES, V7X_LANES), lambda i: (jnp.minimum(i + 1, nsteps - 1), 0, 0),
                               memory_space=smem),
                  pl.BlockSpec((bm, d), lambda i: (i, 0)),
                  pl.BlockSpec((1, d), lambda i: (0, 0)),
                  pl.BlockSpec((1, d), lambda i: (0, 0)),
                  pl.BlockSpec(memory_space=pl.ANY)],
        out_specs=pl.BlockSpec((bm, d), lambda i: (i, 0)),
        out_shape=jax.ShapeDtypeStruct((m, d), F32),
        scratch_shapes=[pltpu.VMEM((2, TOP_K * bm * V7X_SUBLANES, V7X_LANES), F32),
                        pltpu.VMEM((bm * V7X_SUBLANES, V7X_LANES), F32),
                        pltpu.SemaphoreType.DMA((2,))],
        compiler_params=_params(("arbitrary",)),
        name="moe_combine_ln",
    )(plan_q, plan_g, plan_t, plan_t, xt, g.reshape(1, d), b.reshape(1, d), yrows)


def _moe_layer(xt, layer, w_router, w_gu, w_down, g, b):
    m, d = xt.shape
    sb = _row_block(m, MOE_SUPER_ROWS)
    rb = _row_block(sb, MOE_ROW_BLOCK)
    bm = _row_block(m, ROUTE_ROWS)
    routed, cnt = _router(xt, w_router[layer], bm)
    counts = cnt[0, :N_EXPERTS].astype(jnp.int32)
    nsb = (counts + sb - 1) // sb
    iend = jnp.cumsum(nsb)
    n_max = (m * TOP_K) // sb + N_EXPERTS
    starts = jnp.concatenate([(iend - nsb) * sb, jnp.full((1,), n_max * sb, jnp.int32)]).astype(jnp.int32)
    item = jnp.arange(n_max, dtype=jnp.int32)
    item_expert = jnp.minimum(jnp.searchsorted(iend, item, side="right"), N_EXPERTS - 1).astype(jnp.int32)
    item_valid = jnp.clip(starts[item_expert] + counts[item_expert] - item * sb, 0, sb).astype(jnp.int32)
    n_items = iend[-1:].astype(jnp.int32)
    plan_q, plan_g, plan_t, rows = _dispatch(xt, routed, counts, starts, n_max * sb, bm)
    yrows = _experts(rows, w_gu, w_down, layer, item_expert, item_valid, n_items, sb, rb, MOE_FF_CHUNK)
    return _combine_ln(xt, plan_q, plan_g, plan_t, yrows, g, b, bm)


def kernel(x, lru_w_in, lru_conv_w, lru_conv_b, lru_w_gates, lru_b_gates, lru_lambda, lru_w_out, pool_w, pool_scale,
           fox_w_qkvf, fox_b_f, fox_w_o, ffn_w_gu, ffn_w_down, moe_router, moe_w_gu, moe_w_down,
           ln_mix_g, ln_mix_b, ln_ffn_g, ln_ffn_b):
    nbat, s, d = x.shape
    xt = x.reshape(nbat * s, d)
    for i in range(DEPTH):
        mixer, j = i % 3, i // 3
        if mixer == 0:
            xt = _rglru_layer(xt, nbat, s, j, lru_w_in, lru_conv_w, lru_conv_b, lru_w_gates, lru_b_gates,
                              lru_lambda, lru_w_out, ln_mix_g[i], ln_mix_b[i])
        elif mixer == 1:
            xt = _pool_layer(xt, nbat, s, j, pool_w, pool_scale, ln_mix_g[i], ln_mix_b[i])
        else:
            xt = _fox_layer(xt, nbat, s, j, fox_w_qkvf, fox_b_f, fox_w_o, ln_mix_g[i], ln_mix_b[i])
        if i % 2 == 0:
            xt = _ffn_ln(xt, ffn_w_gu, ffn_w_down, i // 2, ln_ffn_g[i], ln_ffn_b[i], _row_block(nbat * s, ROW_TILE), FFN_CHUNK)
        else:
            xt = _moe_layer(xt, i // 2, moe_router, moe_w_gu, moe_w_down, ln_ffn_g[i], ln_ffn_b[i])
    return xt.reshape(nbat, s, d)
```

```python
import functools

import jax
import jax.numpy as jnp
from jax import lax
from jax.experimental import pallas as pl
from jax.experimental.pallas import tpu as pltpu

F32 = jnp.float32
BF16 = jnp.bfloat16

DEPTH = 4
LRU_BLOCKS = 8
CONV_WIDTH = 4
LRU_C = 8.0
POOL_WINDOWS = (2, 4, 8, 16)
FOX_HEADS = 16
N_EXPERTS = 8
TOP_K = 2
LN_EPS = 1e-5
NEG_INF = -1e30
ALPHA = (2 * DEPTH) ** 0.25

V7X_LANES = 128
V7X_SUBLANES = 8
V7X_VMEM_LIMIT_BYTES = 58 * 1024 * 1024

ROW_TILE = 1024
FFN_CHUNK = 512
RGLRU_TIME_TILE = 64
POOL_ROWS = 512
POOL_HALO = 32
ATTN_TILE = 512
ATTN_HEADS_PER_STEP = 2
ATTN_STRIP = 64
CUMSUM_ROWS = 256
ROUTE_ROWS = 512
MOE_SUPER_ROWS = 1024
MOE_ROW_BLOCK = 512
MOE_FF_CHUNK = 896


def _params(semantics, vmem=V7X_VMEM_LIMIT_BYTES):
    return pltpu.CompilerParams(dimension_semantics=semantics, vmem_limit_bytes=vmem)


def _row_block(m, want):
    return want if m % want == 0 else m


def _layer_norm(y, g, b):
    mu = jnp.mean(y, axis=-1, keepdims=True)
    yc = y - mu
    var = jnp.mean(yc * yc, axis=-1, keepdims=True)
    return yc * lax.rsqrt(var + LN_EPS) * g + b


def _dot(a, b):
    return jnp.dot(a, b, preferred_element_type=F32)


def _softplus(x):
    return jnp.maximum(x, 0.0) + jnp.log1p(jnp.exp(-jnp.abs(x)))


def _log_sigmoid(z):
    return -_softplus(-z)


def _gelu_tanh(x):
    return 0.5 * x * (1.0 + jnp.tanh(0.7978845608028654 * (x + 0.044715 * (x * x * x))))


def _silu_mul(g, u):
    return g * jax.nn.sigmoid(g) * u


def _matmul_kernel(x_ref, w_ref, o_ref):
    o_ref[...] = _dot(x_ref[...].astype(BF16), w_ref[...].astype(BF16)).astype(o_ref.dtype)


def _matmul(x, w_stack, layer, n_out, out_dtype, bm, bn):
    m, k = x.shape
    return pl.pallas_call(
        _matmul_kernel,
        grid=(m // bm, n_out // bn),
        in_specs=[pl.BlockSpec((bm, k), lambda i, j: (i, 0)),
                  pl.BlockSpec((None, k, bn), lambda i, j: (layer, 0, j))],
        out_specs=pl.BlockSpec((bm, bn), lambda i, j: (i, j)),
        out_shape=jax.ShapeDtypeStruct((m, n_out), out_dtype),
        compiler_params=_params(("parallel", "arbitrary")),
        name="matmul",
    )(x, w_stack)


def _mm_ln_kernel(a_ref, w_ref, x_ref, g_ref, b_ref, o_ref):
    y = _dot(a_ref[...].astype(BF16), w_ref[...].astype(BF16))
    o_ref[...] = _layer_norm(ALPHA * x_ref[...] + y, g_ref[...], b_ref[...])


def _mm_ln(a, w_stack, layer, x, g, b, bm):
    m, k = a.shape
    d = w_stack.shape[2]
    return pl.pallas_call(
        _mm_ln_kernel,
        grid=(m // bm,),
        in_specs=[pl.BlockSpec((bm, k), lambda i: (i, 0)),
                  pl.BlockSpec((None, k, d), lambda i: (layer, 0, 0)),
                  pl.BlockSpec((bm, d), lambda i: (i, 0)),
                  pl.BlockSpec((1, d), lambda i: (0, 0)),
                  pl.BlockSpec((1, d), lambda i: (0, 0))],
        out_specs=pl.BlockSpec((bm, d), lambda i: (i, 0)),
        out_shape=jax.ShapeDtypeStruct((m, d), F32),
        compiler_params=_params(("parallel",)),
        name="mm_ln",
    )(a, w_stack, x, g.reshape(1, d), b.reshape(1, d))


def _ffn_ln_kernel(x_ref, wg_ref, wu_ref, wd_ref, g_ref, b_ref, o_ref, xb_ref, acc_ref):
    j = pl.program_id(1)

    @pl.when(j == 0)
    def _():
        xb_ref[...] = x_ref[...].astype(BF16)
        acc_ref[...] = jnp.zeros_like(acc_ref)

    xb = xb_ref[...]
    h = _silu_mul(_dot(xb, wg_ref[...].astype(BF16)), _dot(xb, wu_ref[...].astype(BF16))).astype(BF16)
    acc_ref[...] += _dot(h, wd_ref[...].astype(BF16))

    @pl.when(j == pl.num_programs(1) - 1)
    def _():
        o_ref[...] = _layer_norm(ALPHA * x_ref[...] + acc_ref[...], g_ref[...], b_ref[...])


def _ffn_ln(x, w_gu_stack, w_down_stack, layer, g, b, bm, fc):
    m, d = x.shape
    f = w_down_stack.shape[1]
    nf = f // fc
    return pl.pallas_call(
        _ffn_ln_kernel,
        grid=(m // bm, nf),
        in_specs=[pl.BlockSpec((bm, d), lambda i, j: (i, 0)),
                  pl.BlockSpec((None, d, fc), lambda i, j: (layer, 0, j)),
                  pl.BlockSpec((None, d, fc), lambda i, j: (layer, 0, nf + j)),
                  pl.BlockSpec((None, fc, d), lambda i, j: (layer, j, 0)),
                  pl.BlockSpec((1, d), lambda i, j: (0, 0)),
                  pl.BlockSpec((1, d), lambda i, j: (0, 0))],
        out_specs=pl.BlockSpec((bm, d), lambda i, j: (i, 0)),
        out_shape=jax.ShapeDtypeStruct((m, d), F32),
        scratch_shapes=[pltpu.VMEM((bm, d), BF16), pltpu.VMEM((bm, d), F32)],
        compiler_params=_params(("parallel", "arbitrary")),
        name="ffn_ln",
    )(x, w_gu_stack, w_gu_stack, w_down_stack, g.reshape(1, d), b.reshape(1, d))


def _rglru_kernel(x_ref, win_ref, cw_ref, cb_ref, wg_ref, bg_ref, lam_ref, wout_ref, g_ref, bt_ref, o_ref,
                  xs_ref, tm_ref, a_ref, b_ref, hs_ref, h_ref, *, ts):
    nb = LRU_BLOCKS
    halo = (CONV_WIDTH - 1) * V7X_SUBLANES
    rows = ts * V7X_SUBLANES
    step = pl.program_id(0)

    @pl.when(step == 0)
    def _():
        h_ref[...] = jnp.zeros_like(h_ref)
        tm_ref[:, 0:halo, :] = jnp.zeros((2 * nb, halo, V7X_LANES), F32)

    @pl.when(step > 0)
    def _():
        tm_ref[:, 0:halo, :] = tm_ref[:, rows:rows + halo, :]

    for bi in range(V7X_SUBLANES):
        for c in range(nb):
            xs_ref[c, pl.ds(bi, ts, stride=V7X_SUBLANES), :] = x_ref[bi, :, c * V7X_LANES:(c + 1) * V7X_LANES]
    x_tm = jnp.concatenate([xs_ref[c] for c in range(nb)], axis=1)
    xb = x_tm.astype(BF16)
    for c2 in range(nb):
        u2 = _dot(xb, win_ref[:, 2 * c2 * V7X_LANES:(2 * c2 + 2) * V7X_LANES])
        tm_ref[2 * c2, halo:halo + rows, :] = u2[:, :V7X_LANES]
        tm_ref[2 * c2 + 1, halo:halo + rows, :] = u2[:, V7X_LANES:]

    for n in range(nb):
        sl = slice(n * V7X_LANES, (n + 1) * V7X_LANES)
        xr = cb_ref[:, sl]
        for k in range(CONV_WIDTH):
            xr = xr + tm_ref[nb + n, k * V7X_SUBLANES:k * V7X_SUBLANES + rows, :] * cw_ref[k:k + 1, sl]
        z = _dot(xr.astype(BF16), wg_ref[n].astype(BF16)) + bg_ref[n]
        r = jax.nn.sigmoid(z[:, :V7X_LANES])
        i = jax.nn.sigmoid(z[:, V7X_LANES:])
        log_a = (-LRU_C * r) * _softplus(-lam_ref[:, sl])
        a = jnp.exp(log_a)
        a_ref[n] = a
        b_ref[n] = jnp.sqrt(-jnp.tanh(log_a) * (a * a + 1.0)) * (i * xr)

    def scan_step(t, hs):
        r0 = pl.multiple_of(t * V7X_SUBLANES, V7X_SUBLANES)
        new = []
        for n in range(nb):
            h = a_ref[n, pl.ds(r0, V7X_SUBLANES), :] * hs[n] + b_ref[n, pl.ds(r0, V7X_SUBLANES), :]
            hs_ref[n, pl.ds(r0, V7X_SUBLANES), :] = h
            new.append(h)
        return tuple(new)

    hs = lax.fori_loop(0, ts, scan_step, tuple(h_ref[n] for n in range(nb)), unroll=8)
    for n in range(nb):
        h_ref[n] = hs[n]

    for n in range(nb):
        a_ref[n] = _gelu_tanh(tm_ref[n, halo:halo + rows, :]) * hs_ref[n]
    gh = jnp.concatenate([a_ref[n] for n in range(nb)], axis=1).astype(BF16)
    y = _layer_norm(ALPHA * x_tm + _dot(gh, wout_ref[...]), g_ref[...], bt_ref[...])
    for c in range(nb):
        b_ref[c] = y[:, c * V7X_LANES:(c + 1) * V7X_LANES]
    for bi in range(V7X_SUBLANES):
        for c in range(nb):
            o_ref[bi, :, c * V7X_LANES:(c + 1) * V7X_LANES] = b_ref[c, pl.ds(bi, ts, stride=V7X_SUBLANES), :]


def _rglru_layer(xt, nbat, s, layer, w_in, conv_w, conv_b, w_gates, b_gates, lam, w_out, g, b):
    m, d = xt.shape
    ts = _row_block(s, RGLRU_TIME_TILE)
    assert nbat == V7X_SUBLANES and d == LRU_BLOCKS * V7X_LANES
    rows = ts * V7X_SUBLANES
    halo = (CONV_WIDTH - 1) * V7X_SUBLANES
    gw = 2 * V7X_LANES
    slab = pltpu.VMEM((LRU_BLOCKS, rows, V7X_LANES), F32)
    out = pl.pallas_call(
        functools.partial(_rglru_kernel, ts=ts),
        grid=(s // ts,),
        in_specs=[pl.BlockSpec((nbat, ts, d), lambda t: (0, t, 0)),
                  pl.BlockSpec((d, 2 * d), lambda t: (0, 0)),
                  pl.BlockSpec((CONV_WIDTH, d), lambda t: (0, 0)),
                  pl.BlockSpec((1, d), lambda t: (0, 0)),
                  pl.BlockSpec((None, LRU_BLOCKS, V7X_LANES, gw), lambda t: (layer, 0, 0, 0)),
                  pl.BlockSpec((LRU_BLOCKS, 1, gw), lambda t: (0, 0, 0)),
                  pl.BlockSpec((1, d), lambda t: (0, 0)),
                  pl.BlockSpec((d, d), lambda t: (0, 0)),
                  pl.BlockSpec((1, d), lambda t: (0, 0)),
                  pl.BlockSpec((1, d), lambda t: (0, 0))],
        out_specs=pl.BlockSpec((nbat, ts, d), lambda t: (0, t, 0)),
        out_shape=jax.ShapeDtypeStruct((nbat, s, d), F32),
        scratch_shapes=[slab,
                        pltpu.VMEM((2 * LRU_BLOCKS, halo + rows, V7X_LANES), F32),
                        slab, slab, slab,
                        pltpu.VMEM((LRU_BLOCKS, V7X_SUBLANES, V7X_LANES), F32)],
        compiler_params=_params(("arbitrary",)),
        name="rglru",
    )(xt.reshape(nbat, s, d), w_in[layer].astype(BF16), conv_w[layer], conv_b[layer].reshape(1, d), w_gates,
      b_gates[layer].reshape(LRU_BLOCKS, 1, gw), lam[layer].reshape(1, d), w_out[layer].astype(BF16),
      g.reshape(1, d), b.reshape(1, d))
    return out.reshape(m, d)


def _pool_ln_kernel(x_ref, prev_ref, w_ref, sc_ref, g_ref, b_ref, o_ref, e_ref, st_ref, y_ref, *, rc, tiles_per_seq):
    i = pl.program_id(0)
    first = (i % tiles_per_seq) == 0
    d = x_ref.shape[1]
    gw = d // len(POOL_WINDOWS)
    e_ref[0:POOL_HALO, :] = jnp.where(first, 0.0, prev_ref[...])
    e_ref[POOL_HALO:, :] = x_ref[...]
    t = ((i % tiles_per_seq) * rc + lax.broadcasted_iota(jnp.int32, (rc, 1), 0) + 1).astype(F32)
    for gi, wl in enumerate(POOL_WINDOWS):
        sl = slice(gi * gw, (gi + 1) * gw)
        lo = V7X_SUBLANES
        shift = 1
        cur = e_ref[lo:, sl] + e_ref[lo - shift:POOL_HALO + rc - shift, sl]
        while 2 * shift < wl:
            shift *= 2
            st_ref[lo:, :] = cur
            nlo = lo + V7X_SUBLANES
            cur = st_ref[nlo:, :] + st_ref[nlo - shift:POOL_HALO + rc - shift, :]
            lo = nlo
        win = cur[POOL_HALO - lo:, :]
        xg = x_ref[:, sl]
        p = win / jnp.minimum(t, float(wl)) - xg
        y_ref[:, sl] = _dot(p.astype(BF16), w_ref[gi].astype(BF16))
    y = y_ref[...] * sc_ref[...]
    o_ref[...] = _layer_norm(ALPHA * x_ref[...] + y, g_ref[...], b_ref[...])


def _pool_layer(xt, nbat, s, layer, pool_w, pool_scale, g, b):
    m, d = xt.shape
    rc = _row_block(s, POOL_ROWS)
    tiles_per_seq = s // rc
    ng = len(POOL_WINDOWS)
    gw = d // ng
    hb = rc // POOL_HALO
    return pl.pallas_call(
        functools.partial(_pool_ln_kernel, rc=rc, tiles_per_seq=tiles_per_seq),
        grid=(m // rc,),
        in_specs=[pl.BlockSpec((rc, d), lambda i: (i, 0)),
                  pl.BlockSpec((POOL_HALO, d), lambda i: (jnp.maximum(i * hb - 1, 0), 0)),
                  pl.BlockSpec((None, ng, gw, gw), lambda i: (layer, 0, 0, 0)),
                  pl.BlockSpec((1, d), lambda i: (0, 0)),
                  pl.BlockSpec((1, d), lambda i: (0, 0)),
                  pl.BlockSpec((1, d), lambda i: (0, 0))],
        out_specs=pl.BlockSpec((rc, d), lambda i: (i, 0)),
        out_shape=jax.ShapeDtypeStruct((m, d), F32),
        scratch_shapes=[pltpu.VMEM((POOL_HALO + rc, d), F32),
                        pltpu.VMEM((POOL_HALO + rc, gw), F32),
                        pltpu.VMEM((rc, d), F32)],
        compiler_params=_params(("parallel",)),
        name="pool_ln",
    )(xt, xt, pool_w, pool_scale[layer].reshape(1, d), g.reshape(1, d), b.reshape(1, d))


def _logf_kernel(x_ref, w_ref, b_ref, o_ref):
    o_ref[...] = _log_sigmoid(_dot(x_ref[...].astype(BF16), w_ref[...]) + b_ref[...])


def _logf(xt, w_f, b_f, bm):
    m, d = xt.shape
    h = w_f.shape[1]
    w_pad = jnp.pad(w_f, ((0, 0), (0, V7X_LANES - h))).astype(BF16)
    b_pad = jnp.pad(b_f, (0, V7X_LANES - h)).reshape(1, V7X_LANES)
    return pl.pallas_call(
        _logf_kernel,
        grid=(m // bm,),
        in_specs=[pl.BlockSpec((bm, d), lambda i: (i, 0)),
                  pl.BlockSpec((d, V7X_LANES), lambda i: (0, 0)),
                  pl.BlockSpec((1, V7X_LANES), lambda i: (0, 0))],
        out_specs=pl.BlockSpec((bm, V7X_LANES), lambda i: (i, 0)),
        out_shape=jax.ShapeDtypeStruct((m, V7X_LANES), F32),
        compiler_params=_params(("parallel",)),
        name="fox_logf",
    )(xt, w_pad, b_pad)


def _split3(v):
    hi = v.astype(BF16)
    r1 = v - hi.astype(F32)
    mid = r1.astype(BF16)
    lo = (r1 - mid.astype(F32)).astype(BF16)
    return hi, mid, lo


def _cumsum_kernel(lf_ref, f_ref, ft_ref, carry_ref):
    j = pl.program_id(1)

    @pl.when(j == 0)
    def _():
        carry_ref[...] = jnp.zeros_like(carry_ref)

    n = lf_ref.shape[0]
    row = lax.broadcasted_iota(jnp.int32, (n, n), 0)
    col = lax.broadcasted_iota(jnp.int32, (n, n), 1)
    tri = jnp.where(row >= col, 1.0, 0.0).astype(BF16)
    hi, mid, lo = _split3(lf_ref[...])
    cs = (_dot(tri, hi) + _dot(tri, mid) + _dot(tri, lo)) + carry_ref[0:1, :]
    f_ref[...] = cs
    ft_ref[...] = cs.T
    carry_ref[...] = jnp.broadcast_to(cs[n - 1:n, :], carry_ref.shape)


def _forget_cumsum(logf, nbat, s):
    n = _row_block(s, CUMSUM_ROWS)
    nj = s // n
    return pl.pallas_call(
        _cumsum_kernel,
        grid=(nbat, nj),
        in_specs=[pl.BlockSpec((n, V7X_LANES), lambda bi, j: (bi * nj + j, 0))],
        out_specs=[pl.BlockSpec((n, V7X_LANES), lambda bi, j: (bi * nj + j, 0)),
                   pl.BlockSpec((None, V7X_LANES, n), lambda bi, j: (bi, 0, j))],
        out_shape=[jax.ShapeDtypeStruct((nbat * s, V7X_LANES), F32),
                   jax.ShapeDtypeStruct((nbat, V7X_LANES, s), F32)],
        scratch_shapes=[pltpu.VMEM((V7X_SUBLANES, V7X_LANES), F32)],
        compiler_params=_params(("parallel", "arbitrary")),
        name="fox_cumsum",
    )(logf)


def _fox_attn_kernel(q_ref, k_ref, v_ref, fq_ref, fk_ref, o_ref, m_ref, l_ref, acc_ref, fq_s, s_ref, p_ref,
                     *, tq, dh, nh):
    grp = pl.program_id(1)
    qi = pl.program_id(2)
    width = nh * dh
    wreps = width // V7X_LANES
    lane = lax.broadcasted_iota(jnp.int32, (tq, width), 1)
    lane_f = lax.broadcasted_iota(jnp.int32, (tq, V7X_LANES), 1)
    q = q_ref[...] * jnp.asarray(dh ** -0.5, q_ref.dtype)
    fq_all = fq_ref[...]
    qh = []
    for hh in range(nh):
        in_head = (lane >= hh * dh) & (lane < (hh + 1) * dh)
        qh.append(jnp.where(in_head, q, jnp.zeros_like(q)))
        fq_col = jnp.sum(jnp.where(lane_f == nh * grp + hh, fq_all, 0.0), axis=1, keepdims=True)
        fq_s[hh] = jnp.broadcast_to(fq_col, (tq, V7X_LANES))
    m_ref[...] = jnp.full(m_ref.shape, NEG_INF, F32)
    l_ref[...] = jnp.zeros_like(l_ref)
    acc_ref[...] = jnp.zeros_like(acc_ref)
    sr = ATTN_STRIP

    def kv_step(j, diagonal):
        r0 = pl.multiple_of(j * tq, tq)
        k = k_ref[pl.ds(r0, tq), :]
        v = v_ref[pl.ds(r0, tq), :]
        for hh in range(nh):
            s_ref[hh] = lax.dot_general(qh[hh], k, (((1,), (1,)), ((), ())), preferred_element_type=F32)
        fks = [fk_ref[hh:hh + 1, pl.ds(r0, tq)] for hh in range(nh)]

        def strip(hh, rows, a, cw):
            t = s_ref[hh, rows, 0:cw] - fks[hh][:, 0:cw]
            if diagonal:
                row = a + lax.broadcasted_iota(jnp.int32, (sr, cw), 0)
                col = lax.broadcasted_iota(jnp.int32, (sr, cw), 1)
                t = jnp.where(col <= row, t, NEG_INF)
            m_prev = m_ref[hh, rows, :]
            fq = fq_s[hh, rows, :]
            m_new = jnp.maximum(m_prev, fq + jnp.max(t, axis=1, keepdims=True))
            alpha = jnp.exp(m_prev - m_new)
            c = fq - m_new
            pr = jnp.exp(t + jnp.concatenate([c] * (cw // V7X_LANES), axis=1))
            l_ref[hh, rows, :] = alpha * l_ref[hh, rows, :] + jnp.sum(pr, axis=1, keepdims=True)
            acc_ref[hh, rows, :] = jnp.concatenate([alpha] * wreps, axis=1) * acc_ref[hh, rows, :]
            m_ref[hh, rows, :] = m_new
            p_ref[hh, rows, 0:cw] = pr.astype(BF16)
            if cw < tq:
                p_ref[hh, rows, cw:tq] = jnp.zeros((sr, tq - cw), BF16)

        for a in range(0, tq, sr):
            cw = min(tq, -(-(a + sr) // V7X_LANES) * V7X_LANES) if diagonal else tq
            for hh in range(nh):
                strip(hh, slice(a, a + sr), a, cw)
        for hh in range(nh):
            acc_ref[hh] += _dot(p_ref[hh], v)

    def body(j, carry):
        kv_step(j, False)
        return carry

    lax.fori_loop(0, qi, body, 0)
    kv_step(qi, True)
    out = jnp.zeros((tq, width), F32)
    for hh in range(nh):
        in_head = (lane >= hh * dh) & (lane < (hh + 1) * dh)
        out = jnp.where(in_head, acc_ref[hh] / jnp.concatenate([l_ref[hh]] * wreps, axis=1), out)
    o_ref[...] = out.astype(o_ref.dtype)


def _fox_attention(qkv, f_rows, f_cols, nbat, s, d, nh):
    dh = d // FOX_HEADS
    ngrp = FOX_HEADS // nh
    width = nh * dh
    assert width % V7X_LANES == 0 and nh <= V7X_SUBLANES
    tq = _row_block(s, ATTN_TILE)
    nq = s // tq
    return pl.pallas_call(
        functools.partial(_fox_attn_kernel, tq=tq, dh=dh, nh=nh),
        grid=(nbat, ngrp, nq),
        in_specs=[pl.BlockSpec((tq, width), lambda bi, p, qi: (bi * nq + qi, p)),
                  pl.BlockSpec((s, width), lambda bi, p, qi: (bi, ngrp + p)),
                  pl.BlockSpec((s, width), lambda bi, p, qi: (bi, 2 * ngrp + p)),
                  pl.BlockSpec((tq, V7X_LANES), lambda bi, p, qi: (bi * nq + qi, 0)),
                  pl.BlockSpec((None, None, V7X_SUBLANES, s), lambda bi, p, qi: (bi, p, 0, 0))],
        out_specs=pl.BlockSpec((tq, width), lambda bi, p, qi: (bi * nq + qi, p)),
        out_shape=jax.ShapeDtypeStruct((nbat * s, d), BF16),
        scratch_shapes=[pltpu.VMEM((nh, tq, V7X_LANES), F32), pltpu.VMEM((nh, tq, V7X_LANES), F32),
                        pltpu.VMEM((nh, tq, width), F32), pltpu.VMEM((nh, tq, V7X_LANES), F32),
                        pltpu.VMEM((nh, tq, tq), F32), pltpu.VMEM((nh, tq, tq), BF16)],
        compiler_params=_params(("parallel", "parallel", "arbitrary")),
        name="fox_attn",
    )(qkv, qkv, qkv, f_rows, f_cols)


def _fox_layer(xt, nbat, s, layer, w_qkvf, b_f, w_o, g, b):
    m, d = xt.shape
    bm = _row_block(m, ROW_TILE)
    qkv = _matmul(xt, w_qkvf, layer, 3 * d, BF16, bm, d)
    logf = _logf(xt, w_qkvf[layer][:, 3 * d:], b_f[layer], bm)
    f_rows, f_t = _forget_cumsum(logf, nbat, s)
    nh = ATTN_HEADS_PER_STEP
    f_cols = jnp.pad(f_t[:, :FOX_HEADS, :].reshape(nbat, FOX_HEADS // nh, nh, s),
                     ((0, 0), (0, 0), (0, V7X_SUBLANES - nh), (0, 0)))
    o = _fox_attention(qkv, f_rows, f_cols, nbat, s, d, nh)
    return _mm_ln(o, w_o, layer, xt, g, b, bm)


PLAN_LEN, PLAN_OFF, PLAN_DST = 0, 1, 2


def _to_slabs(slab_ref, value):
    rows = value.shape[0]
    for c in range(V7X_SUBLANES):
        slab_ref[pl.ds(c, rows, stride=V7X_SUBLANES), :] = value[:, c * V7X_LANES:(c + 1) * V7X_LANES]


def _from_slabs(slab_ref, rows):
    return jnp.concatenate([slab_ref[pl.ds(c, rows, stride=V7X_SUBLANES), :] for c in range(V7X_SUBLANES)], axis=1)


def _slab_rows(ref, row):
    return ref.at[pl.ds(pl.multiple_of(row * V7X_SUBLANES, V7X_SUBLANES), V7X_SUBLANES)]


def _run_copies(src_ref, dst_ref, src_row, dst_row, length, sem, max_len, wait):
    bit = max_len
    while bit >= 1:
        @pl.when((length & bit) != 0)
        def _(bit=bit):
            done = length & (-2 * bit)
            n = bit * V7X_SUBLANES
            s0 = pl.multiple_of((src_row + done) * V7X_SUBLANES, V7X_SUBLANES)
            d0 = pl.multiple_of((dst_row + done) * V7X_SUBLANES, V7X_SUBLANES)
            cp = pltpu.make_async_copy(src_ref.at[pl.ds(s0, n)], dst_ref.at[pl.ds(d0, n)], sem)
            if wait:
                cp.wait()
            else:
                cp.start()

        bit //= 2


def _router_kernel(x_ref, w_ref, q_ref, gt_ref, t_ref, *, bm):
    x = x_ref[...]
    xh = x.astype(BF16)
    xl = (x - xh.astype(F32)).astype(BF16)
    w = w_ref[...]
    wh = w.astype(BF16)
    wl = (w - wh.astype(F32)).astype(BF16)
    logits = (_dot(xh, wh) + _dot(xl, wh)) + _dot(xh, wl)
    lane = lax.broadcasted_iota(jnp.int32, logits.shape, 1)
    logits = jnp.where(lane < N_EXPERTS, logits, NEG_INF)
    m1 = jnp.max(logits, axis=1, keepdims=True)
    i1 = jnp.min(jnp.where(logits == m1, lane, V7X_LANES), axis=1, keepdims=True)
    rest = jnp.where(lane == i1, NEG_INF, logits)
    m2 = jnp.max(rest, axis=1, keepdims=True)
    i2 = jnp.min(jnp.where(rest == m2, lane, V7X_LANES), axis=1, keepdims=True)
    e21 = jnp.exp(m2 - m1)
    g1 = 1.0 / (1.0 + e21)
    g2 = e21 * g1
    oh1 = lane == i1
    oh2 = lane == i2
    hit = jnp.where(oh1 | oh2, 1.0, 0.0)
    row = lax.broadcasted_iota(jnp.int32, (bm, bm), 0)
    col = lax.broadcasted_iota(jnp.int32, (bm, bm), 1)
    strict = jnp.where(row > col, 1.0, 0.0).astype(BF16)
    local = _dot(strict, hit.astype(BF16))
    cnt8 = jnp.broadcast_to(local[bm - 1:bm, :] + hit[bm - 1:bm, :], (V7X_SUBLANES, V7X_LANES))
    lane8 = lax.broadcasted_iota(jnp.int32, cnt8.shape, 1)
    incl = cnt8
    shift = 1
    while shift < N_EXPERTS:
        incl = incl + jnp.where(lane8 >= shift, pltpu.roll(incl, shift, 1), 0.0)
        shift *= 2
    off8 = incl - cnt8
    place = local + off8[0:1, :]
    q1 = jnp.sum(jnp.where(oh1, place, 0.0), axis=1, keepdims=True)
    q2 = jnp.sum(jnp.where(oh2, place, 0.0), axis=1, keepdims=True)
    q_ref[...] = jnp.where(lane == 0, q1, jnp.where(lane == 1, q2, 0.0)).T[0:V7X_SUBLANES, :].astype(jnp.int32)
    gt_ref[...] = jnp.where(lane == 0, g1, jnp.where(lane == 1, g2, 0.0)).T[0:V7X_SUBLANES, :]
    row8 = lax.broadcasted_iota(jnp.int32, cnt8.shape, 0)
    t_ref[...] = jnp.where(row8 == PLAN_LEN, cnt8, jnp.where(row8 == PLAN_OFF, off8, 0.0)).astype(jnp.int32)


def _router(xt, w_router, bm):
    m, d = xt.shape
    nsteps = m // bm
    w_pad = jnp.pad(w_router, ((0, 0), (0, V7X_LANES - w_router.shape[1])))
    return pl.pallas_call(
        functools.partial(_router_kernel, bm=bm),
        grid=(nsteps,),
        in_specs=[pl.BlockSpec((bm, d), lambda i: (i, 0)),
                  pl.BlockSpec((d, V7X_LANES), lambda i: (0, 0))],
        out_specs=[pl.BlockSpec((None, V7X_SUBLANES, bm), lambda i: (i, 0, 0)),
                   pl.BlockSpec((None, V7X_SUBLANES, bm), lambda i: (i, 0, 0)),
                   pl.BlockSpec((None, V7X_SUBLANES, V7X_LANES), lambda i: (i, 0, 0))],
        out_shape=[jax.ShapeDtypeStruct((nsteps, V7X_SUBLANES, bm), jnp.int32),
                   jax.ShapeDtypeStruct((nsteps, V7X_SUBLANES, bm), F32),
                   jax.ShapeDtypeStruct((nsteps, V7X_SUBLANES, V7X_LANES), jnp.int32)],
        compiler_params=_params(("parallel",)),
        name="moe_router",
    )(xt, w_pad)


def _dispatch_kernel(cnt_ref, start_ref, q_ref, t0_ref, t1_ref, t2_ref, x_ref, rows_hbm,
                     xs_ref, srt_ref, zero_ref, sem_rows, sem_idx, *, bm):
    i = pl.program_id(0)
    nsteps = pl.num_programs(0)

    def runs(tref, slot, wait):
        for e in range(N_EXPERTS):
            _run_copies(srt_ref.at[slot], rows_hbm, tref[PLAN_OFF, e], tref[PLAN_DST, e], tref[PLAN_LEN, e],
                        sem_rows.at[slot], bm, wait)

    for slot in range(2):
        @pl.when(i % 2 == slot)
        def _(slot=slot):
            @pl.when(i >= 2)
            def _():
                runs(t2_ref, slot, True)

            _to_slabs(xs_ref, x_ref[...])

            def place_rows(t, c):
                tile = xs_ref[pl.ds(pl.multiple_of(t * V7X_SUBLANES, V7X_SUBLANES), V7X_SUBLANES), :]
                for k in range(TOP_K):
                    p0 = pl.multiple_of(q_ref[k, t] * V7X_SUBLANES, V7X_SUBLANES)
                    srt_ref[slot, pl.ds(p0, V7X_SUBLANES), :] = tile
                return c

            lax.fori_loop(0, bm, place_rows, 0, unroll=8)
            runs(t0_ref, slot, False)

            @pl.when(i == nsteps - 1)
            def _():
                runs(t0_ref, slot, True)

            @pl.when((i == nsteps - 1) & (i >= 1))
            def _():
                runs(t1_ref, 1 - slot, True)

    @pl.when(i == nsteps - 1)
    def _():
        zero_ref[...] = jnp.zeros_like(zero_ref)
        for e in range(N_EXPERTS):
            c0 = start_ref[e] + cnt_ref[e]
            c1 = start_ref[e + 1]

            def pad_copy(row):
                return pltpu.make_async_copy(zero_ref, _slab_rows(rows_hbm, row), sem_idx)

            def start_pad(row, c):
                pad_copy(row).start()
                return c

            def wait_pad(row, c):
                pad_copy(row).wait()
                return c

            lax.fori_loop(c0, c1, start_pad, 0)
            lax.fori_loop(c0, c1, wait_pad, 0)


def _dispatch(xt, plan_q, plan_t, counts, starts, n_rows, bm):
    m, d = xt.shape
    nsteps = m // bm
    smem = pltpu.SMEM

    def table(back):
        return pl.BlockSpec((None, V7X_SUBLANES, V7X_LANES), lambda i, cnt, st: (jnp.maximum(i - back, 0), 0, 0),
                            memory_space=smem)

    grid_spec = pltpu.PrefetchScalarGridSpec(
        num_scalar_prefetch=2,
        grid=(nsteps,),
        in_specs=[pl.BlockSpec((None, V7X_SUBLANES, bm), lambda i, cnt, st: (i, 0, 0), memory_space=smem),
                  table(0), table(1), table(2),
                  pl.BlockSpec((bm, d), lambda i, cnt, st: (i, 0))],
        out_specs=pl.BlockSpec(memory_space=pl.ANY),
        scratch_shapes=[pltpu.VMEM((bm * V7X_SUBLANES, V7X_LANES), F32),
                        pltpu.VMEM((2, TOP_K * bm * V7X_SUBLANES, V7X_LANES), F32),
                        pltpu.VMEM((V7X_SUBLANES, V7X_LANES), F32),
                        pltpu.SemaphoreType.DMA((2,)),
                        pltpu.SemaphoreType.DMA],
    )
    return pl.pallas_call(
        functools.partial(_dispatch_kernel, bm=bm),
        grid_spec=grid_spec,
        out_shape=jax.ShapeDtypeStruct((n_rows * V7X_SUBLANES, V7X_LANES), F32),
        compiler_params=_params(("arbitrary",)),
        name="moe_dispatch",
    )(counts, starts, plan_q, plan_t, plan_t, plan_t, xt)


def _experts_kernel(ie_ref, iv_ref, ni_ref, x_ref, wg_ref, wu_ref, wd_ref, o_ref, xb_ref, acc_ref, *, sb, rb):
    i = pl.program_id(0)
    j = pl.program_id(1)
    last = pl.num_programs(1) - 1
    valid = iv_ref[i]
    used = i < ni_ref[0]

    @pl.when(jnp.logical_not(used) & (j == last))
    def _():
        o_ref[...] = jnp.zeros_like(o_ref)

    @pl.when(used)
    def _():
        @pl.when(j == 0)
        def _():
            xb_ref[...] = _from_slabs(x_ref, sb).astype(BF16)
            acc_ref[...] = jnp.zeros_like(acc_ref)

        def run(nrows):
            xb = xb_ref[0:nrows, :]
            h = _silu_mul(_dot(xb, wg_ref[...].astype(BF16)), _dot(xb, wu_ref[...].astype(BF16))).astype(BF16)
            acc_ref[0:nrows, :] += _dot(h, wd_ref[...].astype(BF16))

        @pl.when(valid > rb)
        def _():
            run(sb)

        @pl.when(valid <= rb)
        def _():
            run(rb)

        @pl.when(j == last)
        def _():
            _to_slabs(o_ref, acc_ref[...])


def _experts(rows, w_gu, w_down, layer, item_expert, item_valid, n_items, sb, rb, fc):
    assert sb == 2 * rb
    d = w_down.shape[3]
    f = w_down.shape[2]
    nf = f // fc
    n_max = item_expert.shape[0]
    ssb = sb * V7X_SUBLANES

    def item(i, ni):
        return jnp.minimum(i, ni[0] - 1)

    def chunk(i, j, ni):
        return jnp.where(i < ni[0], j, nf - 1)

    grid_spec = pltpu.PrefetchScalarGridSpec(
        num_scalar_prefetch=3,
        grid=(n_max, nf),
        in_specs=[pl.BlockSpec((ssb, V7X_LANES), lambda i, j, ie, iv, ni: (item(i, ni), 0)),
                  pl.BlockSpec((None, None, d, fc), lambda i, j, ie, iv, ni: (layer, ie[item(i, ni)], 0, chunk(i, j, ni))),
                  pl.BlockSpec((None, None, d, fc), lambda i, j, ie, iv, ni: (layer, ie[item(i, ni)], 0, nf + chunk(i, j, ni))),
                  pl.BlockSpec((None, None, fc, d), lambda i, j, ie, iv, ni: (layer, ie[item(i, ni)], chunk(i, j, ni), 0))],
        out_specs=pl.BlockSpec((ssb, V7X_LANES), lambda i, j, ie, iv, ni: (i, 0)),
        scratch_shapes=[pltpu.VMEM((sb, d), BF16), pltpu.VMEM((sb, d), F32)],
    )
    return pl.pallas_call(
        functools.partial(_experts_kernel, sb=sb, rb=rb),
        grid_spec=grid_spec,
        out_shape=jax.ShapeDtypeStruct(rows.shape, F32),
        compiler_params=_params(("arbitrary", "arbitrary")),
        name="moe_experts",
    )(item_expert, item_valid, n_items, rows, w_gu, w_gu, w_down)


def _combine_ln_kernel(q_ref, gt_ref, tcur_ref, tnext_ref, x_ref, g_ref, b_ref, y_hbm, o_ref, ybuf, mix_ref, sem, *, bm):
    i = pl.program_id(0)
    nsteps = pl.num_programs(0)

    def runs(tref, slot, wait):
        for e in range(N_EXPERTS):
            _run_copies(y_hbm, ybuf.at[slot], tref[PLAN_DST, e], tref[PLAN_OFF, e], tref[PLAN_LEN, e],
                        sem.at[slot], bm, wait)

    for slot in range(2):
        @pl.when(i % 2 == slot)
        def _(slot=slot):
            if slot == 0:
                @pl.when(i == 0)
                def _():
                    runs(tcur_ref, 0, False)

            @pl.when(i + 1 < nsteps)
            def _():
                runs(tnext_ref, 1 - slot, False)

            runs(tcur_ref, slot, True)

            def mix(t, c):
                p1 = pl.multiple_of(q_ref[0, t] * V7X_SUBLANES, V7X_SUBLANES)
                p2 = pl.multiple_of(q_ref[1, t] * V7X_SUBLANES, V7X_SUBLANES)
                t0 = pl.multiple_of(t * V7X_SUBLANES, V7X_SUBLANES)
                mix_ref[pl.ds(t0, V7X_SUBLANES), :] = (gt_ref[0, t] * ybuf[slot, pl.ds(p1, V7X_SUBLANES), :]
                                                       + gt_ref[1, t] * ybuf[slot, pl.ds(p2, V7X_SUBLANES), :])
                return c

            lax.fori_loop(0, bm, mix, 0, unroll=8)
            o_ref[...] = _layer_norm(ALPHA * x_ref[...] + _from_slabs(mix_ref, bm), g_ref[...], b_ref[...])


def _combine_ln(xt, plan_q, plan_g, plan_t, yrows, g, b, bm):
    m, d = xt.shape
    nsteps = m // bm
    smem = pltpu.SMEM
    return pl.pallas_call(
        functools.partial(_combine_ln_kernel, bm=bm),
        grid=(nsteps,),
        in_specs=[pl.BlockSpec((None, V7X_SUBLANES, bm), lambda i: (i, 0, 0), memory_space=smem),
                  pl.BlockSpec((None, V7X_SUBLANES, bm), lambda i: (i, 0, 0), memory_space=smem),
                  pl.BlockSpec((None, V7X_SUBLANES, V7X_LANES), lambda i: (i, 0, 0), memory_space=smem),
                  pl.BlockSpec((None, V7X_SUBLANES, V7X_LANES), lambda i: (jnp.minimum(i + 1, nsteps - 1), 0, 0),
                               memory_space=smem),
                  pl.BlockSpec((bm, d), lambda i: (i, 0)),
                  pl.BlockSpec((1, d), lambda i: (0, 0)),
                  pl.BlockSpec((1, d), lambda i: (0, 0)),
                  pl.BlockSpec(memory_space=pl.ANY)],
        out_specs=pl.BlockSpec((bm, d), lambda i: (i, 0)),
        out_shape=jax.ShapeDtypeStruct((m, d), F32),
        scratch_shapes=[pltpu.VMEM((2, TOP_K * bm * V7X_SUBLANES, V7X_LANES), F32),
                        pltpu.VMEM((bm * V7X_SUBLANES, V7X_LANES), F32),
                        pltpu.SemaphoreType.DMA((2,))],
        compiler_params=_params(("arbitrary",)),
        name="moe_combine_ln",
    )(plan_q, plan_g, plan_t, plan_t, xt, g.reshape(1, d), b.reshape(1, d), yrows)


def _moe_layer(xt, layer, w_router, w_gu, w_down, g, b):
    m, d = xt.shape
    sb = _row_block(m, MOE_SUPER_ROWS)
    rb = _row_block(sb, MOE_ROW_BLOCK)
    bm = _row_block(m, ROUTE_ROWS)
    plan_q, plan_g, plan_t = _router(xt, w_router[layer], bm)
    lens = plan_t[:, PLAN_LEN, :N_EXPERTS]
    counts = jnp.sum(lens, axis=0)
    nsb = (counts + sb - 1) // sb
    iend = jnp.cumsum(nsb)
    n_max = (m * TOP_K) // sb + N_EXPERTS
    starts = jnp.concatenate([(iend - nsb) * sb, jnp.full((1,), n_max * sb, jnp.int32)]).astype(jnp.int32)
    run_dst = starts[None, :N_EXPERTS] + jnp.cumsum(lens, axis=0) - lens
    plan_t = plan_t.at[:, PLAN_DST, :N_EXPERTS].set(run_dst)
    item = jnp.arange(n_max, dtype=jnp.int32)
    item_expert = jnp.minimum(jnp.sum(item[:, None] >= iend[None, :], axis=1), N_EXPERTS - 1).astype(jnp.int32)
    item_valid = jnp.clip(starts[item_expert] + counts[item_expert] - item * sb, 0, sb).astype(jnp.int32)
    n_items = iend[-1:].astype(jnp.int32)
    rows = _dispatch(xt, plan_q, plan_t, counts, starts, n_max * sb, bm)
    yrows = _experts(rows, w_gu, w_down, layer, item_expert, item_valid, n_items, sb, rb, MOE_FF_CHUNK)
    return _combine_ln(xt, plan_q, plan_g, plan_t, yrows, g, b, bm)


def kernel(x, lru_w_in, lru_conv_w, lru_conv_b, lru_w_gates, lru_b_gates, lru_lambda, lru_w_out, pool_w, pool_scale,
           fox_w_qkvf, fox_b_f, fox_w_o, ffn_w_gu, ffn_w_down, moe_router, moe_w_gu, moe_w_down,
           ln_mix_g, ln_mix_b, ln_ffn_g, ln_ffn_b):
    nbat, s, d = x.shape
    xt = x.reshape(nbat * s, d)
    for i in range(DEPTH):
        mixer, j = i % 3, i // 3
        if mixer == 0:
            xt = _rglru_layer(xt, nbat, s, j, lru_w_in, lru_conv_w, lru_conv_b, lru_w_gates, lru_b_gates,
                              lru_lambda, lru_w_out, ln_mix_g[i], ln_mix_b[i])
        elif mixer == 1:
            xt = _pool_layer(xt, nbat, s, j, pool_w, pool_scale, ln_mix_g[i], ln_mix_b[i])
        else:
            xt = _fox_layer(xt, nbat, s, j, fox_w_qkvf, fox_b_f, fox_w_o, ln_mix_g[i], ln_mix_b[i])
        if i % 2 == 0:
            xt = _ffn_ln(xt, ffn_w_gu, ffn_w_down, i // 2, ln_ffn_g[i], ln_ffn_b[i], _row_block(nbat * s, ROW_TILE), FFN_CHUNK)
        else:
            xt = _moe_layer(xt, i // 2, moe_router, moe_w_gu, moe_w_down, ln_ffn_g[i], ln_ffn_b[i])
    return xt.reshape(nbat, s, d)
```

```python
import functools

import jax
import jax.numpy as jnp
from jax import lax
from jax.experimental import pallas as pl
from jax.experimental.pallas import tpu as pltpu

F32 = jnp.float32
BF16 = jnp.bfloat16

DEPTH = 4
LRU_BLOCKS = 8
CONV_WIDTH = 4
LRU_C = 8.0
POOL_WINDOWS = (2, 4, 8, 16)
FOX_HEADS = 16
N_EXPERTS = 8
TOP_K = 2
LN_EPS = 1e-5
NEG_INF = -1e30
ALPHA = (2 * DEPTH) ** 0.25

V7X_LANES = 128
V7X_SUBLANES = 8
V7X_VMEM_LIMIT_BYTES = 58 * 1024 * 1024

ROW_TILE = 1024
FFN_CHUNK = 512
RGLRU_TIME_TILE = 64
POOL_ROWS = 512
POOL_HALO = 32
ATTN_TILE = 512
ATTN_HEADS_PER_STEP = 2
ATTN_STRIP = 64
CUMSUM_ROWS = 512
ROUTE_ROWS = 1024
MOE_SUPER_ROWS = 1024
MOE_ROW_BLOCK = 256
MOE_FF_CHUNK = 896


def _params(semantics, vmem=V7X_VMEM_LIMIT_BYTES):
    return pltpu.CompilerParams(dimension_semantics=semantics, vmem_limit_bytes=vmem)


def _row_block(m, want):
    return want if m % want == 0 else m


def _layer_norm(y, g, b):
    mu = jnp.mean(y, axis=-1, keepdims=True)
    yc = y - mu
    var = jnp.mean(yc * yc, axis=-1, keepdims=True)
    return yc * lax.rsqrt(var + LN_EPS) * g + b


def _dot(a, b):
    return jnp.dot(a, b, preferred_element_type=F32)


def _softplus(x):
    return jnp.maximum(x, 0.0) + jnp.log1p(jnp.exp(-jnp.abs(x)))


def _log_sigmoid(z):
    return -_softplus(-z)


def _gelu_tanh(x):
    return 0.5 * x * (1.0 + jnp.tanh(0.7978845608028654 * (x + 0.044715 * (x * x * x))))


def _silu_mul(g, u):
    return g * jax.nn.sigmoid(g) * u


def _mm_ln_kernel(a_ref, w_ref, x_ref, g_ref, b_ref, o_ref):
    y = _dot(a_ref[...].astype(BF16), w_ref[...].astype(BF16))
    o_ref[...] = _layer_norm(ALPHA * x_ref[...] + y, g_ref[...], b_ref[...])


def _mm_ln(a, w_stack, layer, x, g, b, bm):
    m, k = a.shape
    d = w_stack.shape[2]
    return pl.pallas_call(
        _mm_ln_kernel,
        grid=(m // bm,),
        in_specs=[pl.BlockSpec((bm, k), lambda i: (i, 0)),
                  pl.BlockSpec((None, k, d), lambda i: (layer, 0, 0)),
                  pl.BlockSpec((bm, d), lambda i: (i, 0)),
                  pl.BlockSpec((1, d), lambda i: (0, 0)),
                  pl.BlockSpec((1, d), lambda i: (0, 0))],
        out_specs=pl.BlockSpec((bm, d), lambda i: (i, 0)),
        out_shape=jax.ShapeDtypeStruct((m, d), F32),
        compiler_params=_params(("parallel",)),
        name="mm_ln",
    )(a, w_stack, x, g.reshape(1, d), b.reshape(1, d))


def _ffn_ln_kernel(x_ref, wg_ref, wu_ref, wd_ref, g_ref, b_ref, o_ref, xb_ref, acc_ref):
    j = pl.program_id(1)

    @pl.when(j == 0)
    def _():
        xb_ref[...] = x_ref[...].astype(BF16)
        acc_ref[...] = jnp.zeros_like(acc_ref)

    xb = xb_ref[...]
    h = _silu_mul(_dot(xb, wg_ref[...].astype(BF16)), _dot(xb, wu_ref[...].astype(BF16))).astype(BF16)
    acc_ref[...] += _dot(h, wd_ref[...].astype(BF16))

    @pl.when(j == pl.num_programs(1) - 1)
    def _():
        o_ref[...] = _layer_norm(ALPHA * x_ref[...] + acc_ref[...], g_ref[...], b_ref[...])


def _ffn_ln(x, w_gu_stack, w_down_stack, layer, g, b, bm, fc):
    m, d = x.shape
    f = w_down_stack.shape[1]
    nf = f // fc
    return pl.pallas_call(
        _ffn_ln_kernel,
        grid=(m // bm, nf),
        in_specs=[pl.BlockSpec((bm, d), lambda i, j: (i, 0)),
                  pl.BlockSpec((None, d, fc), lambda i, j: (layer, 0, j)),
                  pl.BlockSpec((None, d, fc), lambda i, j: (layer, 0, nf + j)),
                  pl.BlockSpec((None, fc, d), lambda i, j: (layer, j, 0)),
                  pl.BlockSpec((1, d), lambda i, j: (0, 0)),
                  pl.BlockSpec((1, d), lambda i, j: (0, 0))],
        out_specs=pl.BlockSpec((bm, d), lambda i, j: (i, 0)),
        out_shape=jax.ShapeDtypeStruct((m, d), F32),
        scratch_shapes=[pltpu.VMEM((bm, d), BF16), pltpu.VMEM((bm, d), F32)],
        compiler_params=_params(("parallel", "arbitrary")),
        name="ffn_ln",
    )(x, w_gu_stack, w_gu_stack, w_down_stack, g.reshape(1, d), b.reshape(1, d))


def _rglru_kernel(x_ref, win_ref, cw_ref, cb_ref, wg_ref, bg_ref, lam_ref, wout_ref, g_ref, bt_ref, o_ref,
                  xs_ref, tm_ref, a_ref, b_ref, hs_ref, h_ref, *, ts):
    nb = LRU_BLOCKS
    halo = (CONV_WIDTH - 1) * V7X_SUBLANES
    rows = ts * V7X_SUBLANES
    step = pl.program_id(0)

    @pl.when(step == 0)
    def _():
        h_ref[...] = jnp.zeros_like(h_ref)
        tm_ref[:, 0:halo, :] = jnp.zeros((2 * nb, halo, V7X_LANES), F32)

    @pl.when(step > 0)
    def _():
        tm_ref[:, 0:halo, :] = tm_ref[:, rows:rows + halo, :]

    for bi in range(V7X_SUBLANES):
        for c in range(nb):
            xs_ref[c, pl.ds(bi, ts, stride=V7X_SUBLANES), :] = x_ref[bi, :, c * V7X_LANES:(c + 1) * V7X_LANES]
    x_tm = jnp.concatenate([xs_ref[c] for c in range(nb)], axis=1)
    xb = x_tm.astype(BF16)
    for c2 in range(nb):
        u2 = _dot(xb, win_ref[:, 2 * c2 * V7X_LANES:(2 * c2 + 2) * V7X_LANES])
        tm_ref[2 * c2, halo:halo + rows, :] = u2[:, :V7X_LANES]
        tm_ref[2 * c2 + 1, halo:halo + rows, :] = u2[:, V7X_LANES:]

    for n in range(nb):
        sl = slice(n * V7X_LANES, (n + 1) * V7X_LANES)
        xr = cb_ref[:, sl]
        for k in range(CONV_WIDTH):
            xr = xr + tm_ref[nb + n, k * V7X_SUBLANES:k * V7X_SUBLANES + rows, :] * cw_ref[k:k + 1, sl]
        z = _dot(xr.astype(BF16), wg_ref[n].astype(BF16)) + bg_ref[n]
        r = jax.nn.sigmoid(z[:, :V7X_LANES])
        i = jax.nn.sigmoid(z[:, V7X_LANES:])
        log_a = (-LRU_C * r) * _softplus(-lam_ref[:, sl])
        a = jnp.exp(log_a)
        a_ref[n] = a
        b_ref[n] = jnp.sqrt(-jnp.tanh(log_a) * (a * a + 1.0)) * (i * xr)

    def scan_step(t, hs):
        r0 = pl.multiple_of(t * V7X_SUBLANES, V7X_SUBLANES)
        new = []
        for n in range(nb):
            h = a_ref[n, pl.ds(r0, V7X_SUBLANES), :] * hs[n] + b_ref[n, pl.ds(r0, V7X_SUBLANES), :]
            hs_ref[n, pl.ds(r0, V7X_SUBLANES), :] = h
            new.append(h)
        return tuple(new)

    hs = lax.fori_loop(0, ts, scan_step, tuple(h_ref[n] for n in range(nb)), unroll=8)
    for n in range(nb):
        h_ref[n] = hs[n]

    for n in range(nb):
        a_ref[n] = _gelu_tanh(tm_ref[n, halo:halo + rows, :]) * hs_ref[n]
    gh = jnp.concatenate([a_ref[n] for n in range(nb)], axis=1).astype(BF16)
    y = _layer_norm(ALPHA * x_tm + _dot(gh, wout_ref[...]), g_ref[...], bt_ref[...])
    for c in range(nb):
        b_ref[c] = y[:, c * V7X_LANES:(c + 1) * V7X_LANES]
    for bi in range(V7X_SUBLANES):
        for c in range(nb):
            o_ref[bi, :, c * V7X_LANES:(c + 1) * V7X_LANES] = b_ref[c, pl.ds(bi, ts, stride=V7X_SUBLANES), :]


def _rglru_layer(xt, nbat, s, layer, w_in, conv_w, conv_b, w_gates, b_gates, lam, w_out, g, b):
    m, d = xt.shape
    ts = _row_block(s, RGLRU_TIME_TILE)
    assert nbat == V7X_SUBLANES and d == LRU_BLOCKS * V7X_LANES
    rows = ts * V7X_SUBLANES
    halo = (CONV_WIDTH - 1) * V7X_SUBLANES
    gw = 2 * V7X_LANES
    slab = pltpu.VMEM((LRU_BLOCKS, rows, V7X_LANES), F32)
    out = pl.pallas_call(
        functools.partial(_rglru_kernel, ts=ts),
        grid=(s // ts,),
        in_specs=[pl.BlockSpec((nbat, ts, d), lambda t: (0, t, 0)),
                  pl.BlockSpec((d, 2 * d), lambda t: (0, 0)),
                  pl.BlockSpec((CONV_WIDTH, d), lambda t: (0, 0)),
                  pl.BlockSpec((1, d), lambda t: (0, 0)),
                  pl.BlockSpec((None, LRU_BLOCKS, V7X_LANES, gw), lambda t: (layer, 0, 0, 0)),
                  pl.BlockSpec((LRU_BLOCKS, 1, gw), lambda t: (0, 0, 0)),
                  pl.BlockSpec((1, d), lambda t: (0, 0)),
                  pl.BlockSpec((d, d), lambda t: (0, 0)),
                  pl.BlockSpec((1, d), lambda t: (0, 0)),
                  pl.BlockSpec((1, d), lambda t: (0, 0))],
        out_specs=pl.BlockSpec((nbat, ts, d), lambda t: (0, t, 0)),
        out_shape=jax.ShapeDtypeStruct((nbat, s, d), F32),
        scratch_shapes=[slab,
                        pltpu.VMEM((2 * LRU_BLOCKS, halo + rows, V7X_LANES), F32),
                        slab, slab, slab,
                        pltpu.VMEM((LRU_BLOCKS, V7X_SUBLANES, V7X_LANES), F32)],
        compiler_params=_params(("arbitrary",)),
        name="rglru",
    )(xt.reshape(nbat, s, d), w_in[layer].astype(BF16), conv_w[layer], conv_b[layer].reshape(1, d), w_gates,
      b_gates[layer].reshape(LRU_BLOCKS, 1, gw), lam[layer].reshape(1, d), w_out[layer].astype(BF16),
      g.reshape(1, d), b.reshape(1, d))
    return out.reshape(m, d)


def _pool_ln_kernel(x_ref, prev_ref, w_ref, sc_ref, g_ref, b_ref, o_ref, e_ref, st_ref, y_ref, *, rc, tiles_per_seq):
    i = pl.program_id(0)
    first = (i % tiles_per_seq) == 0
    d = x_ref.shape[1]
    gw = d // len(POOL_WINDOWS)
    e_ref[0:POOL_HALO, :] = jnp.where(first, 0.0, prev_ref[...])
    e_ref[POOL_HALO:, :] = x_ref[...]
    t = ((i % tiles_per_seq) * rc + lax.broadcasted_iota(jnp.int32, (rc, 1), 0) + 1).astype(F32)
    for gi, wl in enumerate(POOL_WINDOWS):
        sl = slice(gi * gw, (gi + 1) * gw)
        lo = V7X_SUBLANES
        shift = 1
        cur = e_ref[lo:, sl] + e_ref[lo - shift:POOL_HALO + rc - shift, sl]
        while 2 * shift < wl:
            shift *= 2
            st_ref[lo:, :] = cur
            nlo = lo + V7X_SUBLANES
            cur = st_ref[nlo:, :] + st_ref[nlo - shift:POOL_HALO + rc - shift, :]
            lo = nlo
        win = cur[POOL_HALO - lo:, :]
        xg = x_ref[:, sl]
        p = win / jnp.minimum(t, float(wl)) - xg
        y_ref[:, sl] = _dot(p.astype(BF16), w_ref[gi].astype(BF16))
    y = y_ref[...] * sc_ref[...]
    o_ref[...] = _layer_norm(ALPHA * x_ref[...] + y, g_ref[...], b_ref[...])


def _pool_layer(xt, nbat, s, layer, pool_w, pool_scale, g, b):
    m, d = xt.shape
    rc = _row_block(s, POOL_ROWS)
    tiles_per_seq = s // rc
    ng = len(POOL_WINDOWS)
    gw = d // ng
    hb = rc // POOL_HALO
    return pl.pallas_call(
        functools.partial(_pool_ln_kernel, rc=rc, tiles_per_seq=tiles_per_seq),
        grid=(m // rc,),
        in_specs=[pl.BlockSpec((rc, d), lambda i: (i, 0)),
                  pl.BlockSpec((POOL_HALO, d), lambda i: (jnp.maximum(i * hb - 1, 0), 0)),
                  pl.BlockSpec((None, ng, gw, gw), lambda i: (layer, 0, 0, 0)),
                  pl.BlockSpec((1, d), lambda i: (0, 0)),
                  pl.BlockSpec((1, d), lambda i: (0, 0)),
                  pl.BlockSpec((1, d), lambda i: (0, 0))],
        out_specs=pl.BlockSpec((rc, d), lambda i: (i, 0)),
        out_shape=jax.ShapeDtypeStruct((m, d), F32),
        scratch_shapes=[pltpu.VMEM((POOL_HALO + rc, d), F32),
                        pltpu.VMEM((POOL_HALO + rc, gw), F32),
                        pltpu.VMEM((rc, d), F32)],
        compiler_params=_params(("parallel",)),
        name="pool_ln",
    )(xt, xt, pool_w, pool_scale[layer].reshape(1, d), g.reshape(1, d), b.reshape(1, d))


def _qkvf_kernel(x_ref, w_ref, wf_ref, bf_ref, o_ref, lf_ref):
    xb = x_ref[...].astype(BF16)
    o_ref[...] = _dot(xb, w_ref[...].astype(BF16)).astype(o_ref.dtype)

    @pl.when(pl.program_id(1) == 0)
    def _():
        lf_ref[...] = _log_sigmoid(_dot(xb, wf_ref[...]) + bf_ref[...])


def _qkvf(xt, w_qkvf, b_f, layer, bm):
    m, d = xt.shape
    h = b_f.shape[1]
    w_pad = jnp.pad(w_qkvf[layer][:, 3 * d:], ((0, 0), (0, V7X_LANES - h))).astype(BF16)
    b_pad = jnp.pad(b_f[layer], (0, V7X_LANES - h)).reshape(1, V7X_LANES)
    return pl.pallas_call(
        _qkvf_kernel,
        grid=(m // bm, 3),
        in_specs=[pl.BlockSpec((bm, d), lambda i, j: (i, 0)),
                  pl.BlockSpec((None, d, d), lambda i, j: (layer, 0, j)),
                  pl.BlockSpec((d, V7X_LANES), lambda i, j: (0, 0)),
                  pl.BlockSpec((1, V7X_LANES), lambda i, j: (0, 0))],
        out_specs=[pl.BlockSpec((bm, d), lambda i, j: (i, j)),
                   pl.BlockSpec((bm, V7X_LANES), lambda i, j: (i, 0))],
        out_shape=[jax.ShapeDtypeStruct((m, 3 * d), BF16),
                   jax.ShapeDtypeStruct((m, V7X_LANES), F32)],
        compiler_params=_params(("parallel", "arbitrary")),
        name="fox_qkvf",
    )(xt, w_qkvf, w_pad, b_pad)


def _split3(v):
    hi = v.astype(BF16)
    r1 = v - hi.astype(F32)
    mid = r1.astype(BF16)
    lo = (r1 - mid.astype(F32)).astype(BF16)
    return hi, mid, lo


def _cumsum_kernel(lf_ref, f_ref, ft_ref, carry_ref):
    j = pl.program_id(1)

    @pl.when(j == 0)
    def _():
        carry_ref[...] = jnp.zeros_like(carry_ref)

    n = lf_ref.shape[0]
    row = lax.broadcasted_iota(jnp.int32, (n, n), 0)
    col = lax.broadcasted_iota(jnp.int32, (n, n), 1)
    tri = jnp.where(row >= col, 1.0, 0.0).astype(BF16)
    hi, mid, lo = _split3(lf_ref[...])
    cs = (_dot(tri, hi) + _dot(tri, mid) + _dot(tri, lo)) + carry_ref[0:1, :]
    f_ref[...] = cs
    ft_ref[...] = cs.T
    carry_ref[...] = jnp.broadcast_to(cs[n - 1:n, :], carry_ref.shape)


def _forget_cumsum(logf, nbat, s):
    n = _row_block(s, CUMSUM_ROWS)
    nj = s // n
    return pl.pallas_call(
        _cumsum_kernel,
        grid=(nbat, nj),
        in_specs=[pl.BlockSpec((n, V7X_LANES), lambda bi, j: (bi * nj + j, 0))],
        out_specs=[pl.BlockSpec((n, V7X_LANES), lambda bi, j: (bi * nj + j, 0)),
                   pl.BlockSpec((None, V7X_LANES, n), lambda bi, j: (bi, 0, j))],
        out_shape=[jax.ShapeDtypeStruct((nbat * s, V7X_LANES), F32),
                   jax.ShapeDtypeStruct((nbat, V7X_LANES, s), F32)],
        scratch_shapes=[pltpu.VMEM((V7X_SUBLANES, V7X_LANES), F32)],
        compiler_params=_params(("parallel", "arbitrary")),
        name="fox_cumsum",
    )(logf)


def _fox_attn_kernel(q_ref, k_ref, v_ref, fq_ref, fk_ref, o_ref, m_ref, l_ref, acc_ref, fq_s, s_ref, p_ref,
                     *, tq, dh, nh):
    grp = pl.program_id(1)
    qi = pl.program_id(2)
    width = nh * dh
    wreps = width // V7X_LANES
    lane = lax.broadcasted_iota(jnp.int32, (tq, width), 1)
    lane_f = lax.broadcasted_iota(jnp.int32, (tq, V7X_LANES), 1)
    q = q_ref[...] * jnp.asarray(dh ** -0.5, q_ref.dtype)
    fq_all = fq_ref[...]
    qh = []
    for hh in range(nh):
        in_head = (lane >= hh * dh) & (lane < (hh + 1) * dh)
        qh.append(jnp.where(in_head, q, jnp.zeros_like(q)))
        fq_col = jnp.sum(jnp.where(lane_f == nh * grp + hh, fq_all, 0.0), axis=1, keepdims=True)
        fq_s[hh] = jnp.broadcast_to(fq_col, (tq, V7X_LANES))
    m_ref[...] = jnp.full(m_ref.shape, NEG_INF, F32)
    l_ref[...] = jnp.zeros_like(l_ref)
    acc_ref[...] = jnp.zeros_like(acc_ref)
    sr = ATTN_STRIP

    def kv_step(j, diagonal):
        r0 = pl.multiple_of(j * tq, tq)
        k = k_ref[pl.ds(r0, tq), :]
        v = v_ref[pl.ds(r0, tq), :]
        for hh in range(nh):
            s_ref[hh] = lax.dot_general(qh[hh], k, (((1,), (1,)), ((), ())), preferred_element_type=F32)
        fks = [fk_ref[hh:hh + 1, pl.ds(r0, tq)] for hh in range(nh)]

        def strip(hh, rows, a, cw):
            t = s_ref[hh, rows, 0:cw] - fks[hh][:, 0:cw]
            if diagonal:
                row = a + lax.broadcasted_iota(jnp.int32, (sr, cw), 0)
                col = lax.broadcasted_iota(jnp.int32, (sr, cw), 1)
                t = jnp.where(col <= row, t, NEG_INF)
            m_prev = m_ref[hh, rows, :]
            fq = fq_s[hh, rows, :]
            m_new = jnp.maximum(m_prev, fq + jnp.max(t, axis=1, keepdims=True))
            alpha = jnp.exp(m_prev - m_new)
            c = fq - m_new
            pr = jnp.exp(t + jnp.concatenate([c] * (cw // V7X_LANES), axis=1))
            l_ref[hh, rows, :] = alpha * l_ref[hh, rows, :] + jnp.sum(pr, axis=1, keepdims=True)
            acc_ref[hh, rows, :] = jnp.concatenate([alpha] * wreps, axis=1) * acc_ref[hh, rows, :]
            m_ref[hh, rows, :] = m_new
            p_ref[hh, rows, 0:cw] = pr.astype(BF16)
            if cw < tq:
                p_ref[hh, rows, cw:tq] = jnp.zeros((sr, tq - cw), BF16)

        for a in range(0, tq, sr):
            cw = min(tq, -(-(a + sr) // V7X_LANES) * V7X_LANES) if diagonal else tq
            for hh in range(nh):
                strip(hh, slice(a, a + sr), a, cw)
        for hh in range(nh):
            acc_ref[hh] += _dot(p_ref[hh], v)

    def body(j, carry):
        kv_step(j, False)
        return carry

    lax.fori_loop(0, qi, body, 0)
    kv_step(qi, True)
    out = jnp.zeros((tq, width), F32)
    for hh in range(nh):
        in_head = (lane >= hh * dh) & (lane < (hh + 1) * dh)
        out = jnp.where(in_head, acc_ref[hh] / jnp.concatenate([l_ref[hh]] * wreps, axis=1), out)
    o_ref[...] = out.astype(o_ref.dtype)


def _fox_attention(qkv, f_rows, f_cols, nbat, s, d, nh):
    dh = d // FOX_HEADS
    ngrp = FOX_HEADS // nh
    width = nh * dh
    assert width % V7X_LANES == 0 and nh <= V7X_SUBLANES
    tq = _row_block(s, ATTN_TILE)
    nq = s // tq
    return pl.pallas_call(
        functools.partial(_fox_attn_kernel, tq=tq, dh=dh, nh=nh),
        grid=(nbat, ngrp, nq),
        in_specs=[pl.BlockSpec((tq, width), lambda bi, p, qi: (bi * nq + qi, p)),
                  pl.BlockSpec((s, width), lambda bi, p, qi: (bi, ngrp + p)),
                  pl.BlockSpec((s, width), lambda bi, p, qi: (bi, 2 * ngrp + p)),
                  pl.BlockSpec((tq, V7X_LANES), lambda bi, p, qi: (bi * nq + qi, 0)),
                  pl.BlockSpec((None, None, V7X_SUBLANES, s), lambda bi, p, qi: (bi, p, 0, 0))],
        out_specs=pl.BlockSpec((tq, width), lambda bi, p, qi: (bi * nq + qi, p)),
        out_shape=jax.ShapeDtypeStruct((nbat * s, d), BF16),
        scratch_shapes=[pltpu.VMEM((nh, tq, V7X_LANES), F32), pltpu.VMEM((nh, tq, V7X_LANES), F32),
                        pltpu.VMEM((nh, tq, width), F32), pltpu.VMEM((nh, tq, V7X_LANES), F32),
                        pltpu.VMEM((nh, tq, tq), F32), pltpu.VMEM((nh, tq, tq), BF16)],
        compiler_params=_params(("parallel", "parallel", "arbitrary")),
        name="fox_attn",
    )(qkv, qkv, qkv, f_rows, f_cols)


def _fox_layer(xt, nbat, s, layer, w_qkvf, b_f, w_o, g, b):
    m, d = xt.shape
    bm = _row_block(m, ROW_TILE)
    qkv, logf = _qkvf(xt, w_qkvf, b_f, layer, bm)
    f_rows, f_t = _forget_cumsum(logf, nbat, s)
    nh = ATTN_HEADS_PER_STEP
    f_cols = jnp.pad(f_t[:, :FOX_HEADS, :].reshape(nbat, FOX_HEADS // nh, nh, s),
                     ((0, 0), (0, 0), (0, V7X_SUBLANES - nh), (0, 0)))
    o = _fox_attention(qkv, f_rows, f_cols, nbat, s, d, nh)
    return _mm_ln(o, w_o, layer, xt, g, b, bm)


PLAN_LEN, PLAN_OFF, PLAN_DST = 0, 1, 2


def _to_slabs(slab_ref, value):
    rows = value.shape[0]
    for c in range(V7X_SUBLANES):
        slab_ref[pl.ds(c, rows, stride=V7X_SUBLANES), :] = value[:, c * V7X_LANES:(c + 1) * V7X_LANES]


def _from_slabs(slab_ref, rows):
    return jnp.concatenate([slab_ref[pl.ds(c, rows, stride=V7X_SUBLANES), :] for c in range(V7X_SUBLANES)], axis=1)


def _slab_rows(ref, row):
    return ref.at[pl.ds(pl.multiple_of(row * V7X_SUBLANES, V7X_SUBLANES), V7X_SUBLANES)]


def _run_copies(src_ref, dst_ref, src_row, dst_row, length, sem, max_len, wait):
    bit = max_len
    while bit >= 1:
        @pl.when((length & bit) != 0)
        def _(bit=bit):
            done = length & (-2 * bit)
            n = bit * V7X_SUBLANES
            s0 = pl.multiple_of((src_row + done) * V7X_SUBLANES, V7X_SUBLANES)
            d0 = pl.multiple_of((dst_row + done) * V7X_SUBLANES, V7X_SUBLANES)
            cp = pltpu.make_async_copy(src_ref.at[pl.ds(s0, n)], dst_ref.at[pl.ds(d0, n)], sem)
            if wait:
                cp.wait()
            else:
                cp.start()

        bit //= 2


def _router_kernel(x_ref, w_ref, q_ref, gt_ref, t_ref, *, bm):
    x = x_ref[...]
    xh = x.astype(BF16)
    xl = (x - xh.astype(F32)).astype(BF16)
    w = w_ref[...]
    wh = w.astype(BF16)
    wl = (w - wh.astype(F32)).astype(BF16)
    logits = (_dot(xh, wh) + _dot(xl, wh)) + _dot(xh, wl)
    lane = lax.broadcasted_iota(jnp.int32, logits.shape, 1)
    logits = jnp.where(lane < N_EXPERTS, logits, NEG_INF)
    m1 = jnp.max(logits, axis=1, keepdims=True)
    i1 = jnp.min(jnp.where(logits == m1, lane, V7X_LANES), axis=1, keepdims=True)
    rest = jnp.where(lane == i1, NEG_INF, logits)
    m2 = jnp.max(rest, axis=1, keepdims=True)
    i2 = jnp.min(jnp.where(rest == m2, lane, V7X_LANES), axis=1, keepdims=True)
    e21 = jnp.exp(m2 - m1)
    g1 = 1.0 / (1.0 + e21)
    g2 = e21 * g1
    oh1 = lane == i1
    oh2 = lane == i2
    hit = jnp.where(oh1 | oh2, 1.0, 0.0)
    row = lax.broadcasted_iota(jnp.int32, (bm, bm), 0)
    col = lax.broadcasted_iota(jnp.int32, (bm, bm), 1)
    strict = jnp.where(row > col, 1.0, 0.0).astype(BF16)
    local = _dot(strict, hit.astype(BF16))
    cnt8 = jnp.broadcast_to(local[bm - 1:bm, :] + hit[bm - 1:bm, :], (V7X_SUBLANES, V7X_LANES))
    lane8 = lax.broadcasted_iota(jnp.int32, cnt8.shape, 1)
    incl = cnt8
    shift = 1
    while shift < N_EXPERTS:
        incl = incl + jnp.where(lane8 >= shift, pltpu.roll(incl, shift, 1), 0.0)
        shift *= 2
    off8 = incl - cnt8
    place = local + off8[0:1, :]
    q1 = jnp.sum(jnp.where(oh1, place, 0.0), axis=1, keepdims=True)
    q2 = jnp.sum(jnp.where(oh2, place, 0.0), axis=1, keepdims=True)
    q_ref[...] = jnp.where(lane == 0, q1, jnp.where(lane == 1, q2, 0.0)).T[0:V7X_SUBLANES, :].astype(jnp.int32)
    gt_ref[...] = jnp.where(lane == 0, g1, jnp.where(lane == 1, g2, 0.0)).T[0:V7X_SUBLANES, :]
    row8 = lax.broadcasted_iota(jnp.int32, cnt8.shape, 0)
    t_ref[...] = jnp.where(row8 == PLAN_LEN, cnt8, jnp.where(row8 == PLAN_OFF, off8, 0.0)).astype(jnp.int32)


def _router(xt, w_router, bm):
    m, d = xt.shape
    nsteps = m // bm
    w_pad = jnp.pad(w_router, ((0, 0), (0, V7X_LANES - w_router.shape[1])))
    return pl.pallas_call(
        functools.partial(_router_kernel, bm=bm),
        grid=(nsteps,),
        in_specs=[pl.BlockSpec((bm, d), lambda i: (i, 0)),
                  pl.BlockSpec((d, V7X_LANES), lambda i: (0, 0))],
        out_specs=[pl.BlockSpec((None, V7X_SUBLANES, bm), lambda i: (i, 0, 0)),
                   pl.BlockSpec((None, V7X_SUBLANES, bm), lambda i: (i, 0, 0)),
                   pl.BlockSpec((None, V7X_SUBLANES, V7X_LANES), lambda i: (i, 0, 0))],
        out_shape=[jax.ShapeDtypeStruct((nsteps, V7X_SUBLANES, bm), jnp.int32),
                   jax.ShapeDtypeStruct((nsteps, V7X_SUBLANES, bm), F32),
                   jax.ShapeDtypeStruct((nsteps, V7X_SUBLANES, V7X_LANES), jnp.int32)],
        compiler_params=_params(("parallel",)),
        name="moe_router",
    )(xt, w_pad)


def _dispatch_kernel(cnt_ref, start_ref, q_ref, t0_ref, t1_ref, t2_ref, x_ref, rows_hbm,
                     xs_ref, srt_ref, zero_ref, sem_rows, sem_idx, *, bm):
    i = pl.program_id(0)
    nsteps = pl.num_programs(0)

    def runs(tref, slot, wait):
        for e in range(N_EXPERTS):
            _run_copies(srt_ref.at[slot], rows_hbm, tref[PLAN_OFF, e], tref[PLAN_DST, e], tref[PLAN_LEN, e],
                        sem_rows.at[slot], bm, wait)

    for slot in range(2):
        @pl.when(i % 2 == slot)
        def _(slot=slot):
            @pl.when(i >= 2)
            def _():
                runs(t2_ref, slot, True)

            _to_slabs(xs_ref, x_ref[...])

            def place_rows(t, c):
                tile = xs_ref[pl.ds(pl.multiple_of(t * V7X_SUBLANES, V7X_SUBLANES), V7X_SUBLANES), :]
                for k in range(TOP_K):
                    p0 = pl.multiple_of(q_ref[k, t] * V7X_SUBLANES, V7X_SUBLANES)
                    srt_ref[slot, pl.ds(p0, V7X_SUBLANES), :] = tile
                return c

            lax.fori_loop(0, bm, place_rows, 0, unroll=8)
            runs(t0_ref, slot, False)

            @pl.when(i == nsteps - 1)
            def _():
                runs(t0_ref, slot, True)

            @pl.when((i == nsteps - 1) & (i >= 1))
            def _():
                runs(t1_ref, 1 - slot, True)

    @pl.when(i == nsteps - 1)
    def _():
        zero_ref[...] = jnp.zeros_like(zero_ref)
        for e in range(N_EXPERTS):
            c0 = start_ref[e] + cnt_ref[e]
            c1 = start_ref[e + 1]

            def pad_copy(row):
                return pltpu.make_async_copy(zero_ref, _slab_rows(rows_hbm, row), sem_idx)

            def start_pad(row, c):
                pad_copy(row).start()
                return c

            def wait_pad(row, c):
                pad_copy(row).wait()
                return c

            lax.fori_loop(c0, c1, start_pad, 0)
            lax.fori_loop(c0, c1, wait_pad, 0)


def _dispatch(xt, plan_q, plan_t, counts, starts, n_rows, bm):
    m, d = xt.shape
    nsteps = m // bm
    smem = pltpu.SMEM

    def table(back):
        return pl.BlockSpec((None, V7X_SUBLANES, V7X_LANES), lambda i, cnt, st: (jnp.maximum(i - back, 0), 0, 0),
                            memory_space=smem)

    grid_spec = pltpu.PrefetchScalarGridSpec(
        num_scalar_prefetch=2,
        grid=(nsteps,),
        in_specs=[pl.BlockSpec((None, V7X_SUBLANES, bm), lambda i, cnt, st: (i, 0, 0), memory_space=smem),
                  table(0), table(1), table(2),
                  pl.BlockSpec((bm, d), lambda i, cnt, st: (i, 0))],
        out_specs=pl.BlockSpec(memory_space=pl.ANY),
        scratch_shapes=[pltpu.VMEM((bm * V7X_SUBLANES, V7X_LANES), F32),
                        pltpu.VMEM((2, TOP_K * bm * V7X_SUBLANES, V7X_LANES), F32),
                        pltpu.VMEM((V7X_SUBLANES, V7X_LANES), F32),
                        pltpu.SemaphoreType.DMA((2,)),
                        pltpu.SemaphoreType.DMA],
    )
    return pl.pallas_call(
        functools.partial(_dispatch_kernel, bm=bm),
        grid_spec=grid_spec,
        out_shape=jax.ShapeDtypeStruct((n_rows * V7X_SUBLANES, V7X_LANES), F32),
        compiler_params=_params(("arbitrary",)),
        name="moe_dispatch",
    )(counts, starts, plan_q, plan_t, plan_t, plan_t, xt)


def _experts_kernel(ie_ref, iv_ref, ni_ref, x_ref, wg_ref, wu_ref, wd_ref, o_ref, xb_ref, acc_ref, *, sb, rb):
    i = pl.program_id(0)
    j = pl.program_id(1)
    last = pl.num_programs(1) - 1
    valid = iv_ref[i]
    used = i < ni_ref[0]

    @pl.when(jnp.logical_not(used) & (j == last))
    def _():
        o_ref[...] = jnp.zeros_like(o_ref)

    @pl.when(used)
    def _():
        @pl.when(j == 0)
        def _():
            xb_ref[...] = _from_slabs(x_ref, sb).astype(BF16)
            acc_ref[...] = jnp.zeros_like(acc_ref)

        def run(nrows):
            xb = xb_ref[0:nrows, :]
            h = _silu_mul(_dot(xb, wg_ref[...].astype(BF16)), _dot(xb, wu_ref[...].astype(BF16))).astype(BF16)
            acc_ref[0:nrows, :] += _dot(h, wd_ref[...].astype(BF16))

        for nrows in range(rb, sb + rb, rb):
            @pl.when((valid > nrows - rb) & (valid <= nrows))
            def _(nrows=nrows):
                run(nrows)

        @pl.when(j == last)
        def _():
            _to_slabs(o_ref, acc_ref[...])


def _experts(rows, w_gu, w_down, layer, item_expert, item_valid, n_items, sb, rb, fc):
    assert sb % rb == 0
    d = w_down.shape[3]
    f = w_down.shape[2]
    nf = f // fc
    n_max = item_expert.shape[0]
    ssb = sb * V7X_SUBLANES

    def item(i, ni):
        return jnp.minimum(i, ni[0] - 1)

    def chunk(i, j, ni):
        return jnp.where(i < ni[0], j, nf - 1)

    grid_spec = pltpu.PrefetchScalarGridSpec(
        num_scalar_prefetch=3,
        grid=(n_max, nf),
        in_specs=[pl.BlockSpec((ssb, V7X_LANES), lambda i, j, ie, iv, ni: (item(i, ni), 0)),
                  pl.BlockSpec((None, None, d, fc), lambda i, j, ie, iv, ni: (layer, ie[item(i, ni)], 0, chunk(i, j, ni))),
                  pl.BlockSpec((None, None, d, fc), lambda i, j, ie, iv, ni: (layer, ie[item(i, ni)], 0, nf + chunk(i, j, ni))),
                  pl.BlockSpec((None, None, fc, d), lambda i, j, ie, iv, ni: (layer, ie[item(i, ni)], chunk(i, j, ni), 0))],
        out_specs=pl.BlockSpec((ssb, V7X_LANES), lambda i, j, ie, iv, ni: (i, 0)),
        scratch_shapes=[pltpu.VMEM((sb, d), BF16), pltpu.VMEM((sb, d), F32)],
    )
    return pl.pallas_call(
        functools.partial(_experts_kernel, sb=sb, rb=rb),
        grid_spec=grid_spec,
        out_shape=jax.ShapeDtypeStruct(rows.shape, F32),
        compiler_params=_params(("arbitrary", "arbitrary")),
        name="moe_experts",
    )(item_expert, item_valid, n_items, rows, w_gu, w_gu, w_down)


def _combine_ln_kernel(q_ref, gt_ref, tcur_ref, tnext_ref, x_ref, g_ref, b_ref, y_hbm, o_ref, ybuf, mix_ref, sem, *, bm):
    i = pl.program_id(0)
    nsteps = pl.num_programs(0)

    def runs(tref, slot, wait):
        for e in range(N_EXPERTS):
            _run_copies(y_hbm, ybuf.at[slot], tref[PLAN_DST, e], tref[PLAN_OFF, e], tref[PLAN_LEN, e],
                        sem.at[slot], bm, wait)

    for slot in range(2):
        @pl.when(i % 2 == slot)
        def _(slot=slot):
            if slot == 0:
                @pl.when(i == 0)
                def _():
                    runs(tcur_ref, 0, False)

            @pl.when(i + 1 < nsteps)
            def _():
                runs(tnext_ref, 1 - slot, False)

            runs(tcur_ref, slot, True)

            def mix(t, c):
                p1 = pl.multiple_of(q_ref[0, t] * V7X_SUBLANES, V7X_SUBLANES)
                p2 = pl.multiple_of(q_ref[1, t] * V7X_SUBLANES, V7X_SUBLANES)
                t0 = pl.multiple_of(t * V7X_SUBLANES, V7X_SUBLANES)
                mix_ref[pl.ds(t0, V7X_SUBLANES), :] = (gt_ref[0, t] * ybuf[slot, pl.ds(p1, V7X_SUBLANES), :]
                                                       + gt_ref[1, t] * ybuf[slot, pl.ds(p2, V7X_SUBLANES), :])
                return c

            lax.fori_loop(0, bm, mix, 0, unroll=8)
            o_ref[...] = _layer_norm(ALPHA * x_ref[...] + _from_slabs(mix_ref, bm), g_ref[...], b_ref[...])


def _combine_ln(xt, plan_q, plan_g, plan_t, yrows, g, b, bm):
    m, d = xt.shape
    nsteps = m // bm
    smem = pltpu.SMEM
    return pl.pallas_call(
        functools.partial(_combine_ln_kernel, bm=bm),
        grid=(nsteps,),
        in_specs=[pl.BlockSpec((None, V7X_SUBLANES, bm), lambda i: (i, 0, 0), memory_space=smem),
                  pl.BlockSpec((None, V7X_SUBLANES, bm), lambda i: (i, 0, 0), memory_space=smem),
                  pl.BlockSpec((None, V7X_SUBLANES, V7X_LANES), lambda i: (i, 0, 0), memory_space=smem),
                  pl.BlockSpec((None, V7X_SUBLANES, V7X_LANES), lambda i: (jnp.minimum(i + 1, nsteps - 1), 0, 0),
                               memory_space=smem),
                  pl.BlockSpec((bm, d), lambda i: (i, 0)),
                  pl.BlockSpec((1, d), lambda i: (0, 0)),
                  pl.BlockSpec((1, d), lambda i: (0, 0)),
                  pl.BlockSpec(memory_space=pl.ANY)],
        out_specs=pl.BlockSpec((bm, d), lambda i: (i, 0)),
        out_shape=jax.ShapeDtypeStruct((m, d), F32),
        scratch_shapes=[pltpu.VMEM((2, TOP_K * bm * V7X_SUBLANES, V7X_LANES), F32),
                        pltpu.VMEM((bm * V7X_SUBLANES, V7X_LANES), F32),
                        pltpu.SemaphoreType.DMA((2,))],
        compiler_params=_params(("arbitrary",)),
        name="moe_combine_ln",
    )(plan_q, plan_g, plan_t, plan_t, xt, g.reshape(1, d), b.reshape(1, d), yrows)


def _moe_layer(xt, layer, w_router, w_gu, w_down, g, b):
    m, d = xt.shape
    sb = _row_block(m, MOE_SUPER_ROWS)
    rb = _row_block(sb, MOE_ROW_BLOCK)
    bm = _row_block(m, ROUTE_ROWS)
    plan_q, plan_g, plan_t = _router(xt, w_router[layer], bm)
    lens = plan_t[:, PLAN_LEN, :N_EXPERTS]
    counts = jnp.sum(lens, axis=0)
    nsb = (counts + sb - 1) // sb
    iend = jnp.cumsum(nsb)
    n_max = (m * TOP_K) // sb + N_EXPERTS
    starts = jnp.concatenate([(iend - nsb) * sb, jnp.full((1,), n_max * sb, jnp.int32)]).astype(jnp.int32)
    run_dst = starts[None, :N_EXPERTS] + jnp.cumsum(lens, axis=0) - lens
    plan_t = plan_t.at[:, PLAN_DST, :N_EXPERTS].set(run_dst)
    item = jnp.arange(n_max, dtype=jnp.int32)
    item_expert = jnp.minimum(jnp.sum(item[:, None] >= iend[None, :], axis=1), N_EXPERTS - 1).astype(jnp.int32)
    item_valid = jnp.clip(starts[item_expert] + counts[item_expert] - item * sb, 0, sb).astype(jnp.int32)
    n_items = iend[-1:].astype(jnp.int32)
    rows = _dispatch(xt, plan_q, plan_t, counts, starts, n_max * sb, bm)
    yrows = _experts(rows, w_gu, w_down, layer, item_expert, item_valid, n_items, sb, rb, MOE_FF_CHUNK)
    return _combine_ln(xt, plan_q, plan_g, plan_t, yrows, g, b, bm)


def kernel(x, lru_w_in, lru_conv_w, lru_conv_b, lru_w_gates, lru_b_gates, lru_lambda, lru_w_out, pool_w, pool_scale,
           fox_w_qkvf, fox_b_f, fox_w_o, ffn_w_gu, ffn_w_down, moe_router, moe_w_gu, moe_w_down,
           ln_mix_g, ln_mix_b, ln_ffn_g, ln_ffn_b):
    nbat, s, d = x.shape
    xt = x.reshape(nbat * s, d)
    for i in range(DEPTH):
        mixer, j = i % 3, i // 3
        if mixer == 0:
            xt = _rglru_layer(xt, nbat, s, j, lru_w_in, lru_conv_w, lru_conv_b, lru_w_gates, lru_b_gates,
                              lru_lambda, lru_w_out, ln_mix_g[i], ln_mix_b[i])
        elif mixer == 1:
            xt = _pool_layer(xt, nbat, s, j, pool_w, pool_scale, ln_mix_g[i], ln_mix_b[i])
        else:
            xt = _fox_layer(xt, nbat, s, j, fox_w_qkvf, fox_b_f, fox_w_o, ln_mix_g[i], ln_mix_b[i])
        if i % 2 == 0:
            xt = _ffn_ln(xt, ffn_w_gu, ffn_w_down, i // 2, ln_ffn_g[i], ln_ffn_b[i], _row_block(nbat * s, ROW_TILE), FFN_CHUNK)
        else:
            xt = _moe_layer(xt, i // 2, moe_router, moe_w_gu, moe_w_down, ln_ffn_g[i], ln_ffn_b[i])
    return xt.reshape(nbat, s, d)
```

```python
import functools

import jax
import jax.numpy as jnp
from jax import lax
from jax.experimental import pallas as pl
from jax.experimental.pallas import tpu as pltpu

F32 = jnp.float32
BF16 = jnp.bfloat16

DEPTH = 4
LRU_BLOCKS = 8
CONV_WIDTH = 4
LRU_C = 8.0
POOL_WINDOWS = (2, 4, 8, 16)
FOX_HEADS = 16
N_EXPERTS = 8
TOP_K = 2
LN_EPS = 1e-5
NEG_INF = -1e30
ALPHA = (2 * DEPTH) ** 0.25

V7X_LANES = 128
V7X_SUBLANES = 8
V7X_VMEM_LIMIT_BYTES = 58 * 1024 * 1024

ROW_TILE = 1024
FFN_CHUNK = 512
RGLRU_TIME_TILE = 64
POOL_ROWS = 512
POOL_HALO = 32
ATTN_TILE = 512
ATTN_HEADS_PER_STEP = 2
ATTN_STRIP = 64
CUMSUM_ROWS = 512
ROUTE_ROWS = 1024
MOE_SUPER_ROWS = 1024
MOE_ROW_BLOCK = 256
MOE_FF_CHUNK = 896


def _params(semantics, vmem=V7X_VMEM_LIMIT_BYTES):
    return pltpu.CompilerParams(dimension_semantics=semantics, vmem_limit_bytes=vmem)


def _row_block(m, want):
    return want if m % want == 0 else m


def _layer_norm(y, g, b):
    mu = jnp.mean(y, axis=-1, keepdims=True)
    yc = y - mu
    var = jnp.mean(yc * yc, axis=-1, keepdims=True)
    return yc * lax.rsqrt(var + LN_EPS) * g + b


def _dot(a, b):
    return jnp.dot(a, b, preferred_element_type=F32)


def _softplus(x):
    return jnp.maximum(x, 0.0) + jnp.log1p(jnp.exp(-jnp.abs(x)))


def _log_sigmoid(z):
    return -_softplus(-z)


def _gelu_tanh(x):
    return 0.5 * x * (1.0 + jnp.tanh(0.7978845608028654 * (x + 0.044715 * (x * x * x))))


def _silu_mul(g, u):
    return g * jax.nn.sigmoid(g) * u


def _mm_ln_kernel(a_ref, w_ref, x_ref, g_ref, b_ref, o_ref):
    y = _dot(a_ref[...].astype(BF16), w_ref[...].astype(BF16))
    o_ref[...] = _layer_norm(ALPHA * x_ref[...] + y, g_ref[...], b_ref[...])


def _mm_ln(a, w_stack, layer, x, g, b, bm):
    m, k = a.shape
    d = w_stack.shape[2]
    return pl.pallas_call(
        _mm_ln_kernel,
        grid=(m // bm,),
        in_specs=[pl.BlockSpec((bm, k), lambda i: (i, 0)),
                  pl.BlockSpec((None, k, d), lambda i: (layer, 0, 0)),
                  pl.BlockSpec((bm, d), lambda i: (i, 0)),
                  pl.BlockSpec((1, d), lambda i: (0, 0)),
                  pl.BlockSpec((1, d), lambda i: (0, 0))],
        out_specs=pl.BlockSpec((bm, d), lambda i: (i, 0)),
        out_shape=jax.ShapeDtypeStruct((m, d), F32),
        compiler_params=_params(("parallel",)),
        name="mm_ln",
    )(a, w_stack, x, g.reshape(1, d), b.reshape(1, d))


def _ffn_ln_kernel(x_ref, wg_ref, wu_ref, wd_ref, g_ref, b_ref, o_ref, xb_ref, acc_ref):
    j = pl.program_id(1)

    @pl.when(j == 0)
    def _():
        xb_ref[...] = x_ref[...].astype(BF16)
        acc_ref[...] = jnp.zeros_like(acc_ref)

    xb = xb_ref[...]
    h = _silu_mul(_dot(xb, wg_ref[...].astype(BF16)), _dot(xb, wu_ref[...].astype(BF16))).astype(BF16)
    acc_ref[...] += _dot(h, wd_ref[...].astype(BF16))

    @pl.when(j == pl.num_programs(1) - 1)
    def _():
        o_ref[...] = _layer_norm(ALPHA * x_ref[...] + acc_ref[...], g_ref[...], b_ref[...])


def _ffn_ln(x, w_gu_stack, w_down_stack, layer, g, b, bm, fc):
    m, d = x.shape
    f = w_down_stack.shape[1]
    nf = f // fc
    return pl.pallas_call(
        _ffn_ln_kernel,
        grid=(m // bm, nf),
        in_specs=[pl.BlockSpec((bm, d), lambda i, j: (i, 0)),
                  pl.BlockSpec((None, d, fc), lambda i, j: (layer, 0, j)),
                  pl.BlockSpec((None, d, fc), lambda i, j: (layer, 0, nf + j)),
                  pl.BlockSpec((None, fc, d), lambda i, j: (layer, j, 0)),
                  pl.BlockSpec((1, d), lambda i, j: (0, 0)),
                  pl.BlockSpec((1, d), lambda i, j: (0, 0))],
        out_specs=pl.BlockSpec((bm, d), lambda i, j: (i, 0)),
        out_shape=jax.ShapeDtypeStruct((m, d), F32),
        scratch_shapes=[pltpu.VMEM((bm, d), BF16), pltpu.VMEM((bm, d), F32)],
        compiler_params=_params(("parallel", "arbitrary")),
        name="ffn_ln",
    )(x, w_gu_stack, w_gu_stack, w_down_stack, g.reshape(1, d), b.reshape(1, d))


def _rglru_kernel(x_ref, win_ref, cw_ref, cb_ref, wg_ref, bg_ref, lam_ref, wout_ref, g_ref, bt_ref, o_ref,
                  xs_ref, tm_ref, a_ref, b_ref, hs_ref, h_ref, *, ts):
    nb = LRU_BLOCKS
    halo = (CONV_WIDTH - 1) * V7X_SUBLANES
    rows = ts * V7X_SUBLANES
    step = pl.program_id(0)

    @pl.when(step == 0)
    def _():
        h_ref[...] = jnp.zeros_like(h_ref)
        tm_ref[:, 0:halo, :] = jnp.zeros((2 * nb, halo, V7X_LANES), F32)

    @pl.when(step > 0)
    def _():
        tm_ref[:, 0:halo, :] = tm_ref[:, rows:rows + halo, :]

    for bi in range(V7X_SUBLANES):
        for c in range(nb):
            xs_ref[c, pl.ds(bi, ts, stride=V7X_SUBLANES), :] = x_ref[bi, :, c * V7X_LANES:(c + 1) * V7X_LANES]
    x_tm = jnp.concatenate([xs_ref[c] for c in range(nb)], axis=1)
    xb = x_tm.astype(BF16)
    for c2 in range(nb):
        u2 = _dot(xb, win_ref[:, 2 * c2 * V7X_LANES:(2 * c2 + 2) * V7X_LANES])
        tm_ref[2 * c2, halo:halo + rows, :] = u2[:, :V7X_LANES]
        tm_ref[2 * c2 + 1, halo:halo + rows, :] = u2[:, V7X_LANES:]

    for n in range(nb):
        sl = slice(n * V7X_LANES, (n + 1) * V7X_LANES)
        xr = cb_ref[:, sl]
        for k in range(CONV_WIDTH):
            xr = xr + tm_ref[nb + n, k * V7X_SUBLANES:k * V7X_SUBLANES + rows, :] * cw_ref[k:k + 1, sl]
        z = _dot(xr.astype(BF16), wg_ref[n].astype(BF16)) + bg_ref[n]
        r = jax.nn.sigmoid(z[:, :V7X_LANES])
        i = jax.nn.sigmoid(z[:, V7X_LANES:])
        log_a = (-LRU_C * r) * _softplus(-lam_ref[:, sl])
        a = jnp.exp(log_a)
        a_ref[n] = a
        b_ref[n] = jnp.sqrt(-jnp.tanh(log_a) * (a * a + 1.0)) * (i * xr)

    def scan_step(t, hs):
        r0 = pl.multiple_of(t * V7X_SUBLANES, V7X_SUBLANES)
        new = []
        for n in range(nb):
            h = a_ref[n, pl.ds(r0, V7X_SUBLANES), :] * hs[n] + b_ref[n, pl.ds(r0, V7X_SUBLANES), :]
            hs_ref[n, pl.ds(r0, V7X_SUBLANES), :] = h
            new.append(h)
        return tuple(new)

    hs = lax.fori_loop(0, ts, scan_step, tuple(h_ref[n] for n in range(nb)), unroll=8)
    for n in range(nb):
        h_ref[n] = hs[n]

    for n in range(nb):
        a_ref[n] = _gelu_tanh(tm_ref[n, halo:halo + rows, :]) * hs_ref[n]
    gh = jnp.concatenate([a_ref[n] for n in range(nb)], axis=1).astype(BF16)
    y = _layer_norm(ALPHA * x_tm + _dot(gh, wout_ref[...]), g_ref[...], bt_ref[...])
    for c in range(nb):
        b_ref[c] = y[:, c * V7X_LANES:(c + 1) * V7X_LANES]
    for bi in range(V7X_SUBLANES):
        for c in range(nb):
            o_ref[bi, :, c * V7X_LANES:(c + 1) * V7X_LANES] = b_ref[c, pl.ds(bi, ts, stride=V7X_SUBLANES), :]


def _rglru_layer(xt, nbat, s, layer, w_in, conv_w, conv_b, w_gates, b_gates, lam, w_out, g, b):
    m, d = xt.shape
    ts = _row_block(s, RGLRU_TIME_TILE)
    assert nbat == V7X_SUBLANES and d == LRU_BLOCKS * V7X_LANES
    rows = ts * V7X_SUBLANES
    halo = (CONV_WIDTH - 1) * V7X_SUBLANES
    gw = 2 * V7X_LANES
    slab = pltpu.VMEM((LRU_BLOCKS, rows, V7X_LANES), F32)
    out = pl.pallas_call(
        functools.partial(_rglru_kernel, ts=ts),
        grid=(s // ts,),
        in_specs=[pl.BlockSpec((nbat, ts, d), lambda t: (0, t, 0)),
                  pl.BlockSpec((d, 2 * d), lambda t: (0, 0)),
                  pl.BlockSpec((CONV_WIDTH, d), lambda t: (0, 0)),
                  pl.BlockSpec((1, d), lambda t: (0, 0)),
                  pl.BlockSpec((None, LRU_BLOCKS, V7X_LANES, gw), lambda t: (layer, 0, 0, 0)),
                  pl.BlockSpec((LRU_BLOCKS, 1, gw), lambda t: (0, 0, 0)),
                  pl.BlockSpec((1, d), lambda t: (0, 0)),
                  pl.BlockSpec((d, d), lambda t: (0, 0)),
                  pl.BlockSpec((1, d), lambda t: (0, 0)),
                  pl.BlockSpec((1, d), lambda t: (0, 0))],
        out_specs=pl.BlockSpec((nbat, ts, d), lambda t: (0, t, 0)),
        out_shape=jax.ShapeDtypeStruct((nbat, s, d), F32),
        scratch_shapes=[slab,
                        pltpu.VMEM((2 * LRU_BLOCKS, halo + rows, V7X_LANES), F32),
                        slab, slab, slab,
                        pltpu.VMEM((LRU_BLOCKS, V7X_SUBLANES, V7X_LANES), F32)],
        compiler_params=_params(("arbitrary",)),
        name="rglru",
    )(xt.reshape(nbat, s, d), w_in[layer].astype(BF16), conv_w[layer], conv_b[layer].reshape(1, d), w_gates,
      b_gates[layer].reshape(LRU_BLOCKS, 1, gw), lam[layer].reshape(1, d), w_out[layer].astype(BF16),
      g.reshape(1, d), b.reshape(1, d))
    return out.reshape(m, d)


def _pool_ln_kernel(x_ref, prev_ref, w_ref, sc_ref, g_ref, b_ref, o_ref, e_ref, st_ref, y_ref, *, rc, tiles_per_seq):
    i = pl.program_id(0)
    first = (i % tiles_per_seq) == 0
    d = x_ref.shape[1]
    gw = d // len(POOL_WINDOWS)
    e_ref[0:POOL_HALO, :] = jnp.where(first, 0.0, prev_ref[...])
    e_ref[POOL_HALO:, :] = x_ref[...]
    t = ((i % tiles_per_seq) * rc + lax.broadcasted_iota(jnp.int32, (rc, 1), 0) + 1).astype(F32)
    for gi, wl in enumerate(POOL_WINDOWS):
        sl = slice(gi * gw, (gi + 1) * gw)
        lo = V7X_SUBLANES
        shift = 1
        cur = e_ref[lo:, sl] + e_ref[lo - shift:POOL_HALO + rc - shift, sl]
        while 2 * shift < wl:
            shift *= 2
            st_ref[lo:, :] = cur
            nlo = lo + V7X_SUBLANES
            cur = st_ref[nlo:, :] + st_ref[nlo - shift:POOL_HALO + rc - shift, :]
            lo = nlo
        win = cur[POOL_HALO - lo:, :]
        xg = x_ref[:, sl]
        p = win / jnp.minimum(t, float(wl)) - xg
        y_ref[:, sl] = _dot(p.astype(BF16), w_ref[gi].astype(BF16))
    y = y_ref[...] * sc_ref[...]
    o_ref[...] = _layer_norm(ALPHA * x_ref[...] + y, g_ref[...], b_ref[...])


def _pool_layer(xt, nbat, s, layer, pool_w, pool_scale, g, b):
    m, d = xt.shape
    rc = _row_block(s, POOL_ROWS)
    tiles_per_seq = s // rc
    ng = len(POOL_WINDOWS)
    gw = d // ng
    hb = rc // POOL_HALO
    return pl.pallas_call(
        functools.partial(_pool_ln_kernel, rc=rc, tiles_per_seq=tiles_per_seq),
        grid=(m // rc,),
        in_specs=[pl.BlockSpec((rc, d), lambda i: (i, 0)),
                  pl.BlockSpec((POOL_HALO, d), lambda i: (jnp.maximum(i * hb - 1, 0), 0)),
                  pl.BlockSpec((None, ng, gw, gw), lambda i: (layer, 0, 0, 0)),
                  pl.BlockSpec((1, d), lambda i: (0, 0)),
                  pl.BlockSpec((1, d), lambda i: (0, 0)),
                  pl.BlockSpec((1, d), lambda i: (0, 0))],
        out_specs=pl.BlockSpec((rc, d), lambda i: (i, 0)),
        out_shape=jax.ShapeDtypeStruct((m, d), F32),
        scratch_shapes=[pltpu.VMEM((POOL_HALO + rc, d), F32),
                        pltpu.VMEM((POOL_HALO + rc, gw), F32),
                        pltpu.VMEM((rc, d), F32)],
        compiler_params=_params(("parallel",)),
        name="pool_ln",
    )(xt, xt, pool_w, pool_scale[layer].reshape(1, d), g.reshape(1, d), b.reshape(1, d))


def _qkvf_kernel(x_ref, w_ref, wf_ref, bf_ref, o_ref, lf_ref):
    xb = x_ref[...].astype(BF16)
    o_ref[...] = _dot(xb, w_ref[...].astype(BF16)).astype(o_ref.dtype)

    @pl.when(pl.program_id(1) == 0)
    def _():
        lf_ref[...] = _log_sigmoid(_dot(xb, wf_ref[...]) + bf_ref[...])


def _qkvf(xt, w_qkvf, b_f, layer, bm):
    m, d = xt.shape
    h = b_f.shape[1]
    w_pad = jnp.pad(w_qkvf[layer][:, 3 * d:], ((0, 0), (0, V7X_LANES - h))).astype(BF16)
    b_pad = jnp.pad(b_f[layer], (0, V7X_LANES - h)).reshape(1, V7X_LANES)
    return pl.pallas_call(
        _qkvf_kernel,
        grid=(m // bm, 3),
        in_specs=[pl.BlockSpec((bm, d), lambda i, j: (i, 0)),
                  pl.BlockSpec((None, d, d), lambda i, j: (layer, 0, j)),
                  pl.BlockSpec((d, V7X_LANES), lambda i, j: (0, 0)),
                  pl.BlockSpec((1, V7X_LANES), lambda i, j: (0, 0))],
        out_specs=[pl.BlockSpec((bm, d), lambda i, j: (i, j)),
                   pl.BlockSpec((bm, V7X_LANES), lambda i, j: (i, 0))],
        out_shape=[jax.ShapeDtypeStruct((m, 3 * d), BF16),
                   jax.ShapeDtypeStruct((m, V7X_LANES), F32)],
        compiler_params=_params(("parallel", "arbitrary")),
        name="fox_qkvf",
    )(xt, w_qkvf, w_pad, b_pad)


def _split3(v):
    hi = v.astype(BF16)
    r1 = v - hi.astype(F32)
    mid = r1.astype(BF16)
    lo = (r1 - mid.astype(F32)).astype(BF16)
    return hi, mid, lo


def _cumsum_kernel(lf_ref, f_ref, ft_ref, carry_ref):
    j = pl.program_id(1)

    @pl.when(j == 0)
    def _():
        carry_ref[...] = jnp.zeros_like(carry_ref)

    n = lf_ref.shape[0]
    row = lax.broadcasted_iota(jnp.int32, (n, n), 0)
    col = lax.broadcasted_iota(jnp.int32, (n, n), 1)
    tri = jnp.where(row >= col, 1.0, 0.0).astype(BF16)
    hi, mid, lo = _split3(lf_ref[...])
    cs = (_dot(tri, hi) + _dot(tri, mid) + _dot(tri, lo)) + carry_ref[0:1, :]
    f_ref[...] = cs
    ft_ref[...] = cs.T
    carry_ref[...] = jnp.broadcast_to(cs[n - 1:n, :], carry_ref.shape)


def _forget_cumsum(logf, nbat, s):
    n = _row_block(s, CUMSUM_ROWS)
    nj = s // n
    return pl.pallas_call(
        _cumsum_kernel,
        grid=(nbat, nj),
        in_specs=[pl.BlockSpec((n, V7X_LANES), lambda bi, j: (bi * nj + j, 0))],
        out_specs=[pl.BlockSpec((n, V7X_LANES), lambda bi, j: (bi * nj + j, 0)),
                   pl.BlockSpec((None, V7X_LANES, n), lambda bi, j: (bi, 0, j))],
        out_shape=[jax.ShapeDtypeStruct((nbat * s, V7X_LANES), F32),
                   jax.ShapeDtypeStruct((nbat, V7X_LANES, s), F32)],
        scratch_shapes=[pltpu.VMEM((V7X_SUBLANES, V7X_LANES), F32)],
        compiler_params=_params(("parallel", "arbitrary")),
        name="fox_cumsum",
    )(logf)


def _fox_attn_kernel(q_ref, k_ref, v_ref, fq_ref, fk_ref, o_ref, m_ref, l_ref, acc_ref, fq_s, s_ref, p_ref,
                     *, tq, dh, nh):
    grp = pl.program_id(1)
    qi = pl.program_id(2)
    width = nh * dh
    wreps = width // V7X_LANES
    lane = lax.broadcasted_iota(jnp.int32, (tq, width), 1)
    lane_f = lax.broadcasted_iota(jnp.int32, (tq, V7X_LANES), 1)
    q = q_ref[...] * jnp.asarray(dh ** -0.5, q_ref.dtype)
    fq_all = fq_ref[...]
    qh = []
    for hh in range(nh):
        in_head = (lane >= hh * dh) & (lane < (hh + 1) * dh)
        qh.append(jnp.where(in_head, q, jnp.zeros_like(q)))
        fq_col = jnp.sum(jnp.where(lane_f == nh * grp + hh, fq_all, 0.0), axis=1, keepdims=True)
        fq_s[hh] = jnp.broadcast_to(fq_col, (tq, V7X_LANES))
    m_ref[...] = jnp.full(m_ref.shape, NEG_INF, F32)
    l_ref[...] = jnp.zeros_like(l_ref)
    acc_ref[...] = jnp.zeros_like(acc_ref)
    sr = ATTN_STRIP

    def kv_step(j, diagonal):
        r0 = pl.multiple_of(j * tq, tq)
        k = k_ref[pl.ds(r0, tq), :]
        v = v_ref[pl.ds(r0, tq), :]
        for hh in range(nh):
            s_ref[hh] = lax.dot_general(qh[hh], k, (((1,), (1,)), ((), ())), preferred_element_type=F32)
        fks = [fk_ref[hh:hh + 1, pl.ds(r0, tq)] for hh in range(nh)]

        def strip(hh, rows, a, cw):
            t = s_ref[hh, rows, 0:cw] - fks[hh][:, 0:cw]
            if diagonal:
                row = a + lax.broadcasted_iota(jnp.int32, (sr, cw), 0)
                col = lax.broadcasted_iota(jnp.int32, (sr, cw), 1)
                t = jnp.where(col <= row, t, NEG_INF)
            m_prev = m_ref[hh, rows, :]
            fq = fq_s[hh, rows, :]
            m_new = jnp.maximum(m_prev, fq + jnp.max(t, axis=1, keepdims=True))
            alpha = jnp.exp(m_prev - m_new)
            c = fq - m_new
            pr = jnp.exp(t + jnp.concatenate([c] * (cw // V7X_LANES), axis=1))
            l_ref[hh, rows, :] = alpha * l_ref[hh, rows, :] + jnp.sum(pr, axis=1, keepdims=True)
            acc_ref[hh, rows, :] = jnp.concatenate([alpha] * wreps, axis=1) * acc_ref[hh, rows, :]
            m_ref[hh, rows, :] = m_new
            p_ref[hh, rows, 0:cw] = pr.astype(BF16)
            if cw < tq:
                p_ref[hh, rows, cw:tq] = jnp.zeros((sr, tq - cw), BF16)

        for a in range(0, tq, sr):
            cw = min(tq, -(-(a + sr) // V7X_LANES) * V7X_LANES) if diagonal else tq
            for hh in range(nh):
                strip(hh, slice(a, a + sr), a, cw)
        for hh in range(nh):
            acc_ref[hh] += _dot(p_ref[hh], v)

    def body(j, carry):
        kv_step(j, False)
        return carry

    lax.fori_loop(0, qi, body, 0)
    kv_step(qi, True)
    out = jnp.zeros((tq, width), F32)
    for hh in range(nh):
        in_head = (lane >= hh * dh) & (lane < (hh + 1) * dh)
        out = jnp.where(in_head, acc_ref[hh] / jnp.concatenate([l_ref[hh]] * wreps, axis=1), out)
    o_ref[...] = out.astype(o_ref.dtype)


def _fox_attention(qkv, f_rows, f_cols, nbat, s, d, nh):
    dh = d // FOX_HEADS
    ngrp = FOX_HEADS // nh
    width = nh * dh
    assert width % V7X_LANES == 0 and nh <= V7X_SUBLANES
    tq = _row_block(s, ATTN_TILE)
    nq = s // tq
    return pl.pallas_call(
        functools.partial(_fox_attn_kernel, tq=tq, dh=dh, nh=nh),
        grid=(nbat, ngrp, nq),
        in_specs=[pl.BlockSpec((tq, width), lambda bi, p, qi: (bi * nq + qi, p)),
                  pl.BlockSpec((s, width), lambda bi, p, qi: (bi, ngrp + p)),
                  pl.BlockSpec((s, width), lambda bi, p, qi: (bi, 2 * ngrp + p)),
                  pl.BlockSpec((tq, V7X_LANES), lambda bi, p, qi: (bi * nq + qi, 0)),
                  pl.BlockSpec((None, None, V7X_SUBLANES, s), lambda bi, p, qi: (bi, p, 0, 0))],
        out_specs=pl.BlockSpec((tq, width), lambda bi, p, qi: (bi * nq + qi, p)),
        out_shape=jax.ShapeDtypeStruct((nbat * s, d), BF16),
        scratch_shapes=[pltpu.VMEM((nh, tq, V7X_LANES), F32), pltpu.VMEM((nh, tq, V7X_LANES), F32),
                        pltpu.VMEM((nh, tq, width), F32), pltpu.VMEM((nh, tq, V7X_LANES), F32),
                        pltpu.VMEM((nh, tq, tq), F32), pltpu.VMEM((nh, tq, tq), BF16)],
        compiler_params=_params(("parallel", "parallel", "arbitrary")),
        name="fox_attn",
    )(qkv, qkv, qkv, f_rows, f_cols)


def _fox_layer(xt, nbat, s, layer, w_qkvf, b_f, w_o, g, b):
    m, d = xt.shape
    bm = _row_block(m, ROW_TILE)
    qkv, logf = _qkvf(xt, w_qkvf, b_f, layer, bm)
    f_rows, f_t = _forget_cumsum(logf, nbat, s)
    nh = ATTN_HEADS_PER_STEP
    f_cols = jnp.pad(f_t[:, :FOX_HEADS, :].reshape(nbat, FOX_HEADS // nh, nh, s),
                     ((0, 0), (0, 0), (0, V7X_SUBLANES - nh), (0, 0)))
    o = _fox_attention(qkv, f_rows, f_cols, nbat, s, d, nh)
    return _mm_ln(o, w_o, layer, xt, g, b, bm)


PLAN_LEN, PLAN_OFF, PLAN_DST = 0, 1, 2
GATHER_LEN, GATHER_SRC, GATHER_DST = 0, 1, 2


def _to_slabs(slab_ref, value):
    rows = value.shape[0]
    for c in range(V7X_SUBLANES):
        slab_ref[pl.ds(c, rows, stride=V7X_SUBLANES), :] = value[:, c * V7X_LANES:(c + 1) * V7X_LANES]


def _from_slabs(slab_ref, rows):
    return jnp.concatenate([slab_ref[pl.ds(c, rows, stride=V7X_SUBLANES), :] for c in range(V7X_SUBLANES)], axis=1)


def _slab_rows(ref, row):
    return ref.at[pl.ds(pl.multiple_of(row * V7X_SUBLANES, V7X_SUBLANES), V7X_SUBLANES)]


def _run_copies(src_ref, dst_ref, src_row, dst_row, length, sem, max_len, wait):
    bit = max_len
    while bit >= 1:
        @pl.when((length & bit) != 0)
        def _(bit=bit):
            done = length & (-2 * bit)
            n = bit * V7X_SUBLANES
            s0 = pl.multiple_of((src_row + done) * V7X_SUBLANES, V7X_SUBLANES)
            d0 = pl.multiple_of((dst_row + done) * V7X_SUBLANES, V7X_SUBLANES)
            cp = pltpu.make_async_copy(src_ref.at[pl.ds(s0, n)], dst_ref.at[pl.ds(d0, n)], sem)
            if wait:
                cp.wait()
            else:
                cp.start()

        bit //= 2


def _router_kernel(x_ref, w_ref, q_ref, gt_ref, t_ref, srt_ref, xs_ref, qvm_ref, qsm_ref, sem, *, bm):
    x = x_ref[...]
    xh = x.astype(BF16)
    xl = (x - xh.astype(F32)).astype(BF16)
    w = w_ref[...]
    wh = w.astype(BF16)
    wl = (w - wh.astype(F32)).astype(BF16)
    logits = (_dot(xh, wh) + _dot(xl, wh)) + _dot(xh, wl)
    lane = lax.broadcasted_iota(jnp.int32, logits.shape, 1)
    logits = jnp.where(lane < N_EXPERTS, logits, NEG_INF)
    m1 = jnp.max(logits, axis=1, keepdims=True)
    i1 = jnp.min(jnp.where(logits == m1, lane, V7X_LANES), axis=1, keepdims=True)
    rest = jnp.where(lane == i1, NEG_INF, logits)
    m2 = jnp.max(rest, axis=1, keepdims=True)
    i2 = jnp.min(jnp.where(rest == m2, lane, V7X_LANES), axis=1, keepdims=True)
    e21 = jnp.exp(m2 - m1)
    g1 = 1.0 / (1.0 + e21)
    g2 = e21 * g1
    oh1 = lane == i1
    oh2 = lane == i2
    hit = jnp.where(oh1 | oh2, 1.0, 0.0)
    row = lax.broadcasted_iota(jnp.int32, (bm, bm), 0)
    col = lax.broadcasted_iota(jnp.int32, (bm, bm), 1)
    strict = jnp.where(row > col, 1.0, 0.0).astype(BF16)
    local = _dot(strict, hit.astype(BF16))
    cnt8 = jnp.broadcast_to(local[bm - 1:bm, :] + hit[bm - 1:bm, :], (V7X_SUBLANES, V7X_LANES))
    lane8 = lax.broadcasted_iota(jnp.int32, cnt8.shape, 1)
    incl = cnt8
    shift = 1
    while shift < N_EXPERTS:
        incl = incl + jnp.where(lane8 >= shift, pltpu.roll(incl, shift, 1), 0.0)
        shift *= 2
    off8 = incl - cnt8
    place = local + off8[0:1, :]
    q1 = jnp.sum(jnp.where(oh1, place, 0.0), axis=1, keepdims=True)
    q2 = jnp.sum(jnp.where(oh2, place, 0.0), axis=1, keepdims=True)
    qvm_ref[...] = jnp.where(lane == 0, q1, jnp.where(lane == 1, q2, 0.0)).T[0:V7X_SUBLANES, :].astype(jnp.int32)
    to_smem = pltpu.make_async_copy(qvm_ref, qsm_ref, sem)
    to_smem.start()
    q_ref[...] = qvm_ref[...]
    gt_ref[...] = jnp.where(lane == 0, g1, jnp.where(lane == 1, g2, 0.0)).T[0:V7X_SUBLANES, :]
    row8 = lax.broadcasted_iota(jnp.int32, cnt8.shape, 0)
    t_ref[...] = jnp.where(row8 == PLAN_LEN, cnt8, jnp.where(row8 == PLAN_OFF, off8, 0.0)).astype(jnp.int32)
    _to_slabs(xs_ref, x)
    to_smem.wait()

    def place_rows(t, c):
        tile = xs_ref[pl.ds(pl.multiple_of(t * V7X_SUBLANES, V7X_SUBLANES), V7X_SUBLANES), :]
        for k in range(TOP_K):
            p0 = pl.multiple_of(qsm_ref[k, t] * V7X_SUBLANES, V7X_SUBLANES)
            srt_ref[pl.ds(p0, V7X_SUBLANES), :] = tile
        return c

    lax.fori_loop(0, bm, place_rows, 0, unroll=8)


def _router(xt, w_router, bm):
    m, d = xt.shape
    nsteps = m // bm
    w_pad = jnp.pad(w_router, ((0, 0), (0, V7X_LANES - w_router.shape[1])))
    srows = TOP_K * bm * V7X_SUBLANES
    return pl.pallas_call(
        functools.partial(_router_kernel, bm=bm),
        grid=(nsteps,),
        in_specs=[pl.BlockSpec((bm, d), lambda i: (i, 0)),
                  pl.BlockSpec((d, V7X_LANES), lambda i: (0, 0))],
        out_specs=[pl.BlockSpec((None, V7X_SUBLANES, bm), lambda i: (i, 0, 0)),
                   pl.BlockSpec((None, V7X_SUBLANES, bm), lambda i: (i, 0, 0)),
                   pl.BlockSpec((None, V7X_SUBLANES, V7X_LANES), lambda i: (i, 0, 0)),
                   pl.BlockSpec((srows, V7X_LANES), lambda i: (i, 0))],
        out_shape=[jax.ShapeDtypeStruct((nsteps, V7X_SUBLANES, bm), jnp.int32),
                   jax.ShapeDtypeStruct((nsteps, V7X_SUBLANES, bm), F32),
                   jax.ShapeDtypeStruct((nsteps, V7X_SUBLANES, V7X_LANES), jnp.int32),
                   jax.ShapeDtypeStruct((nsteps * srows, V7X_LANES), F32)],
        scratch_shapes=[pltpu.VMEM((bm * V7X_SUBLANES, V7X_LANES), F32),
                        pltpu.VMEM((V7X_SUBLANES, bm), jnp.int32),
                        pltpu.SMEM((V7X_SUBLANES, bm), jnp.int32),
                        pltpu.SemaphoreType.DMA],
        compiler_params=_params(("parallel",)),
        name="moe_router",
    )(xt, w_pad)


def _experts_kernel(ie_ref, iv_ref, ni_ref, gcur_ref, gnext_ref, rows_hbm, wg_ref, wu_ref, wd_ref, o_ref,
                    xin_ref, xb_ref, acc_ref, sem, *, sb, rb, ntiles, max_run):
    i = pl.program_id(0)
    j = pl.program_id(1)
    last = pl.num_programs(1) - 1
    valid = iv_ref[i]
    used = i < ni_ref[0]

    def pieces(gref, slot, wait):
        for t in range(ntiles):
            _run_copies(rows_hbm, xin_ref.at[slot], gref[GATHER_SRC, t], gref[GATHER_DST, t], gref[GATHER_LEN, t],
                        sem.at[slot], max_run, wait)

    @pl.when(jnp.logical_not(used) & (j == last))
    def _():
        o_ref[...] = jnp.zeros_like(o_ref)

    @pl.when(used)
    def _():
        for slot in range(2):
            @pl.when((j == 0) & (i % 2 == slot))
            def _(slot=slot):
                if slot == 0:
                    @pl.when(i == 0)
                    def _():
                        xin_ref[...] = jnp.zeros_like(xin_ref)
                        pieces(gcur_ref, 0, False)

                pieces(gcur_ref, slot, True)
                xb_ref[...] = _from_slabs(xin_ref.at[slot], sb).astype(BF16)
                acc_ref[...] = jnp.zeros_like(acc_ref)

                @pl.when(i + 1 < ni_ref[0])
                def _():
                    pieces(gnext_ref, 1 - slot, False)

        def run(nrows):
            xb = xb_ref[0:nrows, :]
            h = _silu_mul(_dot(xb, wg_ref[...].astype(BF16)), _dot(xb, wu_ref[...].astype(BF16))).astype(BF16)
            acc_ref[0:nrows, :] += _dot(h, wd_ref[...].astype(BF16))

        for nrows in range(rb, sb + rb, rb):
            @pl.when((valid > nrows - rb) & (valid <= nrows))
            def _(nrows=nrows):
                run(nrows)

        @pl.when(j == last)
        def _():
            _to_slabs(o_ref, acc_ref[...])


def _experts(rows, gather, w_gu, w_down, layer, item_expert, item_valid, n_items, sb, rb, fc, ntiles, max_run):
    assert sb % rb == 0
    d = w_down.shape[3]
    f = w_down.shape[2]
    nf = f // fc
    n_max = item_expert.shape[0]
    ssb = sb * V7X_SUBLANES
    smem = pltpu.SMEM

    def item(i, ni):
        return jnp.minimum(i, ni[0] - 1)

    def chunk(i, j, ni):
        return jnp.where(i < ni[0], j, nf - 1)

    grid_spec = pltpu.PrefetchScalarGridSpec(
        num_scalar_prefetch=3,
        grid=(n_max, nf),
        in_specs=[pl.BlockSpec((None, V7X_SUBLANES, V7X_LANES), lambda i, j, ie, iv, ni: (item(i, ni), 0, 0), memory_space=smem),
                  pl.BlockSpec((None, V7X_SUBLANES, V7X_LANES),
                               lambda i, j, ie, iv, ni: (jnp.minimum(item(i, ni) + 1, n_max - 1), 0, 0), memory_space=smem),
                  pl.BlockSpec(memory_space=pl.ANY),
                  pl.BlockSpec((None, None, d, fc), lambda i, j, ie, iv, ni: (layer, ie[item(i, ni)], 0, chunk(i, j, ni))),
                  pl.BlockSpec((None, None, d, fc), lambda i, j, ie, iv, ni: (layer, ie[item(i, ni)], 0, nf + chunk(i, j, ni))),
                  pl.BlockSpec((None, None, fc, d), lambda i, j, ie, iv, ni: (layer, ie[item(i, ni)], chunk(i, j, ni), 0))],
        out_specs=pl.BlockSpec((ssb, V7X_LANES), lambda i, j, ie, iv, ni: (i, 0)),
        scratch_shapes=[pltpu.VMEM((2, ssb, V7X_LANES), F32), pltpu.VMEM((sb, d), BF16), pltpu.VMEM((sb, d), F32),
                        pltpu.SemaphoreType.DMA((2,))],
    )
    return pl.pallas_call(
        functools.partial(_experts_kernel, sb=sb, rb=rb, ntiles=ntiles, max_run=max_run),
        grid_spec=grid_spec,
        out_shape=jax.ShapeDtypeStruct((n_max * ssb, V7X_LANES), F32),
        compiler_params=_params(("arbitrary", "arbitrary")),
        name="moe_experts",
    )(item_expert, item_valid, n_items, gather, gather, rows, w_gu, w_gu, w_down)


def _combine_ln_kernel(q_ref, gt_ref, tcur_ref, tnext_ref, x_ref, g_ref, b_ref, y_hbm, o_ref, ybuf, mix_ref, sem, *, bm):
    i = pl.program_id(0)
    nsteps = pl.num_programs(0)

    def runs(tref, slot, wait):
        for e in range(N_EXPERTS):
            _run_copies(y_hbm, ybuf.at[slot], tref[PLAN_DST, e], tref[PLAN_OFF, e], tref[PLAN_LEN, e],
                        sem.at[slot], bm, wait)

    for slot in range(2):
        @pl.when(i % 2 == slot)
        def _(slot=slot):
            if slot == 0:
                @pl.when(i == 0)
                def _():
                    runs(tcur_ref, 0, False)

            @pl.when(i + 1 < nsteps)
            def _():
                runs(tnext_ref, 1 - slot, False)

            runs(tcur_ref, slot, True)

            def mix(t, c):
                p1 = pl.multiple_of(q_ref[0, t] * V7X_SUBLANES, V7X_SUBLANES)
                p2 = pl.multiple_of(q_ref[1, t] * V7X_SUBLANES, V7X_SUBLANES)
                t0 = pl.multiple_of(t * V7X_SUBLANES, V7X_SUBLANES)
                mix_ref[pl.ds(t0, V7X_SUBLANES), :] = (gt_ref[0, t] * ybuf[slot, pl.ds(p1, V7X_SUBLANES), :]
                                                       + gt_ref[1, t] * ybuf[slot, pl.ds(p2, V7X_SUBLANES), :])
                return c

            lax.fori_loop(0, bm, mix, 0, unroll=8)
            o_ref[...] = _layer_norm(ALPHA * x_ref[...] + _from_slabs(mix_ref, bm), g_ref[...], b_ref[...])


def _combine_ln(xt, plan_q, plan_g, plan_t, yrows, g, b, bm):
    m, d = xt.shape
    nsteps = m // bm
    smem = pltpu.SMEM
    return pl.pallas_call(
        functools.partial(_combine_ln_kernel, bm=bm),
        grid=(nsteps,),
        in_specs=[pl.BlockSpec((None, V7X_SUBLANES, bm), lambda i: (i, 0, 0), memory_space=smem),
                  pl.BlockSpec((None, V7X_SUBLANES, bm), lambda i: (i, 0, 0), memory_space=smem),
                  pl.BlockSpec((None, V7X_SUBLANES, V7X_LANES), lambda i: (i, 0, 0), memory_space=smem),
                  pl.BlockSpec((None, V7X_SUBLANES, V7X_LANES), lambda i: (jnp.minimum(i + 1, nsteps - 1), 0, 0),
                               memory_space=smem),
                  pl.BlockSpec((bm, d), lambda i: (i, 0)),
                  pl.BlockSpec((1, d), lambda i: (0, 0)),
                  pl.BlockSpec((1, d), lambda i: (0, 0)),
                  pl.BlockSpec(memory_space=pl.ANY)],
        out_specs=pl.BlockSpec((bm, d), lambda i: (i, 0)),
        out_shape=jax.ShapeDtypeStruct((m, d), F32),
        scratch_shapes=[pltpu.VMEM((2, TOP_K * bm * V7X_SUBLANES, V7X_LANES), F32),
                        pltpu.VMEM((bm * V7X_SUBLANES, V7X_LANES), F32),
                        pltpu.SemaphoreType.DMA((2,))],
        compiler_params=_params(("arbitrary",)),
        name="moe_combine_ln",
    )(plan_q, plan_g, plan_t, plan_t, xt, g.reshape(1, d), b.reshape(1, d), yrows)


def _moe_layer(xt, layer, w_router, w_gu, w_down, g, b):
    m, d = xt.shape
    sb = _row_block(m, MOE_SUPER_ROWS)
    rb = _row_block(sb, MOE_ROW_BLOCK)
    bm = _row_block(m, ROUTE_ROWS)
    ntiles = m // bm
    assert ntiles <= V7X_LANES
    plan_q, plan_g, plan_t, rows = _router(xt, w_router[layer], bm)
    lens = plan_t[:, PLAN_LEN, :N_EXPERTS]
    offs = plan_t[:, PLAN_OFF, :N_EXPERTS]
    counts = jnp.sum(lens, axis=0)
    nsb = (counts + sb - 1) // sb
    iend = jnp.cumsum(nsb)
    n_max = (m * TOP_K) // sb + N_EXPERTS
    starts = ((iend - nsb) * sb).astype(jnp.int32)
    run_lo = jnp.cumsum(lens, axis=0) - lens
    plan_t = plan_t.at[:, PLAN_DST, :N_EXPERTS].set(starts[None, :] + run_lo)
    item = jnp.arange(n_max, dtype=jnp.int32)
    item_expert = jnp.minimum(jnp.sum(item[:, None] >= iend[None, :], axis=1), N_EXPERTS - 1).astype(jnp.int32)
    item_lo = item * sb - starts[item_expert]
    item_valid = jnp.clip(counts[item_expert] - item_lo, 0, sb).astype(jnp.int32)
    n_items = iend[-1:].astype(jnp.int32)
    r_lo = run_lo[:, item_expert].T
    r_len = lens[:, item_expert].T
    lo = jnp.maximum(item_lo[:, None], r_lo)
    hi = jnp.minimum(item_lo[:, None] + sb, r_lo + r_len)
    tile_row0 = (jnp.arange(ntiles, dtype=jnp.int32) * (TOP_K * bm))[None, :]
    gather = jnp.zeros((n_max, V7X_SUBLANES, V7X_LANES), jnp.int32)
    gather = gather.at[:, GATHER_LEN, :ntiles].set(jnp.maximum(hi - lo, 0))
    gather = gather.at[:, GATHER_SRC, :ntiles].set(tile_row0 + offs[:, item_expert].T + (lo - r_lo))
    gather = gather.at[:, GATHER_DST, :ntiles].set(lo - item_lo[:, None])
    yrows = _experts(rows, gather, w_gu, w_down, layer, item_expert, item_valid, n_items, sb, rb, MOE_FF_CHUNK,
                     ntiles, min(bm, sb))
    return _combine_ln(xt, plan_q, plan_g, plan_t, yrows, g, b, bm)


def kernel(x, lru_w_in, lru_conv_w, lru_conv_b, lru_w_gates, lru_b_gates, lru_lambda, lru_w_out, pool_w, pool_scale,
           fox_w_qkvf, fox_b_f, fox_w_o, ffn_w_gu, ffn_w_down, moe_router, moe_w_gu, moe_w_down,
           ln_mix_g, ln_mix_b, ln_ffn_g, ln_ffn_b):
    nbat, s, d = x.shape
    xt = x.reshape(nbat * s, d)
    for i in range(DEPTH):
        mixer, j = i % 3, i // 3
        if mixer == 0:
            xt = _rglru_layer(xt, nbat, s, j, lru_w_in, lru_conv_w, lru_conv_b, lru_w_gates, lru_b_gates,
                              lru_lambda, lru_w_out, ln_mix_g[i], ln_mix_b[i])
        elif mixer == 1:
            xt = _pool_layer(xt, nbat, s, j, pool_w, pool_scale, ln_mix_g[i], ln_mix_b[i])
        else:
            xt = _fox_layer(xt, nbat, s, j, fox_w_qkvf, fox_b_f, fox_w_o, ln_mix_g[i], ln_mix_b[i])
        if i % 2 == 0:
            xt = _ffn_ln(xt, ffn_w_gu, ffn_w_down, i // 2, ln_ffn_g[i], ln_ffn_b[i], _row_block(nbat * s, ROW_TILE), FFN_CHUNK)
        else:
            xt = _moe_layer(xt, i // 2, moe_router, moe_w_gu, moe_w_down, ln_ffn_g[i], ln_ffn_b[i])
    return xt.reshape(nbat, s, d)
```

```python
import functools

import jax
import jax.numpy as jnp
from jax import lax
from jax.experimental import pallas as pl
from jax.experimental.pallas import tpu as pltpu

F32 = jnp.float32
BF16 = jnp.bfloat16

DEPTH = 4
LRU_BLOCKS = 8
CONV_WIDTH = 4
LRU_C = 8.0
POOL_WINDOWS = (2, 4, 8, 16)
FOX_HEADS = 16
N_EXPERTS = 8
TOP_K = 2
LN_EPS = 1e-5
NEG_INF = -1e30
ALPHA = (2 * DEPTH) ** 0.25

V7X_LANES = 128
V7X_SUBLANES = 8
V7X_VMEM_LIMIT_BYTES = 58 * 1024 * 1024

ROW_TILE = 1024
FFN_CHUNK = 512
RGLRU_TIME_TILE = 64
POOL_ROWS = 512
POOL_HALO = 32
ATTN_TILE = 512
ATTN_HEADS_PER_STEP = 2
ATTN_STRIP = 64
CUMSUM_ROWS = 512
ROUTE_ROWS = 1024
MOE_SUPER_ROWS = 1024
MOE_ROW_BLOCK = 256
MOE_FF_CHUNK = 896


def _params(semantics, vmem=V7X_VMEM_LIMIT_BYTES):
    return pltpu.CompilerParams(dimension_semantics=semantics, vmem_limit_bytes=vmem)


def _row_block(m, want):
    return want if m % want == 0 else m


def _layer_norm(y, g, b):
    mu = jnp.mean(y, axis=-1, keepdims=True)
    yc = y - mu
    var = jnp.mean(yc * yc, axis=-1, keepdims=True)
    return yc * lax.rsqrt(var + LN_EPS) * g + b


def _dot(a, b):
    return jnp.dot(a, b, preferred_element_type=F32)


def _softplus(x):
    return jnp.maximum(x, 0.0) + jnp.log1p(jnp.exp(-jnp.abs(x)))


def _log_sigmoid(z):
    return -_softplus(-z)


def _gelu_tanh(x):
    return 0.5 * x * (1.0 + jnp.tanh(0.7978845608028654 * (x + 0.044715 * (x * x * x))))


def _silu_mul(g, u):
    return g * jax.nn.sigmoid(g) * u


def _mm_ln_kernel(a_ref, w_ref, x_ref, g_ref, b_ref, o_ref):
    y = _dot(a_ref[...].astype(BF16), w_ref[...].astype(BF16))
    o_ref[...] = _layer_norm(ALPHA * x_ref[...] + y, g_ref[...], b_ref[...])


def _mm_ln(a, w_stack, layer, x, g, b, bm):
    m, k = a.shape
    d = w_stack.shape[2]
    return pl.pallas_call(
        _mm_ln_kernel,
        grid=(m // bm,),
        in_specs=[pl.BlockSpec((bm, k), lambda i: (i, 0)),
                  pl.BlockSpec((None, k, d), lambda i: (layer, 0, 0)),
                  pl.BlockSpec((bm, d), lambda i: (i, 0)),
                  pl.BlockSpec((1, d), lambda i: (0, 0)),
                  pl.BlockSpec((1, d), lambda i: (0, 0))],
        out_specs=pl.BlockSpec((bm, d), lambda i: (i, 0)),
        out_shape=jax.ShapeDtypeStruct((m, d), F32),
        compiler_params=_params(("parallel",)),
        name="mm_ln",
    )(a, w_stack, x, g.reshape(1, d), b.reshape(1, d))


def _ffn_ln_kernel(x_ref, wg_ref, wu_ref, wd_ref, g_ref, b_ref, o_ref, xb_ref, acc_ref):
    j = pl.program_id(1)

    @pl.when(j == 0)
    def _():
        xb_ref[...] = x_ref[...].astype(BF16)
        acc_ref[...] = jnp.zeros_like(acc_ref)

    xb = xb_ref[...]
    h = _silu_mul(_dot(xb, wg_ref[...].astype(BF16)), _dot(xb, wu_ref[...].astype(BF16))).astype(BF16)
    acc_ref[...] += _dot(h, wd_ref[...].astype(BF16))

    @pl.when(j == pl.num_programs(1) - 1)
    def _():
        o_ref[...] = _layer_norm(ALPHA * x_ref[...] + acc_ref[...], g_ref[...], b_ref[...])


def _ffn_ln(x, w_gu_stack, w_down_stack, layer, g, b, bm, fc):
    m, d = x.shape
    f = w_down_stack.shape[1]
    nf = f // fc
    return pl.pallas_call(
        _ffn_ln_kernel,
        grid=(m // bm, nf),
        in_specs=[pl.BlockSpec((bm, d), lambda i, j: (i, 0)),
                  pl.BlockSpec((None, d, fc), lambda i, j: (layer, 0, j)),
                  pl.BlockSpec((None, d, fc), lambda i, j: (layer, 0, nf + j)),
                  pl.BlockSpec((None, fc, d), lambda i, j: (layer, j, 0)),
                  pl.BlockSpec((1, d), lambda i, j: (0, 0)),
                  pl.BlockSpec((1, d), lambda i, j: (0, 0))],
        out_specs=pl.BlockSpec((bm, d), lambda i, j: (i, 0)),
        out_shape=jax.ShapeDtypeStruct((m, d), F32),
        scratch_shapes=[pltpu.VMEM((bm, d), BF16), pltpu.VMEM((bm, d), F32)],
        compiler_params=_params(("parallel", "arbitrary")),
        name="ffn_ln",
    )(x, w_gu_stack, w_gu_stack, w_down_stack, g.reshape(1, d), b.reshape(1, d))


def _rglru_kernel(x_ref, win_ref, cw_ref, cb_ref, wg_ref, bg_ref, lam_ref, wout_ref, g_ref, bt_ref, o_ref,
                  xs_ref, tm_ref, a_ref, b_ref, hs_ref, h_ref, *, ts):
    nb = LRU_BLOCKS
    halo = (CONV_WIDTH - 1) * V7X_SUBLANES
    rows = ts * V7X_SUBLANES
    step = pl.program_id(0)

    @pl.when(step == 0)
    def _():
        h_ref[...] = jnp.zeros_like(h_ref)
        tm_ref[:, 0:halo, :] = jnp.zeros((2 * nb, halo, V7X_LANES), F32)

    @pl.when(step > 0)
    def _():
        tm_ref[:, 0:halo, :] = tm_ref[:, rows:rows + halo, :]

    for bi in range(V7X_SUBLANES):
        for c in range(nb):
            xs_ref[c, pl.ds(bi, ts, stride=V7X_SUBLANES), :] = x_ref[bi, :, c * V7X_LANES:(c + 1) * V7X_LANES]
    x_tm = jnp.concatenate([xs_ref[c] for c in range(nb)], axis=1)
    xb = x_tm.astype(BF16)
    for c2 in range(nb):
        u2 = _dot(xb, win_ref[:, 2 * c2 * V7X_LANES:(2 * c2 + 2) * V7X_LANES])
        tm_ref[2 * c2, halo:halo + rows, :] = u2[:, :V7X_LANES]
        tm_ref[2 * c2 + 1, halo:halo + rows, :] = u2[:, V7X_LANES:]

    for n in range(nb):
        sl = slice(n * V7X_LANES, (n + 1) * V7X_LANES)
        xr = cb_ref[:, sl]
        for k in range(CONV_WIDTH):
            xr = xr + tm_ref[nb + n, k * V7X_SUBLANES:k * V7X_SUBLANES + rows, :] * cw_ref[k:k + 1, sl]
        z = _dot(xr.astype(BF16), wg_ref[n].astype(BF16)) + bg_ref[n]
        r = jax.nn.sigmoid(z[:, :V7X_LANES])
        i = jax.nn.sigmoid(z[:, V7X_LANES:])
        log_a = (-LRU_C * r) * _softplus(-lam_ref[:, sl])
        a = jnp.exp(log_a)
        a_ref[n] = a
        b_ref[n] = jnp.sqrt(-jnp.tanh(log_a) * (a * a + 1.0)) * (i * xr)

    def scan_step(t, hs):
        r0 = pl.multiple_of(t * V7X_SUBLANES, V7X_SUBLANES)
        new = []
        for n in range(nb):
            h = a_ref[n, pl.ds(r0, V7X_SUBLANES), :] * hs[n] + b_ref[n, pl.ds(r0, V7X_SUBLANES), :]
            hs_ref[n, pl.ds(r0, V7X_SUBLANES), :] = h
            new.append(h)
        return tuple(new)

    hs = lax.fori_loop(0, ts, scan_step, tuple(h_ref[n] for n in range(nb)), unroll=8)
    for n in range(nb):
        h_ref[n] = hs[n]

    for n in range(nb):
        a_ref[n] = _gelu_tanh(tm_ref[n, halo:halo + rows, :]) * hs_ref[n]
    gh = jnp.concatenate([a_ref[n] for n in range(nb)], axis=1).astype(BF16)
    y = _layer_norm(ALPHA * x_tm + _dot(gh, wout_ref[...]), g_ref[...], bt_ref[...])
    for c in range(nb):
        b_ref[c] = y[:, c * V7X_LANES:(c + 1) * V7X_LANES]
    for bi in range(V7X_SUBLANES):
        for c in range(nb):
            o_ref[bi, :, c * V7X_LANES:(c + 1) * V7X_LANES] = b_ref[c, pl.ds(bi, ts, stride=V7X_SUBLANES), :]


def _rglru_layer(xt, nbat, s, layer, w_in, conv_w, conv_b, w_gates, b_gates, lam, w_out, g, b):
    m, d = xt.shape
    ts = _row_block(s, RGLRU_TIME_TILE)
    assert nbat == V7X_SUBLANES and d == LRU_BLOCKS * V7X_LANES
    rows = ts * V7X_SUBLANES
    halo = (CONV_WIDTH - 1) * V7X_SUBLANES
    gw = 2 * V7X_LANES
    slab = pltpu.VMEM((LRU_BLOCKS, rows, V7X_LANES), F32)
    out = pl.pallas_call(
        functools.partial(_rglru_kernel, ts=ts),
        grid=(s // ts,),
        in_specs=[pl.BlockSpec((nbat, ts, d), lambda t: (0, t, 0)),
                  pl.BlockSpec((d, 2 * d), lambda t: (0, 0)),
                  pl.BlockSpec((CONV_WIDTH, d), lambda t: (0, 0)),
                  pl.BlockSpec((1, d), lambda t: (0, 0)),
                  pl.BlockSpec((None, LRU_BLOCKS, V7X_LANES, gw), lambda t: (layer, 0, 0, 0)),
                  pl.BlockSpec((LRU_BLOCKS, 1, gw), lambda t: (0, 0, 0)),
                  pl.BlockSpec((1, d), lambda t: (0, 0)),
                  pl.BlockSpec((d, d), lambda t: (0, 0)),
                  pl.BlockSpec((1, d), lambda t: (0, 0)),
                  pl.BlockSpec((1, d), lambda t: (0, 0))],
        out_specs=pl.BlockSpec((nbat, ts, d), lambda t: (0, t, 0)),
        out_shape=jax.ShapeDtypeStruct((nbat, s, d), F32),
        scratch_shapes=[slab,
                        pltpu.VMEM((2 * LRU_BLOCKS, halo + rows, V7X_LANES), F32),
                        slab, slab, slab,
                        pltpu.VMEM((LRU_BLOCKS, V7X_SUBLANES, V7X_LANES), F32)],
        compiler_params=_params(("arbitrary",)),
        name="rglru",
    )(xt.reshape(nbat, s, d), w_in[layer].astype(BF16), conv_w[layer], conv_b[layer].reshape(1, d), w_gates,
      b_gates[layer].reshape(LRU_BLOCKS, 1, gw), lam[layer].reshape(1, d), w_out[layer].astype(BF16),
      g.reshape(1, d), b.reshape(1, d))
    return out.reshape(m, d)


def _pool_ln_kernel(x_ref, prev_ref, w_ref, sc_ref, g_ref, b_ref, o_ref, e_ref, st_ref, y_ref, *, rc, tiles_per_seq):
    i = pl.program_id(0)
    first = (i % tiles_per_seq) == 0
    d = x_ref.shape[1]
    gw = d // len(POOL_WINDOWS)
    e_ref[0:POOL_HALO, :] = jnp.where(first, 0.0, prev_ref[...])
    e_ref[POOL_HALO:, :] = x_ref[...]
    t = ((i % tiles_per_seq) * rc + lax.broadcasted_iota(jnp.int32, (rc, 1), 0) + 1).astype(F32)
    for gi, wl in enumerate(POOL_WINDOWS):
        sl = slice(gi * gw, (gi + 1) * gw)
        lo = V7X_SUBLANES
        shift = 1
        cur = e_ref[lo:, sl] + e_ref[lo - shift:POOL_HALO + rc - shift, sl]
        while 2 * shift < wl:
            shift *= 2
            st_ref[lo:, :] = cur
            nlo = lo + V7X_SUBLANES
            cur = st_ref[nlo:, :] + st_ref[nlo - shift:POOL_HALO + rc - shift, :]
            lo = nlo
        win = cur[POOL_HALO - lo:, :]
        xg = x_ref[:, sl]
        p = win / jnp.minimum(t, float(wl)) - xg
        y_ref[:, sl] = _dot(p.astype(BF16), w_ref[gi].astype(BF16))
    y = y_ref[...] * sc_ref[...]
    o_ref[...] = _layer_norm(ALPHA * x_ref[...] + y, g_ref[...], b_ref[...])


def _pool_layer(xt, nbat, s, layer, pool_w, pool_scale, g, b):
    m, d = xt.shape
    rc = _row_block(s, POOL_ROWS)
    tiles_per_seq = s // rc
    ng = len(POOL_WINDOWS)
    gw = d // ng
    hb = rc // POOL_HALO
    return pl.pallas_call(
        functools.partial(_pool_ln_kernel, rc=rc, tiles_per_seq=tiles_per_seq),
        grid=(m // rc,),
        in_specs=[pl.BlockSpec((rc, d), lambda i: (i, 0)),
                  pl.BlockSpec((POOL_HALO, d), lambda i: (jnp.maximum(i * hb - 1, 0), 0)),
                  pl.BlockSpec((None, ng, gw, gw), lambda i: (layer, 0, 0, 0)),
                  pl.BlockSpec((1, d), lambda i: (0, 0)),
                  pl.BlockSpec((1, d), lambda i: (0, 0)),
                  pl.BlockSpec((1, d), lambda i: (0, 0))],
        out_specs=pl.BlockSpec((rc, d), lambda i: (i, 0)),
        out_shape=jax.ShapeDtypeStruct((m, d), F32),
        scratch_shapes=[pltpu.VMEM((POOL_HALO + rc, d), F32),
                        pltpu.VMEM((POOL_HALO + rc, gw), F32),
                        pltpu.VMEM((rc, d), F32)],
        compiler_params=_params(("parallel",)),
        name="pool_ln",
    )(xt, xt, pool_w, pool_scale[layer].reshape(1, d), g.reshape(1, d), b.reshape(1, d))


def _qkvf_kernel(x_ref, w_ref, wf_ref, bf_ref, o_ref, lf_ref):
    xb = x_ref[...].astype(BF16)
    o_ref[...] = _dot(xb, w_ref[...]).astype(o_ref.dtype)
    lf_ref[...] = _log_sigmoid(_dot(xb, wf_ref[...]) + bf_ref[...])


def _qkvf(xt, w_qkvf, b_f, layer, bm):
    m, d = xt.shape
    h = b_f.shape[1]
    w_qkv = w_qkvf[layer][:, :3 * d].astype(BF16)
    w_pad = jnp.pad(w_qkvf[layer][:, 3 * d:], ((0, 0), (0, V7X_LANES - h))).astype(BF16)
    b_pad = jnp.pad(b_f[layer], (0, V7X_LANES - h)).reshape(1, V7X_LANES)
    return pl.pallas_call(
        _qkvf_kernel,
        grid=(m // bm,),
        in_specs=[pl.BlockSpec((bm, d), lambda i: (i, 0)),
                  pl.BlockSpec((d, 3 * d), lambda i: (0, 0)),
                  pl.BlockSpec((d, V7X_LANES), lambda i: (0, 0)),
                  pl.BlockSpec((1, V7X_LANES), lambda i: (0, 0))],
        out_specs=[pl.BlockSpec((bm, 3 * d), lambda i: (i, 0)),
                   pl.BlockSpec((bm, V7X_LANES), lambda i: (i, 0))],
        out_shape=[jax.ShapeDtypeStruct((m, 3 * d), BF16),
                   jax.ShapeDtypeStruct((m, V7X_LANES), F32)],
        compiler_params=_params(("parallel",)),
        name="fox_qkvf",
    )(xt, w_qkv, w_pad, b_pad)


def _split3(v):
    hi = v.astype(BF16)
    r1 = v - hi.astype(F32)
    mid = r1.astype(BF16)
    lo = (r1 - mid.astype(F32)).astype(BF16)
    return hi, mid, lo


def _cumsum_kernel(lf_ref, f_ref, ft_ref, carry_ref):
    j = pl.program_id(1)

    @pl.when(j == 0)
    def _():
        carry_ref[...] = jnp.zeros_like(carry_ref)

    n = lf_ref.shape[0]
    row = lax.broadcasted_iota(jnp.int32, (n, n), 0)
    col = lax.broadcasted_iota(jnp.int32, (n, n), 1)
    tri = jnp.where(row >= col, 1.0, 0.0).astype(BF16)
    hi, mid, lo = _split3(lf_ref[...])
    cs = (_dot(tri, hi) + _dot(tri, mid) + _dot(tri, lo)) + carry_ref[0:1, :]
    f_ref[...] = cs
    ft_ref[...] = cs.T
    carry_ref[...] = jnp.broadcast_to(cs[n - 1:n, :], carry_ref.shape)


def _forget_cumsum(logf, nbat, s):
    n = _row_block(s, CUMSUM_ROWS)
    nj = s // n
    return pl.pallas_call(
        _cumsum_kernel,
        grid=(nbat, nj),
        in_specs=[pl.BlockSpec((n, V7X_LANES), lambda bi, j: (bi * nj + j, 0))],
        out_specs=[pl.BlockSpec((n, V7X_LANES), lambda bi, j: (bi * nj + j, 0)),
                   pl.BlockSpec((None, V7X_LANES, n), lambda bi, j: (bi, 0, j))],
        out_shape=[jax.ShapeDtypeStruct((nbat * s, V7X_LANES), F32),
                   jax.ShapeDtypeStruct((nbat, V7X_LANES, s), F32)],
        scratch_shapes=[pltpu.VMEM((V7X_SUBLANES, V7X_LANES), F32)],
        compiler_params=_params(("parallel", "arbitrary")),
        name="fox_cumsum",
    )(logf)


def _fox_attn_kernel(q_ref, k_ref, v_ref, fq_ref, fk_ref, o_ref, m_ref, l_ref, acc_ref, fq_s, s_ref, p_ref,
                     *, tq, dh, nh):
    grp = pl.program_id(1)
    qi = pl.program_id(2)
    width = nh * dh
    wreps = width // V7X_LANES
    lane = lax.broadcasted_iota(jnp.int32, (tq, width), 1)
    lane_f = lax.broadcasted_iota(jnp.int32, (tq, V7X_LANES), 1)
    q = q_ref[...] * jnp.asarray(dh ** -0.5, q_ref.dtype)
    fq_all = fq_ref[...]
    qh = []
    for hh in range(nh):
        in_head = (lane >= hh * dh) & (lane < (hh + 1) * dh)
        qh.append(jnp.where(in_head, q, jnp.zeros_like(q)))
        fq_col = jnp.sum(jnp.where(lane_f == nh * grp + hh, fq_all, 0.0), axis=1, keepdims=True)
        fq_s[hh] = jnp.broadcast_to(fq_col, (tq, V7X_LANES))
    m_ref[...] = jnp.full(m_ref.shape, NEG_INF, F32)
    l_ref[...] = jnp.zeros_like(l_ref)
    acc_ref[...] = jnp.zeros_like(acc_ref)
    sr = ATTN_STRIP

    def kv_step(j, diagonal):
        r0 = pl.multiple_of(j * tq, tq)
        k = k_ref[pl.ds(r0, tq), :]
        v = v_ref[pl.ds(r0, tq), :]
        for hh in range(nh):
            s_ref[hh] = lax.dot_general(qh[hh], k, (((1,), (1,)), ((), ())), preferred_element_type=F32)
        fks = [fk_ref[hh:hh + 1, pl.ds(r0, tq)] for hh in range(nh)]

        def strip(hh, rows, a, cw):
            t = s_ref[hh, rows, 0:cw] - fks[hh][:, 0:cw]
            if diagonal:
                row = a + lax.broadcasted_iota(jnp.int32, (sr, cw), 0)
                col = lax.broadcasted_iota(jnp.int32, (sr, cw), 1)
                t = jnp.where(col <= row, t, NEG_INF)
            m_prev = m_ref[hh, rows, :]
            fq = fq_s[hh, rows, :]
            m_new = jnp.maximum(m_prev, fq + jnp.max(t, axis=1, keepdims=True))
            alpha = jnp.exp(m_prev - m_new)
            c = fq - m_new
            pr = jnp.exp(t + jnp.concatenate([c] * (cw // V7X_LANES), axis=1))
            l_ref[hh, rows, :] = alpha * l_ref[hh, rows, :] + jnp.sum(pr, axis=1, keepdims=True)
            acc_ref[hh, rows, :] = jnp.concatenate([alpha] * wreps, axis=1) * acc_ref[hh, rows, :]
            m_ref[hh, rows, :] = m_new
            p_ref[hh, rows, 0:cw] = pr.astype(BF16)
            if cw < tq:
                p_ref[hh, rows, cw:tq] = jnp.zeros((sr, tq - cw), BF16)

        for a in range(0, tq, sr):
            cw = min(tq, -(-(a + sr) // V7X_LANES) * V7X_LANES) if diagonal else tq
            for hh in range(nh):
                strip(hh, slice(a, a + sr), a, cw)
        for hh in range(nh):
            acc_ref[hh] += _dot(p_ref[hh], v)

    def body(j, carry):
        kv_step(j, False)
        return carry

    lax.fori_loop(0, qi, body, 0)
    kv_step(qi, True)
    out = jnp.zeros((tq, width), F32)
    for hh in range(nh):
        in_head = (lane >= hh * dh) & (lane < (hh + 1) * dh)
        out = jnp.where(in_head, acc_ref[hh] / jnp.concatenate([l_ref[hh]] * wreps, axis=1), out)
    o_ref[...] = out.astype(o_ref.dtype)


def _fox_attention(qkv, f_rows, f_cols, nbat, s, d, nh):
    dh = d // FOX_HEADS
    ngrp = FOX_HEADS // nh
    width = nh * dh
    assert width % V7X_LANES == 0 and nh <= V7X_SUBLANES
    tq = _row_block(s, ATTN_TILE)
    nq = s // tq
    return pl.pallas_call(
        functools.partial(_fox_attn_kernel, tq=tq, dh=dh, nh=nh),
        grid=(nbat, ngrp, nq),
        in_specs=[pl.BlockSpec((tq, width), lambda bi, p, qi: (bi * nq + qi, p)),
                  pl.BlockSpec((s, width), lambda bi, p, qi: (bi, ngrp + p)),
                  pl.BlockSpec((s, width), lambda bi, p, qi: (bi, 2 * ngrp + p)),
                  pl.BlockSpec((tq, V7X_LANES), lambda bi, p, qi: (bi * nq + qi, 0)),
                  pl.BlockSpec((None, None, V7X_SUBLANES, s), lambda bi, p, qi: (bi, p, 0, 0))],
        out_specs=pl.BlockSpec((tq, width), lambda bi, p, qi: (bi * nq + qi, p)),
        out_shape=jax.ShapeDtypeStruct((nbat * s, d), BF16),
        scratch_shapes=[pltpu.VMEM((nh, tq, V7X_LANES), F32), pltpu.VMEM((nh, tq, V7X_LANES), F32),
                        pltpu.VMEM((nh, tq, width), F32), pltpu.VMEM((nh, tq, V7X_LANES), F32),
                        pltpu.VMEM((nh, tq, tq), F32), pltpu.VMEM((nh, tq, tq), BF16)],
        compiler_params=_params(("parallel", "parallel", "arbitrary")),
        name="fox_attn",
    )(qkv, qkv, qkv, f_rows, f_cols)


def _fox_layer(xt, nbat, s, layer, w_qkvf, b_f, w_o, g, b):
    m, d = xt.shape
    bm = _row_block(m, ROW_TILE)
    qkv, logf = _qkvf(xt, w_qkvf, b_f, layer, bm)
    f_rows, f_t = _forget_cumsum(logf, nbat, s)
    nh = ATTN_HEADS_PER_STEP
    f_cols = jnp.pad(f_t[:, :FOX_HEADS, :].reshape(nbat, FOX_HEADS // nh, nh, s),
                     ((0, 0), (0, 0), (0, V7X_SUBLANES - nh), (0, 0)))
    o = _fox_attention(qkv, f_rows, f_cols, nbat, s, d, nh)
    return _mm_ln(o, w_o, layer, xt, g, b, bm)


PLAN_LEN, PLAN_OFF, PLAN_DST = 0, 1, 2
GATHER_LEN, GATHER_SRC, GATHER_DST = 0, 1, 2


def _to_slabs(slab_ref, value):
    rows = value.shape[0]
    for c in range(V7X_SUBLANES):
        slab_ref[pl.ds(c, rows, stride=V7X_SUBLANES), :] = value[:, c * V7X_LANES:(c + 1) * V7X_LANES]


def _from_slabs(slab_ref, rows):
    return jnp.concatenate([slab_ref[pl.ds(c, rows, stride=V7X_SUBLANES), :] for c in range(V7X_SUBLANES)], axis=1)


def _slab_rows(ref, row):
    return ref.at[pl.ds(pl.multiple_of(row * V7X_SUBLANES, V7X_SUBLANES), V7X_SUBLANES)]


def _run_copies(src_ref, dst_ref, src_row, dst_row, length, sem, max_len, wait):
    @pl.when(length > 0)
    def _():
        bit = max_len
        while bit >= 1:
            @pl.when((length & bit) != 0)
            def _(bit=bit):
                done = length & (-2 * bit)
                n = bit * V7X_SUBLANES
                s0 = pl.multiple_of((src_row + done) * V7X_SUBLANES, V7X_SUBLANES)
                d0 = pl.multiple_of((dst_row + done) * V7X_SUBLANES, V7X_SUBLANES)
                cp = pltpu.make_async_copy(src_ref.at[pl.ds(s0, n)], dst_ref.at[pl.ds(d0, n)], sem)
                if wait:
                    cp.wait()
                else:
                    cp.start()

            bit //= 2


def _router_kernel(x_ref, w_ref, q_ref, gt_ref, t_ref, srt_ref, xs_ref, qvm_ref, qsm_ref, sem, *, bm):
    x = x_ref[...]
    xh = x.astype(BF16)
    xl = (x - xh.astype(F32)).astype(BF16)
    w = w_ref[...]
    wh = w.astype(BF16)
    wl = (w - wh.astype(F32)).astype(BF16)
    hh_hl = _dot(xh, jnp.concatenate([wh, wl], axis=1))
    logits = (hh_hl[:, :V7X_LANES] + _dot(xl, wh)) + hh_hl[:, V7X_LANES:]
    lane = lax.broadcasted_iota(jnp.int32, logits.shape, 1)
    logits = jnp.where(lane < N_EXPERTS, logits, NEG_INF)
    m1 = jnp.max(logits, axis=1, keepdims=True)
    i1 = jnp.min(jnp.where(logits == m1, lane, V7X_LANES), axis=1, keepdims=True)
    rest = jnp.where(lane == i1, NEG_INF, logits)
    m2 = jnp.max(rest, axis=1, keepdims=True)
    i2 = jnp.min(jnp.where(rest == m2, lane, V7X_LANES), axis=1, keepdims=True)
    e21 = jnp.exp(m2 - m1)
    g1 = 1.0 / (1.0 + e21)
    g2 = e21 * g1
    oh1 = lane == i1
    oh2 = lane == i2
    hit = jnp.where(oh1 | oh2, 1.0, 0.0)
    row = lax.broadcasted_iota(jnp.int32, (bm, bm), 0)
    col = lax.broadcasted_iota(jnp.int32, (bm, bm), 1)
    strict = jnp.where(row > col, 1.0, 0.0).astype(BF16)
    local = _dot(strict, hit.astype(BF16))
    cnt8 = jnp.broadcast_to(local[bm - 1:bm, :] + hit[bm - 1:bm, :], (V7X_SUBLANES, V7X_LANES))
    lane8 = lax.broadcasted_iota(jnp.int32, cnt8.shape, 1)
    incl = cnt8
    shift = 1
    while shift < N_EXPERTS:
        incl = incl + jnp.where(lane8 >= shift, pltpu.roll(incl, shift, 1), 0.0)
        shift *= 2
    off8 = incl - cnt8
    place = local + off8[0:1, :]
    q1 = jnp.sum(jnp.where(oh1, place, 0.0), axis=1, keepdims=True)
    q2 = jnp.sum(jnp.where(oh2, place, 0.0), axis=1, keepdims=True)
    qvm_ref[...] = jnp.where(lane == 0, q1, jnp.where(lane == 1, q2, 0.0)).T[0:V7X_SUBLANES, :].astype(jnp.int32)
    to_smem = pltpu.make_async_copy(qvm_ref, qsm_ref, sem)
    to_smem.start()
    q_ref[...] = qvm_ref[...]
    gt_ref[...] = jnp.where(lane == 0, g1, jnp.where(lane == 1, g2, 0.0)).T[0:V7X_SUBLANES, :]
    row8 = lax.broadcasted_iota(jnp.int32, cnt8.shape, 0)
    t_ref[...] = jnp.where(row8 == PLAN_LEN, cnt8, jnp.where(row8 == PLAN_OFF, off8, 0.0)).astype(jnp.int32)
    _to_slabs(xs_ref, x)
    to_smem.wait()

    def place_rows(t, c):
        tile = xs_ref[pl.ds(pl.multiple_of(t * V7X_SUBLANES, V7X_SUBLANES), V7X_SUBLANES), :]
        for k in range(TOP_K):
            p0 = pl.multiple_of(qsm_ref[k, t] * V7X_SUBLANES, V7X_SUBLANES)
            srt_ref[pl.ds(p0, V7X_SUBLANES), :] = tile
        return c

    lax.fori_loop(0, bm, place_rows, 0, unroll=8)


def _router(xt, w_router, bm):
    m, d = xt.shape
    nsteps = m // bm
    w_pad = jnp.pad(w_router, ((0, 0), (0, V7X_LANES - w_router.shape[1])))
    srows = TOP_K * bm * V7X_SUBLANES
    return pl.pallas_call(
        functools.partial(_router_kernel, bm=bm),
        grid=(nsteps,),
        in_specs=[pl.BlockSpec((bm, d), lambda i: (i, 0)),
                  pl.BlockSpec((d, V7X_LANES), lambda i: (0, 0))],
        out_specs=[pl.BlockSpec((None, V7X_SUBLANES, bm), lambda i: (i, 0, 0)),
                   pl.BlockSpec((None, V7X_SUBLANES, bm), lambda i: (i, 0, 0)),
                   pl.BlockSpec((None, V7X_SUBLANES, V7X_LANES), lambda i: (i, 0, 0)),
                   pl.BlockSpec((srows, V7X_LANES), lambda i: (i, 0))],
        out_shape=[jax.ShapeDtypeStruct((nsteps, V7X_SUBLANES, bm), jnp.int32),
                   jax.ShapeDtypeStruct((nsteps, V7X_SUBLANES, bm), F32),
                   jax.ShapeDtypeStruct((nsteps, V7X_SUBLANES, V7X_LANES), jnp.int32),
                   jax.ShapeDtypeStruct((nsteps * srows, V7X_LANES), F32)],
        scratch_shapes=[pltpu.VMEM((bm * V7X_SUBLANES, V7X_LANES), F32),
                        pltpu.VMEM((V7X_SUBLANES, bm), jnp.int32),
                        pltpu.SMEM((V7X_SUBLANES, bm), jnp.int32),
                        pltpu.SemaphoreType.DMA],
        compiler_params=_params(("parallel",)),
        name="moe_router",
    )(xt, w_pad)


def _experts_kernel(ie_ref, iv_ref, ni_ref, gcur_ref, gnext_ref, rows_hbm, wg_ref, wu_ref, wd_ref, o_ref,
                    xin_ref, xb_ref, acc_ref, sem, *, sb, rb, ntiles, max_run):
    i = pl.program_id(0)
    j = pl.program_id(1)
    last = pl.num_programs(1) - 1
    valid = iv_ref[i]
    used = i < ni_ref[0]

    def pieces(gref, slot, wait):
        for t in range(ntiles):
            _run_copies(rows_hbm, xin_ref.at[slot], gref[GATHER_SRC, t], gref[GATHER_DST, t], gref[GATHER_LEN, t],
                        sem.at[slot], max_run, wait)

    @pl.when(jnp.logical_not(used) & (j == last))
    def _():
        o_ref[...] = jnp.zeros_like(o_ref)

    @pl.when(used)
    def _():
        for slot in range(2):
            @pl.when((j == 0) & (i % 2 == slot))
            def _(slot=slot):
                if slot == 0:
                    @pl.when(i == 0)
                    def _():
                        xin_ref[...] = jnp.zeros_like(xin_ref)
                        pieces(gcur_ref, 0, False)

                pieces(gcur_ref, slot, True)
                xb_ref[...] = _from_slabs(xin_ref.at[slot], sb).astype(BF16)
                acc_ref[...] = jnp.zeros_like(acc_ref)

                @pl.when(i + 1 < ni_ref[0])
                def _():
                    pieces(gnext_ref, 1 - slot, False)

        def run(nrows):
            xb = xb_ref[0:nrows, :]
            h = _silu_mul(_dot(xb, wg_ref[...].astype(BF16)), _dot(xb, wu_ref[...].astype(BF16))).astype(BF16)
            acc_ref[0:nrows, :] += _dot(h, wd_ref[...].astype(BF16))

        for nrows in range(rb, sb + rb, rb):
            @pl.when((valid > nrows - rb) & (valid <= nrows))
            def _(nrows=nrows):
                run(nrows)

        @pl.when(j == last)
        def _():
            _to_slabs(o_ref, acc_ref[...])


def _experts(rows, gather, w_gu, w_down, layer, item_expert, item_valid, n_items, sb, rb, fc, ntiles, max_run):
    assert sb % rb == 0
    d = w_down.shape[3]
    f = w_down.shape[2]
    nf = f // fc
    n_max = item_expert.shape[0]
    ssb = sb * V7X_SUBLANES
    smem = pltpu.SMEM

    def item(i, ni):
        return jnp.minimum(i, ni[0] - 1)

    def chunk(i, j, ni):
        return jnp.where(i < ni[0], j, nf - 1)

    grid_spec = pltpu.PrefetchScalarGridSpec(
        num_scalar_prefetch=3,
        grid=(n_max, nf),
        in_specs=[pl.BlockSpec((None, V7X_SUBLANES, V7X_LANES), lambda i, j, ie, iv, ni: (item(i, ni), 0, 0), memory_space=smem),
                  pl.BlockSpec((None, V7X_SUBLANES, V7X_LANES),
                               lambda i, j, ie, iv, ni: (jnp.minimum(item(i, ni) + 1, n_max - 1), 0, 0), memory_space=smem),
                  pl.BlockSpec(memory_space=pl.ANY),
                  pl.BlockSpec((None, None, d, fc), lambda i, j, ie, iv, ni: (layer, ie[item(i, ni)], 0, chunk(i, j, ni))),
                  pl.BlockSpec((None, None, d, fc), lambda i, j, ie, iv, ni: (layer, ie[item(i, ni)], 0, nf + chunk(i, j, ni))),
                  pl.BlockSpec((None, None, fc, d), lambda i, j, ie, iv, ni: (layer, ie[item(i, ni)], chunk(i, j, ni), 0))],
        out_specs=pl.BlockSpec((ssb, V7X_LANES), lambda i, j, ie, iv, ni: (i, 0)),
        scratch_shapes=[pltpu.VMEM((2, ssb, V7X_LANES), F32), pltpu.VMEM((sb, d), BF16), pltpu.VMEM((sb, d), F32),
                        pltpu.SemaphoreType.DMA((2,))],
    )
    return pl.pallas_call(
        functools.partial(_experts_kernel, sb=sb, rb=rb, ntiles=ntiles, max_run=max_run),
        grid_spec=grid_spec,
        out_shape=jax.ShapeDtypeStruct((n_max * ssb, V7X_LANES), F32),
        compiler_params=_params(("arbitrary", "arbitrary")),
        name="moe_experts",
    )(item_expert, item_valid, n_items, gather, gather, rows, w_gu, w_gu, w_down)


def _combine_ln_kernel(q_ref, gt_ref, tcur_ref, tnext_ref, x_ref, g_ref, b_ref, y_hbm, o_ref, ybuf, mix_ref, sem, *, bm):
    i = pl.program_id(0)
    nsteps = pl.num_programs(0)

    def runs(tref, slot, wait):
        for e in range(N_EXPERTS):
            _run_copies(y_hbm, ybuf.at[slot], tref[PLAN_DST, e], tref[PLAN_OFF, e], tref[PLAN_LEN, e],
                        sem.at[slot], bm, wait)

    for slot in range(2):
        @pl.when(i % 2 == slot)
        def _(slot=slot):
            if slot == 0:
                @pl.when(i == 0)
                def _():
                    runs(tcur_ref, 0, False)

            @pl.when(i + 1 < nsteps)
            def _():
                runs(tnext_ref, 1 - slot, False)

            runs(tcur_ref, slot, True)

            def mix(t, c):
                p1 = pl.multiple_of(q_ref[0, t] * V7X_SUBLANES, V7X_SUBLANES)
                p2 = pl.multiple_of(q_ref[1, t] * V7X_SUBLANES, V7X_SUBLANES)
                t0 = pl.multiple_of(t * V7X_SUBLANES, V7X_SUBLANES)
                mix_ref[pl.ds(t0, V7X_SUBLANES), :] = (gt_ref[0, t] * ybuf[slot, pl.ds(p1, V7X_SUBLANES), :]
                                                       + gt_ref[1, t] * ybuf[slot, pl.ds(p2, V7X_SUBLANES), :])
                return c

            lax.fori_loop(0, bm, mix, 0, unroll=8)
            o_ref[...] = _layer_norm(ALPHA * x_ref[...] + _from_slabs(mix_ref, bm), g_ref[...], b_ref[...])


def _combine_ln(xt, plan_q, plan_g, plan_t, yrows, g, b, bm):
    m, d = xt.shape
    nsteps = m // bm
    smem = pltpu.SMEM
    return pl.pallas_call(
        functools.partial(_combine_ln_kernel, bm=bm),
        grid=(nsteps,),
        in_specs=[pl.BlockSpec((None, V7X_SUBLANES, bm), lambda i: (i, 0, 0), memory_space=smem),
                  pl.BlockSpec((None, V7X_SUBLANES, bm), lambda i: (i, 0, 0), memory_space=smem),
                  pl.BlockSpec((None, V7X_SUBLANES, V7X_LANES), lambda i: (i, 0, 0), memory_space=smem),
                  pl.BlockSpec((None, V7X_SUBLANES, V7X_LANES), lambda i: (jnp.minimum(i + 1, nsteps - 1), 0, 0),
                               memory_space=smem),
                  pl.BlockSpec((bm, d), lambda i: (i, 0)),
                  pl.BlockSpec((1, d), lambda i: (0, 0)),
                  pl.BlockSpec((1, d), lambda i: (0, 0)),
                  pl.BlockSpec(memory_space=pl.ANY)],
        out_specs=pl.BlockSpec((bm, d), lambda i: (i, 0)),
        out_shape=jax.ShapeDtypeStruct((m, d), F32),
        scratch_shapes=[pltpu.VMEM((2, TOP_K * bm * V7X_SUBLANES, V7X_LANES), F32),
                        pltpu.VMEM((bm * V7X_SUBLANES, V7X_LANES), F32),
                        pltpu.SemaphoreType.DMA((2,))],
        compiler_params=_params(("arbitrary",)),
        name="moe_combine_ln",
    )(plan_q, plan_g, plan_t, plan_t, xt, g.reshape(1, d), b.reshape(1, d), yrows)


def _moe_layer(xt, layer, w_router, w_gu, w_down, g, b):
    m, d = xt.shape
    sb = _row_block(m, MOE_SUPER_ROWS)
    rb = _row_block(sb, MOE_ROW_BLOCK)
    bm = _row_block(m, ROUTE_ROWS)
    ntiles = m // bm
    assert ntiles <= V7X_LANES
    plan_q, plan_g, plan_t, rows = _router(xt, w_router[layer], bm)
    lens = plan_t[:, PLAN_LEN, :N_EXPERTS]
    offs = plan_t[:, PLAN_OFF, :N_EXPERTS]
    counts = jnp.sum(lens, axis=0)
    nsb = (counts + sb - 1) // sb
    iend = jnp.cumsum(nsb)
    n_max = (m * TOP_K) // sb + N_EXPERTS
    starts = ((iend - nsb) * sb).astype(jnp.int32)
    run_lo = jnp.cumsum(lens, axis=0) - lens
    plan_t = plan_t.at[:, PLAN_DST, :N_EXPERTS].set(starts[None, :] + run_lo)
    item = jnp.arange(n_max, dtype=jnp.int32)
    item_expert = jnp.minimum(jnp.sum(item[:, None] >= iend[None, :], axis=1), N_EXPERTS - 1).astype(jnp.int32)
    item_lo = item * sb - starts[item_expert]
    item_valid = jnp.clip(counts[item_expert] - item_lo, 0, sb).astype(jnp.int32)
    n_items = iend[-1:].astype(jnp.int32)
    r_lo = run_lo[:, item_expert].T
    r_len = lens[:, item_expert].T
    lo = jnp.maximum(item_lo[:, None], r_lo)
    hi = jnp.minimum(item_lo[:, None] + sb, r_lo + r_len)
    tile_row0 = (jnp.arange(ntiles, dtype=jnp.int32) * (TOP_K * bm))[None, :]
    gather = jnp.zeros((n_max, V7X_SUBLANES, V7X_LANES), jnp.int32)
    gather = gather.at[:, GATHER_LEN, :ntiles].set(jnp.maximum(hi - lo, 0))
    gather = gather.at[:, GATHER_SRC, :ntiles].set(tile_row0 + offs[:, item_expert].T + (lo - r_lo))
    gather = gather.at[:, GATHER_DST, :ntiles].set(lo - item_lo[:, None])
    yrows = _experts(rows, gather, w_gu, w_down, layer, item_expert, item_valid, n_items, sb, rb, MOE_FF_CHUNK,
                     ntiles, min(bm, sb))
    return _combine_ln(xt, plan_q, plan_g, plan_t, yrows, g, b, bm)


def kernel(x, lru_w_in, lru_conv_w, lru_conv_b, lru_w_gates, lru_b_gates, lru_lambda, lru_w_out, pool_w, pool_scale,
           fox_w_qkvf, fox_b_f, fox_w_o, ffn_w_gu, ffn_w_down, moe_router, moe_w_gu, moe_w_down,
           ln_mix_g, ln_mix_b, ln_ffn_g, ln_ffn_b):
    nbat, s, d = x.shape
    xt = x.reshape(nbat * s, d)
    for i in range(DEPTH):
        mixer, j = i % 3, i // 3
        if mixer == 0:
            xt = _rglru_layer(xt, nbat, s, j, lru_w_in, lru_conv_w, lru_conv_b, lru_w_gates, lru_b_gates,
                              lru_lambda, lru_w_out, ln_mix_g[i], ln_mix_b[i])
        elif mixer == 1:
            xt = _pool_layer(xt, nbat, s, j, pool_w, pool_scale, ln_mix_g[i], ln_mix_b[i])
        else:
            xt = _fox_layer(xt, nbat, s, j, fox_w_qkvf, fox_b_f, fox_w_o, ln_mix_g[i], ln_mix_b[i])
        if i % 2 == 0:
            xt = _ffn_ln(xt, ffn_w_gu, ffn_w_down, i // 2, ln_ffn_g[i], ln_ffn_b[i], _row_block(nbat * s, ROW_TILE), FFN_CHUNK)
        else:
            xt = _moe_layer(xt, i // 2, moe_router, moe_w_gu, moe_w_down, ln_ffn_g[i], ln_ffn_b[i])
    return xt.reshape(nbat, s, d)
```

```python
import functools

import jax
import jax.numpy as jnp
from jax import lax
from jax.experimental import pallas as pl
from jax.experimental.pallas import tpu as pltpu

F32 = jnp.float32
BF16 = jnp.bfloat16

DEPTH = 4
LRU_BLOCKS = 8
CONV_WIDTH = 4
LRU_C = 8.0
POOL_WINDOWS = (2, 4, 8, 16)
FOX_HEADS = 16
N_EXPERTS = 8
TOP_K = 2
LN_EPS = 1e-5
NEG_INF = -1e30
ALPHA = (2 * DEPTH) ** 0.25

V7X_LANES = 128
V7X_SUBLANES = 8
V7X_VMEM_LIMIT_BYTES = 58 * 1024 * 1024

ROW_TILE = 1024
FFN_CHUNK = 512
RGLRU_TIME_TILE = 64
POOL_ROWS = 512
POOL_HALO = 32
ATTN_TILE = 512
ATTN_HEADS_PER_STEP = 2
ATTN_STRIP = 64
CUMSUM_ROWS = 512
ROUTE_ROWS = 1024
MOE_SUPER_ROWS = 1024
MOE_ROW_BLOCK = 256
MOE_FF_CHUNK = 896


def _params(semantics, vmem=V7X_VMEM_LIMIT_BYTES):
    return pltpu.CompilerParams(dimension_semantics=semantics, vmem_limit_bytes=vmem)


def _row_block(m, want):
    return want if m % want == 0 else m


def _layer_norm(y, g, b):
    mu = jnp.mean(y, axis=-1, keepdims=True)
    yc = y - mu
    var = jnp.mean(yc * yc, axis=-1, keepdims=True)
    return yc * lax.rsqrt(var + LN_EPS) * g + b


def _dot(a, b):
    return jnp.dot(a, b, preferred_element_type=F32)


def _softplus(x):
    return jnp.maximum(x, 0.0) + jnp.log1p(jnp.exp(-jnp.abs(x)))


def _log_sigmoid(z):
    return -_softplus(-z)


def _gelu_tanh(x):
    return 0.5 * x * (1.0 + jnp.tanh(0.7978845608028654 * (x + 0.044715 * (x * x * x))))


def _silu_mul(g, u):
    return g * jax.nn.sigmoid(g) * u


def _mm_ln_kernel(a_ref, w_ref, x_ref, g_ref, b_ref, o_ref):
    y = _dot(a_ref[...].astype(BF16), w_ref[...].astype(BF16))
    o_ref[...] = _layer_norm(ALPHA * x_ref[...] + y, g_ref[...], b_ref[...])


def _mm_ln(a, w_stack, layer, x, g, b, bm):
    m, k = a.shape
    d = w_stack.shape[2]
    return pl.pallas_call(
        _mm_ln_kernel,
        grid=(m // bm,),
        in_specs=[pl.BlockSpec((bm, k), lambda i: (i, 0)),
                  pl.BlockSpec((None, k, d), lambda i: (layer, 0, 0)),
                  pl.BlockSpec((bm, d), lambda i: (i, 0)),
                  pl.BlockSpec((1, d), lambda i: (0, 0)),
                  pl.BlockSpec((1, d), lambda i: (0, 0))],
        out_specs=pl.BlockSpec((bm, d), lambda i: (i, 0)),
        out_shape=jax.ShapeDtypeStruct((m, d), F32),
        compiler_params=_params(("parallel",)),
        name="mm_ln",
    )(a, w_stack, x, g.reshape(1, d), b.reshape(1, d))


def _ffn_ln_kernel(x_ref, wg_ref, wu_ref, wd_ref, g_ref, b_ref, o_ref, xb_ref, acc_ref):
    j = pl.program_id(1)

    @pl.when(j == 0)
    def _():
        xb_ref[...] = x_ref[...].astype(BF16)
        acc_ref[...] = jnp.zeros_like(acc_ref)

    xb = xb_ref[...]
    h = _silu_mul(_dot(xb, wg_ref[...].astype(BF16)), _dot(xb, wu_ref[...].astype(BF16))).astype(BF16)
    acc_ref[...] += _dot(h, wd_ref[...].astype(BF16))

    @pl.when(j == pl.num_programs(1) - 1)
    def _():
        o_ref[...] = _layer_norm(ALPHA * x_ref[...] + acc_ref[...], g_ref[...], b_ref[...])


def _ffn_ln(x, w_gu_stack, w_down_stack, layer, g, b, bm, fc):
    m, d = x.shape
    f = w_down_stack.shape[1]
    nf = f // fc
    return pl.pallas_call(
        _ffn_ln_kernel,
        grid=(m // bm, nf),
        in_specs=[pl.BlockSpec((bm, d), lambda i, j: (i, 0)),
                  pl.BlockSpec((None, d, fc), lambda i, j: (layer, 0, j)),
                  pl.BlockSpec((None, d, fc), lambda i, j: (layer, 0, nf + j)),
                  pl.BlockSpec((None, fc, d), lambda i, j: (layer, j, 0)),
                  pl.BlockSpec((1, d), lambda i, j: (0, 0)),
                  pl.BlockSpec((1, d), lambda i, j: (0, 0))],
        out_specs=pl.BlockSpec((bm, d), lambda i, j: (i, 0)),
        out_shape=jax.ShapeDtypeStruct((m, d), F32),
        scratch_shapes=[pltpu.VMEM((bm, d), BF16), pltpu.VMEM((bm, d), F32)],
        compiler_params=_params(("parallel", "arbitrary")),
        name="ffn_ln",
    )(x, w_gu_stack, w_gu_stack, w_down_stack, g.reshape(1, d), b.reshape(1, d))


def _rglru_kernel(x_ref, win_ref, cw_ref, cb_ref, wg_ref, bg_ref, lam_ref, wout_ref, g_ref, bt_ref, o_ref,
                  xs_ref, tm_ref, a_ref, b_ref, hs_ref, h_ref, *, ts):
    nb = LRU_BLOCKS
    halo = (CONV_WIDTH - 1) * V7X_SUBLANES
    rows = ts * V7X_SUBLANES
    step = pl.program_id(0)

    @pl.when(step == 0)
    def _():
        h_ref[...] = jnp.zeros_like(h_ref)
        tm_ref[:, 0:halo, :] = jnp.zeros((2 * nb, halo, V7X_LANES), F32)

    @pl.when(step > 0)
    def _():
        tm_ref[:, 0:halo, :] = tm_ref[:, rows:rows + halo, :]

    for bi in range(V7X_SUBLANES):
        for c in range(nb):
            xs_ref[c, pl.ds(bi, ts, stride=V7X_SUBLANES), :] = x_ref[bi, :, c * V7X_LANES:(c + 1) * V7X_LANES]
    x_tm = jnp.concatenate([xs_ref[c] for c in range(nb)], axis=1)
    xb = x_tm.astype(BF16)
    for c2 in range(nb):
        u2 = _dot(xb, win_ref[:, 2 * c2 * V7X_LANES:(2 * c2 + 2) * V7X_LANES])
        tm_ref[2 * c2, halo:halo + rows, :] = u2[:, :V7X_LANES]
        tm_ref[2 * c2 + 1, halo:halo + rows, :] = u2[:, V7X_LANES:]

    for n in range(nb):
        sl = slice(n * V7X_LANES, (n + 1) * V7X_LANES)
        xr = cb_ref[:, sl]
        for k in range(CONV_WIDTH):
            xr = xr + tm_ref[nb + n, k * V7X_SUBLANES:k * V7X_SUBLANES + rows, :] * cw_ref[k:k + 1, sl]
        z = _dot(xr.astype(BF16), wg_ref[n].astype(BF16)) + bg_ref[n]
        r = jax.nn.sigmoid(z[:, :V7X_LANES])
        i = jax.nn.sigmoid(z[:, V7X_LANES:])
        log_a = (-LRU_C * r) * _softplus(-lam_ref[:, sl])
        a = jnp.exp(log_a)
        a_ref[n] = a
        b_ref[n] = jnp.sqrt(-jnp.tanh(log_a) * (a * a + 1.0)) * (i * xr)

    def scan_step(t, hs):
        r0 = pl.multiple_of(t * V7X_SUBLANES, V7X_SUBLANES)
        new = []
        for n in range(nb):
            h = a_ref[n, pl.ds(r0, V7X_SUBLANES), :] * hs[n] + b_ref[n, pl.ds(r0, V7X_SUBLANES), :]
            hs_ref[n, pl.ds(r0, V7X_SUBLANES), :] = h
            new.append(h)
        return tuple(new)

    hs = lax.fori_loop(0, ts, scan_step, tuple(h_ref[n] for n in range(nb)), unroll=8)
    for n in range(nb):
        h_ref[n] = hs[n]

    for n in range(nb):
        a_ref[n] = _gelu_tanh(tm_ref[n, halo:halo + rows, :]) * hs_ref[n]
    gh = jnp.concatenate([a_ref[n] for n in range(nb)], axis=1).astype(BF16)
    y = _layer_norm(ALPHA * x_tm + _dot(gh, wout_ref[...]), g_ref[...], bt_ref[...])
    for c in range(nb):
        b_ref[c] = y[:, c * V7X_LANES:(c + 1) * V7X_LANES]
    for bi in range(V7X_SUBLANES):
        for c in range(nb):
            o_ref[bi, :, c * V7X_LANES:(c + 1) * V7X_LANES] = b_ref[c, pl.ds(bi, ts, stride=V7X_SUBLANES), :]


def _rglru_layer(xt, nbat, s, layer, w_in, conv_w, conv_b, w_gates, b_gates, lam, w_out, g, b):
    m, d = xt.shape
    ts = _row_block(s, RGLRU_TIME_TILE)
    assert nbat == V7X_SUBLANES and d == LRU_BLOCKS * V7X_LANES
    rows = ts * V7X_SUBLANES
    halo = (CONV_WIDTH - 1) * V7X_SUBLANES
    gw = 2 * V7X_LANES
    slab = pltpu.VMEM((LRU_BLOCKS, rows, V7X_LANES), F32)
    out = pl.pallas_call(
        functools.partial(_rglru_kernel, ts=ts),
        grid=(s // ts,),
        in_specs=[pl.BlockSpec((nbat, ts, d), lambda t: (0, t, 0)),
                  pl.BlockSpec((d, 2 * d), lambda t: (0, 0)),
                  pl.BlockSpec((CONV_WIDTH, d), lambda t: (0, 0)),
                  pl.BlockSpec((1, d), lambda t: (0, 0)),
                  pl.BlockSpec((None, LRU_BLOCKS, V7X_LANES, gw), lambda t: (layer, 0, 0, 0)),
                  pl.BlockSpec((LRU_BLOCKS, 1, gw), lambda t: (0, 0, 0)),
                  pl.BlockSpec((1, d), lambda t: (0, 0)),
                  pl.BlockSpec((d, d), lambda t: (0, 0)),
                  pl.BlockSpec((1, d), lambda t: (0, 0)),
                  pl.BlockSpec((1, d), lambda t: (0, 0))],
        out_specs=pl.BlockSpec((nbat, ts, d), lambda t: (0, t, 0)),
        out_shape=jax.ShapeDtypeStruct((nbat, s, d), F32),
        scratch_shapes=[slab,
                        pltpu.VMEM((2 * LRU_BLOCKS, halo + rows, V7X_LANES), F32),
                        slab, slab, slab,
                        pltpu.VMEM((LRU_BLOCKS, V7X_SUBLANES, V7X_LANES), F32)],
        compiler_params=_params(("arbitrary",)),
        name="rglru",
    )(xt.reshape(nbat, s, d), w_in[layer].astype(BF16), conv_w[layer], conv_b[layer].reshape(1, d), w_gates,
      b_gates[layer].reshape(LRU_BLOCKS, 1, gw), lam[layer].reshape(1, d), w_out[layer].astype(BF16),
      g.reshape(1, d), b.reshape(1, d))
    return out.reshape(m, d)


def _pool_ln_kernel(x_ref, prev_ref, w_ref, sc_ref, g_ref, b_ref, o_ref, e_ref, st_ref, y_ref, *, rc, tiles_per_seq):
    i = pl.program_id(0)
    first = (i % tiles_per_seq) == 0
    d = x_ref.shape[1]
    gw = d // len(POOL_WINDOWS)
    e_ref[0:POOL_HALO, :] = jnp.where(first, 0.0, prev_ref[...])
    e_ref[POOL_HALO:, :] = x_ref[...]
    t = ((i % tiles_per_seq) * rc + lax.broadcasted_iota(jnp.int32, (rc, 1), 0) + 1).astype(F32)
    for gi, wl in enumerate(POOL_WINDOWS):
        sl = slice(gi * gw, (gi + 1) * gw)
        lo = V7X_SUBLANES
        shift = 1
        cur = e_ref[lo:, sl] + e_ref[lo - shift:POOL_HALO + rc - shift, sl]
        while 2 * shift < wl:
            shift *= 2
            st_ref[lo:, :] = cur
            nlo = lo + V7X_SUBLANES
            cur = st_ref[nlo:, :] + st_ref[nlo - shift:POOL_HALO + rc - shift, :]
            lo = nlo
        win = cur[POOL_HALO - lo:, :]
        xg = x_ref[:, sl]
        p = win / jnp.minimum(t, float(wl)) - xg
        y_ref[:, sl] = _dot(p.astype(BF16), w_ref[gi].astype(BF16))
    y = y_ref[...] * sc_ref[...]
    o_ref[...] = _layer_norm(ALPHA * x_ref[...] + y, g_ref[...], b_ref[...])


def _pool_layer(xt, nbat, s, layer, pool_w, pool_scale, g, b):
    m, d = xt.shape
    rc = _row_block(s, POOL_ROWS)
    tiles_per_seq = s // rc
    ng = len(POOL_WINDOWS)
    gw = d // ng
    hb = rc // POOL_HALO
    return pl.pallas_call(
        functools.partial(_pool_ln_kernel, rc=rc, tiles_per_seq=tiles_per_seq),
        grid=(m // rc,),
        in_specs=[pl.BlockSpec((rc, d), lambda i: (i, 0)),
                  pl.BlockSpec((POOL_HALO, d), lambda i: (jnp.maximum(i * hb - 1, 0), 0)),
                  pl.BlockSpec((None, ng, gw, gw), lambda i: (layer, 0, 0, 0)),
                  pl.BlockSpec((1, d), lambda i: (0, 0)),
                  pl.BlockSpec((1, d), lambda i: (0, 0)),
                  pl.BlockSpec((1, d), lambda i: (0, 0))],
        out_specs=pl.BlockSpec((rc, d), lambda i: (i, 0)),
        out_shape=jax.ShapeDtypeStruct((m, d), F32),
        scratch_shapes=[pltpu.VMEM((POOL_HALO + rc, d), F32),
                        pltpu.VMEM((POOL_HALO + rc, gw), F32),
                        pltpu.VMEM((rc, d), F32)],
        compiler_params=_params(("parallel",)),
        name="pool_ln",
    )(xt, xt, pool_w, pool_scale[layer].reshape(1, d), g.reshape(1, d), b.reshape(1, d))


def _qkvf_kernel(x_ref, w_ref, wf_ref, bf_ref, o_ref, lf_ref):
    xb = x_ref[...].astype(BF16)
    o_ref[...] = _dot(xb, w_ref[...]).astype(o_ref.dtype)
    lf_ref[...] = _log_sigmoid(_dot(xb, wf_ref[...]) + bf_ref[...])


def _qkvf(xt, w_qkvf, b_f, layer, bm):
    m, d = xt.shape
    h = b_f.shape[1]
    w_qkv = w_qkvf[layer][:, :3 * d].astype(BF16)
    w_pad = jnp.pad(w_qkvf[layer][:, 3 * d:], ((0, 0), (0, V7X_LANES - h))).astype(BF16)
    b_pad = jnp.pad(b_f[layer], (0, V7X_LANES - h)).reshape(1, V7X_LANES)
    return pl.pallas_call(
        _qkvf_kernel,
        grid=(m // bm,),
        in_specs=[pl.BlockSpec((bm, d), lambda i: (i, 0)),
                  pl.BlockSpec((d, 3 * d), lambda i: (0, 0)),
                  pl.BlockSpec((d, V7X_LANES), lambda i: (0, 0)),
                  pl.BlockSpec((1, V7X_LANES), lambda i: (0, 0))],
        out_specs=[pl.BlockSpec((bm, 3 * d), lambda i: (i, 0)),
                   pl.BlockSpec((bm, V7X_LANES), lambda i: (i, 0))],
        out_shape=[jax.ShapeDtypeStruct((m, 3 * d), BF16),
                   jax.ShapeDtypeStruct((m, V7X_LANES), F32)],
        compiler_params=_params(("parallel",)),
        name="fox_qkvf",
    )(xt, w_qkv, w_pad, b_pad)


def _split3(v):
    hi = v.astype(BF16)
    r1 = v - hi.astype(F32)
    mid = r1.astype(BF16)
    lo = (r1 - mid.astype(F32)).astype(BF16)
    return hi, mid, lo


def _cumsum_kernel(lf_ref, f_ref, ft_ref, carry_ref):
    j = pl.program_id(1)

    @pl.when(j == 0)
    def _():
        carry_ref[...] = jnp.zeros_like(carry_ref)

    n = lf_ref.shape[0]
    row = lax.broadcasted_iota(jnp.int32, (n, n), 0)
    col = lax.broadcasted_iota(jnp.int32, (n, n), 1)
    tri = jnp.where(row >= col, 1.0, 0.0).astype(BF16)
    hi, mid, lo = _split3(lf_ref[...])
    cs = (_dot(tri, hi) + _dot(tri, mid) + _dot(tri, lo)) + carry_ref[0:1, :]
    f_ref[...] = cs
    ft_ref[...] = cs.T
    carry_ref[...] = jnp.broadcast_to(cs[n - 1:n, :], carry_ref.shape)


def _forget_cumsum(logf, nbat, s):
    n = _row_block(s, CUMSUM_ROWS)
    nj = s // n
    return pl.pallas_call(
        _cumsum_kernel,
        grid=(nbat, nj),
        in_specs=[pl.BlockSpec((n, V7X_LANES), lambda bi, j: (bi * nj + j, 0))],
        out_specs=[pl.BlockSpec((n, V7X_LANES), lambda bi, j: (bi * nj + j, 0)),
                   pl.BlockSpec((None, V7X_LANES, n), lambda bi, j: (bi, 0, j))],
        out_shape=[jax.ShapeDtypeStruct((nbat * s, V7X_LANES), F32),
                   jax.ShapeDtypeStruct((nbat, V7X_LANES, s), F32)],
        scratch_shapes=[pltpu.VMEM((V7X_SUBLANES, V7X_LANES), F32)],
        compiler_params=_params(("parallel", "arbitrary")),
        name="fox_cumsum",
    )(logf)


def _fox_attn_kernel(q_ref, k_ref, v_ref, fq_ref, fk_ref, o_ref, m_ref, l_ref, acc_ref, fq_s, s_ref, p_ref,
                     *, tq, dh, nh):
    grp = pl.program_id(1)
    width = nh * dh
    wreps = width // V7X_LANES
    lane = lax.broadcasted_iota(jnp.int32, (tq, width), 1)
    lane_f = lax.broadcasted_iota(jnp.int32, (tq, V7X_LANES), 1)
    sr = ATTN_STRIP
    for qi in range(q_ref.shape[0] // tq):
        _fox_attn_tile(qi, grp, lane, lane_f, wreps, sr, q_ref, k_ref, v_ref, fq_ref, fk_ref, o_ref,
                       m_ref, l_ref, acc_ref, fq_s, s_ref, p_ref, tq=tq, dh=dh, nh=nh)


def _fox_attn_tile(qi, grp, lane, lane_f, wreps, sr, q_ref, k_ref, v_ref, fq_ref, fk_ref, o_ref,
                   m_ref, l_ref, acc_ref, fq_s, s_ref, p_ref, *, tq, dh, nh):
    width = nh * dh
    qrows = slice(qi * tq, (qi + 1) * tq)
    q = q_ref[qrows, :] * jnp.asarray(dh ** -0.5, q_ref.dtype)
    fq_all = fq_ref[qrows, :]
    qh = []
    for hh in range(nh):
        in_head = (lane >= hh * dh) & (lane < (hh + 1) * dh)
        qh.append(jnp.where(in_head, q, jnp.zeros_like(q)))
        fq_col = jnp.sum(jnp.where(lane_f == nh * grp + hh, fq_all, 0.0), axis=1, keepdims=True)
        fq_s[hh] = jnp.broadcast_to(fq_col, (tq, V7X_LANES))
    m_ref[...] = jnp.full(m_ref.shape, NEG_INF, F32)
    l_ref[...] = jnp.zeros_like(l_ref)
    acc_ref[...] = jnp.zeros_like(acc_ref)

    def kv_step(j, diagonal):
        r0 = j * tq
        k = k_ref[r0:r0 + tq, :]
        v = v_ref[r0:r0 + tq, :]
        for hh in range(nh):
            s_ref[hh] = lax.dot_general(qh[hh], k, (((1,), (1,)), ((), ())), preferred_element_type=F32)
        fks = [fk_ref[hh:hh + 1, r0:r0 + tq] for hh in range(nh)]

        def strip(hh, rows, a, cw):
            t = s_ref[hh, rows, 0:cw] - fks[hh][:, 0:cw]
            if diagonal:
                row = a + lax.broadcasted_iota(jnp.int32, (sr, cw), 0)
                col = lax.broadcasted_iota(jnp.int32, (sr, cw), 1)
                t = jnp.where(col <= row, t, NEG_INF)
            m_prev = m_ref[hh, rows, :]
            fq = fq_s[hh, rows, :]
            m_new = jnp.maximum(m_prev, fq + jnp.max(t, axis=1, keepdims=True))
            alpha = jnp.exp(m_prev - m_new)
            c = fq - m_new
            pr = jnp.exp(t + jnp.concatenate([c] * (cw // V7X_LANES), axis=1))
            l_ref[hh, rows, :] = alpha * l_ref[hh, rows, :] + jnp.sum(pr, axis=1, keepdims=True)
            acc_ref[hh, rows, :] = jnp.concatenate([alpha] * wreps, axis=1) * acc_ref[hh, rows, :]
            m_ref[hh, rows, :] = m_new
            p_ref[hh, rows, 0:cw] = pr.astype(BF16)
            if cw < tq:
                p_ref[hh, rows, cw:tq] = jnp.zeros((sr, tq - cw), BF16)

        for a in range(0, tq, sr):
            cw = min(tq, -(-(a + sr) // V7X_LANES) * V7X_LANES) if diagonal else tq
            for hh in range(nh):
                strip(hh, slice(a, a + sr), a, cw)
        for hh in range(nh):
            acc_ref[hh] += _dot(p_ref[hh], v)

    for j in range(qi):
        kv_step(j, False)
    kv_step(qi, True)
    out = jnp.zeros((tq, width), F32)
    for hh in range(nh):
        in_head = (lane >= hh * dh) & (lane < (hh + 1) * dh)
        out = jnp.where(in_head, acc_ref[hh] / jnp.concatenate([l_ref[hh]] * wreps, axis=1), out)
    o_ref[qrows, :] = out.astype(o_ref.dtype)


def _fox_attention(qkv, f_rows, f_cols, nbat, s, d, nh):
    dh = d // FOX_HEADS
    ngrp = FOX_HEADS // nh
    width = nh * dh
    assert width % V7X_LANES == 0 and nh <= V7X_SUBLANES
    tq = _row_block(s, ATTN_TILE)
    return pl.pallas_call(
        functools.partial(_fox_attn_kernel, tq=tq, dh=dh, nh=nh),
        grid=(nbat, ngrp),
        in_specs=[pl.BlockSpec((s, width), lambda bi, p: (bi, p)),
                  pl.BlockSpec((s, width), lambda bi, p: (bi, ngrp + p)),
                  pl.BlockSpec((s, width), lambda bi, p: (bi, 2 * ngrp + p)),
                  pl.BlockSpec((s, V7X_LANES), lambda bi, p: (bi, 0)),
                  pl.BlockSpec((None, None, V7X_SUBLANES, s), lambda bi, p: (bi, p, 0, 0))],
        out_specs=pl.BlockSpec((s, width), lambda bi, p: (bi, p)),
        out_shape=jax.ShapeDtypeStruct((nbat * s, d), BF16),
        scratch_shapes=[pltpu.VMEM((nh, tq, V7X_LANES), F32), pltpu.VMEM((nh, tq, V7X_LANES), F32),
                        pltpu.VMEM((nh, tq, width), F32), pltpu.VMEM((nh, tq, V7X_LANES), F32),
                        pltpu.VMEM((nh, tq, tq), F32), pltpu.VMEM((nh, tq, tq), BF16)],
        compiler_params=_params(("parallel", "parallel")),
        name="fox_attn",
    )(qkv, qkv, qkv, f_rows, f_cols)


def _fox_layer(xt, nbat, s, layer, w_qkvf, b_f, w_o, g, b):
    m, d = xt.shape
    bm = _row_block(m, ROW_TILE)
    qkv, logf = _qkvf(xt, w_qkvf, b_f, layer, bm)
    f_rows, f_t = _forget_cumsum(logf, nbat, s)
    nh = ATTN_HEADS_PER_STEP
    f_cols = jnp.pad(f_t[:, :FOX_HEADS, :].reshape(nbat, FOX_HEADS // nh, nh, s),
                     ((0, 0), (0, 0), (0, V7X_SUBLANES - nh), (0, 0)))
    o = _fox_attention(qkv, f_rows, f_cols, nbat, s, d, nh)
    return _mm_ln(o, w_o, layer, xt, g, b, bm)


PLAN_LEN, PLAN_OFF, PLAN_DST = 0, 1, 2
GATHER_LEN, GATHER_SRC, GATHER_DST = 0, 1, 2


def _to_slabs(slab_ref, value):
    rows = value.shape[0]
    for c in range(V7X_SUBLANES):
        slab_ref[pl.ds(c, rows, stride=V7X_SUBLANES), :] = value[:, c * V7X_LANES:(c + 1) * V7X_LANES]


def _from_slabs(slab_ref, rows):
    return jnp.concatenate([slab_ref[pl.ds(c, rows, stride=V7X_SUBLANES), :] for c in range(V7X_SUBLANES)], axis=1)


def _slab_rows(ref, row):
    return ref.at[pl.ds(pl.multiple_of(row * V7X_SUBLANES, V7X_SUBLANES), V7X_SUBLANES)]


def _run_copies(src_ref, dst_ref, src_row, dst_row, length, sem, max_len, wait):
    @pl.when(length > 0)
    def _():
        bit = max_len
        while bit >= 1:
            @pl.when((length & bit) != 0)
            def _(bit=bit):
                done = length & (-2 * bit)
                n = bit * V7X_SUBLANES
                s0 = pl.multiple_of((src_row + done) * V7X_SUBLANES, V7X_SUBLANES)
                d0 = pl.multiple_of((dst_row + done) * V7X_SUBLANES, V7X_SUBLANES)
                cp = pltpu.make_async_copy(src_ref.at[pl.ds(s0, n)], dst_ref.at[pl.ds(d0, n)], sem)
                if wait:
                    cp.wait()
                else:
                    cp.start()

            bit //= 2


def _router_kernel(x_ref, w_ref, q_ref, gt_ref, t_ref, srt_ref, xs_ref, qvm_ref, qsm_ref, sem, *, bm):
    x = x_ref[...]
    xh = x.astype(BF16)
    xl = (x - xh.astype(F32)).astype(BF16)
    w = w_ref[...]
    wh = w.astype(BF16)
    wl = (w - wh.astype(F32)).astype(BF16)
    hh_hl = _dot(xh, jnp.concatenate([wh, wl], axis=1))
    logits = (hh_hl[:, :V7X_LANES] + _dot(xl, wh)) + hh_hl[:, V7X_LANES:]
    lane = lax.broadcasted_iota(jnp.int32, logits.shape, 1)
    logits = jnp.where(lane < N_EXPERTS, logits, NEG_INF)
    m1 = jnp.max(logits, axis=1, keepdims=True)
    i1 = jnp.min(jnp.where(logits == m1, lane, V7X_LANES), axis=1, keepdims=True)
    rest = jnp.where(lane == i1, NEG_INF, logits)
    m2 = jnp.max(rest, axis=1, keepdims=True)
    i2 = jnp.min(jnp.where(rest == m2, lane, V7X_LANES), axis=1, keepdims=True)
    e21 = jnp.exp(m2 - m1)
    g1 = 1.0 / (1.0 + e21)
    g2 = e21 * g1
    oh1 = lane == i1
    oh2 = lane == i2
    hit = jnp.where(oh1 | oh2, 1.0, 0.0)
    row = lax.broadcasted_iota(jnp.int32, (bm, bm), 0)
    col = lax.broadcasted_iota(jnp.int32, (bm, bm), 1)
    strict = jnp.where(row > col, 1.0, 0.0).astype(BF16)
    local = _dot(strict, hit.astype(BF16))
    cnt8 = jnp.broadcast_to(local[bm - 1:bm, :] + hit[bm - 1:bm, :], (V7X_SUBLANES, V7X_LANES))
    lane8 = lax.broadcasted_iota(jnp.int32, cnt8.shape, 1)
    incl = cnt8
    shift = 1
    while shift < N_EXPERTS:
        incl = incl + jnp.where(lane8 >= shift, pltpu.roll(incl, shift, 1), 0.0)
        shift *= 2
    off8 = incl - cnt8
    place = local + off8[0:1, :]
    q1 = jnp.sum(jnp.where(oh1, place, 0.0), axis=1, keepdims=True)
    q2 = jnp.sum(jnp.where(oh2, place, 0.0), axis=1, keepdims=True)
    qvm_ref[...] = jnp.where(lane == 0, q1, jnp.where(lane == 1, q2, 0.0)).T[0:V7X_SUBLANES, :].astype(jnp.int32)
    to_smem = pltpu.make_async_copy(qvm_ref, qsm_ref, sem)
    to_smem.start()
    q_ref[...] = qvm_ref[...]
    gt_ref[...] = jnp.where(lane == 0, g1, jnp.where(lane == 1, g2, 0.0)).T[0:V7X_SUBLANES, :]
    row8 = lax.broadcasted_iota(jnp.int32, cnt8.shape, 0)
    t_ref[...] = jnp.where(row8 == PLAN_LEN, cnt8, jnp.where(row8 == PLAN_OFF, off8, 0.0)).astype(jnp.int32)
    _to_slabs(xs_ref, x)
    to_smem.wait()

    def place_rows(t, c):
        tile = xs_ref[pl.ds(pl.multiple_of(t * V7X_SUBLANES, V7X_SUBLANES), V7X_SUBLANES), :]
        for k in range(TOP_K):
            p0 = pl.multiple_of(qsm_ref[k, t] * V7X_SUBLANES, V7X_SUBLANES)
            srt_ref[pl.ds(p0, V7X_SUBLANES), :] = tile
        return c

    lax.fori_loop(0, bm, place_rows, 0, unroll=8)


def _router(xt, w_router, bm):
    m, d = xt.shape
    nsteps = m // bm
    w_pad = jnp.pad(w_router, ((0, 0), (0, V7X_LANES - w_router.shape[1])))
    srows = TOP_K * bm * V7X_SUBLANES
    return pl.pallas_call(
        functools.partial(_router_kernel, bm=bm),
        grid=(nsteps,),
        in_specs=[pl.BlockSpec((bm, d), lambda i: (i, 0)),
                  pl.BlockSpec((d, V7X_LANES), lambda i: (0, 0))],
        out_specs=[pl.BlockSpec((None, V7X_SUBLANES, bm), lambda i: (i, 0, 0)),
                   pl.BlockSpec((None, V7X_SUBLANES, bm), lambda i: (i, 0, 0)),
                   pl.BlockSpec((None, V7X_SUBLANES, V7X_LANES), lambda i: (i, 0, 0)),
                   pl.BlockSpec((srows, V7X_LANES), lambda i: (i, 0))],
        out_shape=[jax.ShapeDtypeStruct((nsteps, V7X_SUBLANES, bm), jnp.int32),
                   jax.ShapeDtypeStruct((nsteps, V7X_SUBLANES, bm), F32),
                   jax.ShapeDtypeStruct((nsteps, V7X_SUBLANES, V7X_LANES), jnp.int32),
                   jax.ShapeDtypeStruct((nsteps * srows, V7X_LANES), F32)],
        scratch_shapes=[pltpu.VMEM((bm * V7X_SUBLANES, V7X_LANES), F32),
                        pltpu.VMEM((V7X_SUBLANES, bm), jnp.int32),
                        pltpu.SMEM((V7X_SUBLANES, bm), jnp.int32),
                        pltpu.SemaphoreType.DMA],
        compiler_params=_params(("parallel",)),
        name="moe_router",
    )(xt, w_pad)


def _experts_kernel(ie_ref, iv_ref, ni_ref, gcur_ref, gnext_ref, rows_hbm, wg_ref, wu_ref, wd_ref, o_ref,
                    xin_ref, xb_ref, acc_ref, sem, *, sb, rb, ntiles, max_run):
    i = pl.program_id(0)
    j = pl.program_id(1)
    last = pl.num_programs(1) - 1
    valid = iv_ref[i]
    used = i < ni_ref[0]

    def pieces(gref, slot, wait):
        for t in range(ntiles):
            _run_copies(rows_hbm, xin_ref.at[slot], gref[GATHER_SRC, t], gref[GATHER_DST, t], gref[GATHER_LEN, t],
                        sem.at[slot], max_run, wait)

    @pl.when(jnp.logical_not(used) & (j == last))
    def _():
        o_ref[...] = jnp.zeros_like(o_ref)

    @pl.when(used)
    def _():
        for slot in range(2):
            @pl.when((j == 0) & (i % 2 == slot))
            def _(slot=slot):
                if slot == 0:
                    @pl.when(i == 0)
                    def _():
                        xin_ref[...] = jnp.zeros_like(xin_ref)
                        pieces(gcur_ref, 0, False)

                pieces(gcur_ref, slot, True)
                xb_ref[...] = _from_slabs(xin_ref.at[slot], sb).astype(BF16)
                acc_ref[...] = jnp.zeros_like(acc_ref)

                @pl.when(i + 1 < ni_ref[0])
                def _():
                    pieces(gnext_ref, 1 - slot, False)

        def run(nrows):
            xb = xb_ref[0:nrows, :]
            h = _silu_mul(_dot(xb, wg_ref[...].astype(BF16)), _dot(xb, wu_ref[...].astype(BF16))).astype(BF16)
            acc_ref[0:nrows, :] += _dot(h, wd_ref[...].astype(BF16))

        for nrows in range(rb, sb + rb, rb):
            @pl.when((valid > nrows - rb) & (valid <= nrows))
            def _(nrows=nrows):
                run(nrows)

        @pl.when(j == last)
        def _():
            _to_slabs(o_ref, acc_ref[...])


def _experts(rows, gather, w_gu, w_down, layer, item_expert, item_valid, n_items, sb, rb, fc, ntiles, max_run):
    assert sb % rb == 0
    d = w_down.shape[3]
    f = w_down.shape[2]
    nf = f // fc
    n_max = item_expert.shape[0]
    ssb = sb * V7X_SUBLANES
    smem = pltpu.SMEM

    def item(i, ni):
        return jnp.minimum(i, ni[0] - 1)

    def chunk(i, j, ni):
        return jnp.where(i < ni[0], j, nf - 1)

    grid_spec = pltpu.PrefetchScalarGridSpec(
        num_scalar_prefetch=3,
        grid=(n_max, nf),
        in_specs=[pl.BlockSpec((None, V7X_SUBLANES, V7X_LANES), lambda i, j, ie, iv, ni: (item(i, ni), 0, 0), memory_space=smem),
                  pl.BlockSpec((None, V7X_SUBLANES, V7X_LANES),
                               lambda i, j, ie, iv, ni: (jnp.minimum(item(i, ni) + 1, n_max - 1), 0, 0), memory_space=smem),
                  pl.BlockSpec(memory_space=pl.ANY),
                  pl.BlockSpec((None, None, d, fc), lambda i, j, ie, iv, ni: (layer, ie[item(i, ni)], 0, chunk(i, j, ni))),
                  pl.BlockSpec((None, None, d, fc), lambda i, j, ie, iv, ni: (layer, ie[item(i, ni)], 0, nf + chunk(i, j, ni))),
                  pl.BlockSpec((None, None, fc, d), lambda i, j, ie, iv, ni: (layer, ie[item(i, ni)], chunk(i, j, ni), 0))],
        out_specs=pl.BlockSpec((ssb, V7X_LANES), lambda i, j, ie, iv, ni: (i, 0)),
        scratch_shapes=[pltpu.VMEM((2, ssb, V7X_LANES), F32), pltpu.VMEM((sb, d), BF16), pltpu.VMEM((sb, d), F32),
                        pltpu.SemaphoreType.DMA((2,))],
    )
    return pl.pallas_call(
        functools.partial(_experts_kernel, sb=sb, rb=rb, ntiles=ntiles, max_run=max_run),
        grid_spec=grid_spec,
        out_shape=jax.ShapeDtypeStruct((n_max * ssb, V7X_LANES), F32),
        compiler_params=_params(("arbitrary", "arbitrary")),
        name="moe_experts",
    )(item_expert, item_valid, n_items, gather, gather, rows, w_gu, w_gu, w_down)


def _combine_ln_kernel(q_ref, gt_ref, tcur_ref, tnext_ref, x_ref, g_ref, b_ref, y_hbm, o_ref, ybuf, mix_ref, sem, *, bm):
    i = pl.program_id(0)
    nsteps = pl.num_programs(0)

    def runs(tref, slot, wait):
        for e in range(N_EXPERTS):
            _run_copies(y_hbm, ybuf.at[slot], tref[PLAN_DST, e], tref[PLAN_OFF, e], tref[PLAN_LEN, e],
                        sem.at[slot], bm, wait)

    for slot in range(2):
        @pl.when(i % 2 == slot)
        def _(slot=slot):
            if slot == 0:
                @pl.when(i == 0)
                def _():
                    runs(tcur_ref, 0, False)

            @pl.when(i + 1 < nsteps)
            def _():
                runs(tnext_ref, 1 - slot, False)

            runs(tcur_ref, slot, True)

            def mix(t, c):
                p1 = pl.multiple_of(q_ref[0, t] * V7X_SUBLANES, V7X_SUBLANES)
                p2 = pl.multiple_of(q_ref[1, t] * V7X_SUBLANES, V7X_SUBLANES)
                t0 = pl.multiple_of(t * V7X_SUBLANES, V7X_SUBLANES)
                mix_ref[pl.ds(t0, V7X_SUBLANES), :] = (gt_ref[0, t] * ybuf[slot, pl.ds(p1, V7X_SUBLANES), :]
                                                       + gt_ref[1, t] * ybuf[slot, pl.ds(p2, V7X_SUBLANES), :])
                return c

            lax.fori_loop(0, bm, mix, 0, unroll=8)
            o_ref[...] = _layer_norm(ALPHA * x_ref[...] + _from_slabs(mix_ref, bm), g_ref[...], b_ref[...])


def _combine_ln(xt, plan_q, plan_g, plan_t, yrows, g, b, bm):
    m, d = xt.shape
    nsteps = m // bm
    smem = pltpu.SMEM
    return pl.pallas_call(
        functools.partial(_combine_ln_kernel, bm=bm),
        grid=(nsteps,),
        in_specs=[pl.BlockSpec((None, V7X_SUBLANES, bm), lambda i: (i, 0, 0), memory_space=smem),
                  pl.BlockSpec((None, V7X_SUBLANES, bm), lambda i: (i, 0, 0), memory_space=smem),
                  pl.BlockSpec((None, V7X_SUBLANES, V7X_LANES), lambda i: (i, 0, 0), memory_space=smem),
                  pl.BlockSpec((None, V7X_SUBLANES, V7X_LANES), lambda i: (jnp.minimum(i + 1, nsteps - 1), 0, 0),
                               memory_space=smem),
                  pl.BlockSpec((bm, d), lambda i: (i, 0)),
                  pl.BlockSpec((1, d), lambda i: (0, 0)),
                  pl.BlockSpec((1, d), lambda i: (0, 0)),
                  pl.BlockSpec(memory_space=pl.ANY)],
        out_specs=pl.BlockSpec((bm, d), lambda i: (i, 0)),
        out_shape=jax.ShapeDtypeStruct((m, d), F32),
        scratch_shapes=[pltpu.VMEM((2, TOP_K * bm * V7X_SUBLANES, V7X_LANES), F32),
                        pltpu.VMEM((bm * V7X_SUBLANES, V7X_LANES), F32),
                        pltpu.SemaphoreType.DMA((2,))],
        compiler_params=_params(("arbitrary",)),
        name="moe_combine_ln",
    )(plan_q, plan_g, plan_t, plan_t, xt, g.reshape(1, d), b.reshape(1, d), yrows)


def _moe_layer(xt, layer, w_router, w_gu, w_down, g, b):
    m, d = xt.shape
    sb = _row_block(m, MOE_SUPER_ROWS)
    rb = _row_block(sb, MOE_ROW_BLOCK)
    bm = _row_block(m, ROUTE_ROWS)
    ntiles = m // bm
    assert ntiles <= V7X_LANES
    plan_q, plan_g, plan_t, rows = _router(xt, w_router[layer], bm)
    lens = plan_t[:, PLAN_LEN, :N_EXPERTS]
    offs = plan_t[:, PLAN_OFF, :N_EXPERTS]
    counts = jnp.sum(lens, axis=0)
    nsb = (counts + sb - 1) // sb
    iend = jnp.cumsum(nsb)
    n_max = (m * TOP_K) // sb + N_EXPERTS
    starts = ((iend - nsb) * sb).astype(jnp.int32)
    run_lo = jnp.cumsum(lens, axis=0) - lens
    plan_t = plan_t.at[:, PLAN_DST, :N_EXPERTS].set(starts[None, :] + run_lo)
    item = jnp.arange(n_max, dtype=jnp.int32)
    item_expert = jnp.minimum(jnp.sum(item[:, None] >= iend[None, :], axis=1), N_EXPERTS - 1).astype(jnp.int32)
    item_lo = item * sb - starts[item_expert]
    item_valid = jnp.clip(counts[item_expert] - item_lo, 0, sb).astype(jnp.int32)
    n_items = iend[-1:].astype(jnp.int32)
    r_lo = run_lo[:, item_expert].T
    r_len = lens[:, item_expert].T
    lo = jnp.maximum(item_lo[:, None], r_lo)
    hi = jnp.minimum(item_lo[:, None] + sb, r_lo + r_len)
    tile_row0 = (jnp.arange(ntiles, dtype=jnp.int32) * (TOP_K * bm))[None, :]
    gather = jnp.zeros((n_max, V7X_SUBLANES, V7X_LANES), jnp.int32)
    gather = gather.at[:, GATHER_LEN, :ntiles].set(jnp.maximum(hi - lo, 0))
    gather = gather.at[:, GATHER_SRC, :ntiles].set(tile_row0 + offs[:, item_expert].T + (lo - r_lo))
    gather = gather.at[:, GATHER_DST, :ntiles].set(lo - item_lo[:, None])
    yrows = _experts(rows, gather, w_gu, w_down, layer, item_expert, item_valid, n_items, sb, rb, MOE_FF_CHUNK,
                     ntiles, min(bm, sb))
    return _combine_ln(xt, plan_q, plan_g, plan_t, yrows, g, b, bm)


def kernel(x, lru_w_in, lru_conv_w, lru_conv_b, lru_w_gates, lru_b_gates, lru_lambda, lru_w_out, pool_w, pool_scale,
           fox_w_qkvf, fox_b_f, fox_w_o, ffn_w_gu, ffn_w_down, moe_router, moe_w_gu, moe_w_down,
           ln_mix_g, ln_mix_b, ln_ffn_g, ln_ffn_b):
    nbat, s, d = x.shape
    xt = x.reshape(nbat * s, d)
    for i in range(DEPTH):
        mixer, j = i % 3, i // 3
        if mixer == 0:
            xt = _rglru_layer(xt, nbat, s, j, lru_w_in, lru_conv_w, lru_conv_b, lru_w_gates, lru_b_gates,
                              lru_lambda, lru_w_out, ln_mix_g[i], ln_mix_b[i])
        elif mixer == 1:
            xt = _pool_layer(xt, nbat, s, j, pool_w, pool_scale, ln_mix_g[i], ln_mix_b[i])
        else:
            xt = _fox_layer(xt, nbat, s, j, fox_w_qkvf, fox_b_f, fox_w_o, ln_mix_g[i], ln_mix_b[i])
        if i % 2 == 0:
            xt = _ffn_ln(xt, ffn_w_gu, ffn_w_down, i // 2, ln_ffn_g[i], ln_ffn_b[i], _row_block(nbat * s, ROW_TILE), FFN_CHUNK)
        else:
            xt = _moe_layer(xt, i // 2, moe_router, moe_w_gu, moe_w_down, ln_ffn_g[i], ln_ffn_b[i])
    return xt.reshape(nbat, s, d)
```
